```python
import jax, jax.numpy as jnp
from jax import lax
import numpy as np

D_MODEL = 2048
BATCH = 2
SEQ = 4096
DEPTH = 2
DEC_BATCH = 32
DEC_SEQ = 1
PAST_LEN = 8192
PAGE_SIZE = 128

N_MIX_LAYERS = (DEPTH + 1) // 2
N_ATTN_LAYERS = DEPTH // 2
M_HEADS = 8
M_HEAD_DIM = 256
M_WIDTH = M_HEADS * M_HEAD_DIM
CONV_WIDTH = 4
M_CHUNK = 128
R_HEAD_DIM = 64
R_WIDTH = D_MODEL
R_HEADS = R_WIDTH // R_HEAD_DIM
R_DECAY_LORA = 64
R_A_LORA = 64
F_HEAD_DIM = 128
F_WIDTH = D_MODEL
F_HEADS = F_WIDTH // F_HEAD_DIM
Q_BLOCK = 128
PLE_DIM = 256
NORM_EPS = 1e-6
RWKV_GN_EPS = 64e-5
M_SPLITS = (2 * M_WIDTH, M_WIDTH, M_HEADS, M_HEADS, M_WIDTH, M_WIDTH)
R_SPLITS = (R_WIDTH, R_WIDTH, R_WIDTH, R_DECAY_LORA, R_A_LORA, R_WIDTH)
F_SPLITS = (F_WIDTH, F_WIDTH, F_WIDTH, F_HEADS, F_WIDTH)
M_COLS = sum(M_SPLITS)
R_COLS = sum(R_SPLITS)
MIX_COLS = M_COLS + R_COLS
FOX_COLS = sum(F_SPLITS)

kernel_name = 'hybrid_mlstm_rwkv7_fox_decode_step'


def _split(x, sizes):
    cuts = [int(c) for c in np.cumsum(sizes)[:-1]]
    return jnp.split(x, cuts, axis=-1)


def rmsnorm(x, g, eps=NORM_EPS):
    xf = x.astype(jnp.float32)
    y = xf * lax.rsqrt(jnp.mean(xf * xf, axis=-1, keepdims=True) + eps)
    return (y * g.astype(jnp.float32)).astype(x.dtype)


def causal_conv(x, buf, w, b):
    T = x.shape[1]
    xp = jnp.concatenate([buf.astype(x.dtype), x], axis=1)
    y = b
    for j in range(CONV_WIDTH):
        y = y + xp[:, j:j + T] * w[j]
    return y, xp[:, xp.shape[1] - (CONV_WIDTH - 1):]


def mlstm_chunked(q, k, v, i_pre, f_pre, C0, n0, m0):
    f32 = jnp.float32
    B, T, H, Dh = q.shape
    L = M_CHUNK if T % M_CHUNK == 0 else T
    nc = T // L
    hv = lambda t: t.astype(f32).reshape(B, nc, L, H, Dh).transpose(1, 0, 3, 2, 4)
    hg = lambda t: t.astype(f32).reshape(B, nc, L, H).transpose(1, 0, 3, 2)
    causal = jnp.tril(jnp.ones((L, L), dtype=bool))

    def step(carry, xs):
        C, n, m = carry
        qc, kc, vc, li, lf = xs
        b = jnp.cumsum(lf, axis=-1)
        dmat = jnp.where(causal, b[..., :, None] - b[..., None, :] + li[..., None, :], -jnp.inf)
        inter = b + m[..., None]
        m_t = jnp.maximum(inter, jnp.max(dmat, axis=-1))
        w_intra = jnp.exp(dmat - m_t[..., None])
        w_inter = jnp.exp(inter - m_t)
        s = jnp.einsum('bhtd,bhsd->bhts', qc, kc) * w_intra
        num = jnp.einsum('bhts,bhsd->bhtd', s, vc) + w_inter[..., None] * jnp.einsum('bhvk,bhtk->bhtv', C, qc)
        den = jnp.sum(s, axis=-1) + w_inter * jnp.einsum('bhk,bhtk->bht', n, qc)
        h = num / jnp.maximum(jnp.abs(den), jnp.exp(-m_t))[..., None]
        m_new = m_t[..., -1]
        carry_decay = jnp.exp(b[..., -1] + m - m_new)
        w_row = jnp.exp(b[..., -1:] - b + li - m_new[..., None])
        C_new = carry_decay[..., None, None] * C + jnp.einsum('bhs,bhsv,bhsk->bhvk', w_row, vc, kc)
        n_new = carry_decay[..., None] * n + jnp.einsum('bhs,bhsk->bhk', w_row, kc)
        return (C_new, n_new, m_new), h

    init = (C0.astype(f32), n0.astype(f32), m0.astype(f32))
    xs = (hv(q), hv(k), hv(v), hg(i_pre), hg(jax.nn.log_sigmoid(f_pre.astype(f32))))
    (C, n, m), h = lax.scan(step, init, xs)
    return h.transpose(1, 0, 3, 2, 4).reshape(B, T, H, Dh), C, n, m


def rwkv7_scan(r, w, k, v, kk, a, S0):
    tm = lambda t: jnp.moveaxis(t, 1, 0)

    def step(S, xs):
        r_t, w_t, k_t, v_t, kk_t, a_t = xs
        sa = jnp.einsum('bhvk,bhk->bhv', S, -kk_t)
        S = S * w_t[:, :, None, :] + sa[..., None] * (kk_t * a_t)[:, :, None, :] + v_t[..., None] * k_t[:, :, None, :]
        return S, jnp.einsum('bhvk,bhk->bhv', S, r_t)

    S, ys = lax.scan(step, S0.astype(jnp.float32), tuple(tm(t) for t in (r, w, k, v, kk, a)))
    return jnp.moveaxis(ys, 0, 1), S


def fox_prompt(q, k, v, lf):
    f32 = jnp.float32
    B, S, H, Dh = q.shape
    blk = Q_BLOCK if S % Q_BLOCK == 0 else S
    nb = S // blk
    c = jnp.swapaxes(jnp.cumsum(lf.astype(f32), axis=1), 1, 2)
    qb = jnp.moveaxis(q.reshape(B, nb, blk, H, Dh), 1, 0)
    cb = jnp.moveaxis(c.reshape(B, H, nb, blk), 2, 0)
    kpos = jnp.arange(S)
    vf = v.astype(f32)

    def one_block(args):
        qi, ci, start = args
        s = jnp.einsum('bqhd,bkhd->bhqk', qi, k, preferred_element_type=f32) * (Dh ** -0.5)
        s = s + (ci[..., :, None] - c[..., None, :])
        qpos = start + jnp.arange(blk)
        s = jnp.where(kpos[None, :] <= qpos[:, None], s, -jnp.inf)
        p = jax.nn.softmax(s, axis=-1)
        return jnp.einsum('bhqk,bkhd->bqhd', p, vf)

    o = lax.map(one_block, (qb, cb, jnp.arange(nb) * blk))
    return jnp.moveaxis(o, 0, 1).reshape(B, S, H, Dh).astype(q.dtype)


def fox_sample(q, k, v, lf, k_pool, v_pool, lf_pool, page_table, j):
    f32 = jnp.float32
    B, T, H, Dh = q.shape
    n_pages = page_table.shape[1]
    page = k_pool.shape[2]
    past = n_pages * page
    scale = Dh ** -0.5
    lf_past = lf_pool[j, page_table].reshape(B, past, H).astype(f32)
    c = jnp.cumsum(jnp.concatenate([lf_past, lf.astype(f32)], axis=1), axis=1)
    c_past = c[:, :past]
    cq = jnp.swapaxes(c[:, past:], 1, 2)

    def attend_block(carry, kb, vb, bias):
        m, l, acc = carry
        s = jnp.einsum('bthd,bshd->bhts', q, kb, preferred_element_type=f32) * scale + bias
        m_new = jnp.maximum(m, jnp.max(s, axis=-1))
        alpha = jnp.exp(m - m_new)
        pr = jnp.exp(s - m_new[..., None])
        acc = acc * alpha[..., None] + jnp.einsum('bhts,bshd->bhtd', pr, vb.astype(f32))
        return (m_new, l * alpha + jnp.sum(pr, axis=-1), acc)

    def page_step(carry, xs):
        phys, cp = xs
        bias = cq[..., :, None] - jnp.swapaxes(cp, 1, 2)[:, :, None, :]
        return attend_block(carry, k_pool[j, phys], v_pool[j, phys], bias), None

    init = (jnp.full((B, H, T), -jnp.inf, f32), jnp.zeros((B, H, T), f32), jnp.zeros((B, H, T, Dh), f32))
    xs = (page_table.T, jnp.moveaxis(c_past.reshape(B, n_pages, page, H), 1, 0))
    carry, _ = lax.scan(page_step, init, xs)
    causal = jnp.tril(jnp.ones((T, T), dtype=bool))
    bias_self = jnp.where(causal, cq[..., :, None] - cq[..., None, :], -jnp.inf)
    m, l, acc = attend_block(carry, k, v, bias_self)
    return jnp.swapaxes(acc / l[..., None], 1, 2).astype(q.dtype)


def mix_layer(xn, state, W, j):
    f32 = jnp.float32
    conv0, C0, n0, m0, S0, sh0 = state
    B, T, _ = xn.shape
    proj = xn @ W['w_in_mix'][j]
    mp, rp = proj[..., :M_COLS], proj[..., M_COLS:]
    mqk, mv, mi, mf, mo, mz = _split(mp, M_SPLITS)
    qk, conv_new = causal_conv(mqk, conv0, W['m_conv_w'][j], W['m_conv_b'][j])
    qk = jax.nn.silu(qk)
    q = qk[..., :M_WIDTH].reshape(B, T, M_HEADS, M_HEAD_DIM)
    k = qk[..., M_WIDTH:].reshape(B, T, M_HEADS, M_HEAD_DIM) * (M_HEAD_DIM ** -0.5)
    v = mv.reshape(B, T, M_HEADS, M_HEAD_DIM)
    h, C, n, m = mlstm_chunked(q, k, v, mi + W['m_b_i'][j], mf + W['m_b_f'][j], C0, n0, m0)
    h = rmsnorm(h, W['m_norm'][j].reshape(M_HEADS, M_HEAD_DIM)).reshape(B, T, M_WIDTH)
    branch_m = h * jax.nn.sigmoid(mo.astype(f32)) * jax.nn.silu(mz.astype(f32))
    prev = jnp.concatenate([sh0[:, None].astype(rp.dtype), rp[:, :-1]], axis=1)
    u = rp + W['r_mu'][j] * (prev - rp)
    sh_new = rp[:, -1]
    rr, rk, rv, rw, ra, rz = _split(u, R_SPLITS)
    w_log = -jax.nn.softplus(-(W['r_w0'][j] + jnp.tanh(rw) @ W['r_w2'][j]).astype(f32)) - 0.5
    decay = jnp.exp(-jnp.exp(w_log))
    a = jax.nn.sigmoid((W['r_a0'][j] + ra @ W['r_a2'][j]).astype(f32))
    hs = lambda t: t.astype(f32).reshape(B, T, R_HEADS, R_HEAD_DIM)
    ph = lambda t: t.astype(f32).reshape(R_HEADS, R_HEAD_DIM)
    r_, k_, v_, w_, a_ = hs(rr), hs(rk), hs(rv), hs(decay), hs(a)
    kk = k_ * ph(W['r_k_k'][j])
    kk = kk * lax.rsqrt(jnp.maximum(jnp.sum(kk * kk, axis=-1, keepdims=True), 1e-24))
    k_ = k_ * (1.0 + (a_ - 1.0) * ph(W['r_k_a'][j]))
    y, S = rwkv7_scan(r_, w_, k_, v_, kk, a_, S0)
    mu_y = jnp.mean(y, axis=-1, keepdims=True)
    var_y = jnp.mean(jnp.square(y - mu_y), axis=-1, keepdims=True)
    y = (y - mu_y) * lax.rsqrt(var_y + RWKV_GN_EPS) * ph(W['r_gn_w'][j]) + ph(W['r_gn_b'][j])
    y = y + jnp.sum(r_ * k_ * ph(W['r_r_k'][j]), axis=-1, keepdims=True) * v_
    branch_r = y.reshape(B, T, R_WIDTH) * jax.nn.silu(rz.astype(f32))
    out = jnp.concatenate([branch_m, branch_r], axis=-1).astype(xn.dtype) @ W['w_out_mix'][j]
    return out, (conv_new, C, n, m, S, sh_new)


def fox_project(xn, W, j):
    B, T, _ = xn.shape
    proj = xn @ W['w_in_fox'][j]
    q, k, v, fp, z = _split(proj, F_SPLITS)
    q = rmsnorm(q.reshape(B, T, F_HEADS, F_HEAD_DIM), W['f_q_norm'][j])
    k = rmsnorm(k.reshape(B, T, F_HEADS, F_HEAD_DIM), W['f_k_norm'][j])
    v = v.reshape(B, T, F_HEADS, F_HEAD_DIM)
    lf = jax.nn.log_sigmoid((fp + W['f_b_f'][j]).astype(jnp.float32))
    return q, k, v, lf, z


def fox_output(o, z, W, j):
    B, T = o.shape[:2]
    g = o.reshape(B, T, F_WIDTH).astype(jnp.float32) * jax.nn.silu(z.astype(jnp.float32))
    return g.astype(z.dtype) @ W['w_out_fox'][j]


def ple_add(h, p_i, W, li):
    g = jax.nn.sigmoid((rmsnorm(h, W['ple_norm'][li]) @ W['ple_gate_w'][li]).astype(jnp.float32))
    e = (p_i @ W['ple_proj'][li]).astype(jnp.float32)
    return (h.astype(jnp.float32) + g * e).astype(h.dtype)


def trunk(x, p, mix_state, attn_fn, W):
    h = x
    mix_new, attn_new = [], []
    for li in range(DEPTH):
        j = li // 2
        xn = rmsnorm(h, W['norm_w'][li])
        if li % 2 == 0:
            out, st = mix_layer(xn, tuple(s[j] for s in mix_state), W, j)
            mix_new.append(st)
        else:
            q, k, v, lf, z = fox_project(xn, W, j)
            out = fox_output(attn_fn(q, k, v, lf, j), z, W, j)
            attn_new.append((k, v, lf))
        h = h + out.astype(h.dtype)
        h = ple_add(h, p[li], W, li)
    y = rmsnorm(h, W['final_norm'])
    stack = lambda groups: tuple(jnp.stack([g[c] for g in groups]).astype(x.dtype) for c in range(len(groups[0])))
    return y, stack(attn_new), stack(mix_new)


def setup_inputs(seed: int = 0) -> dict:
    key = jax.random.key(seed)
    ks = iter(jax.random.split(key, 64))
    f32 = jnp.float32
    nrm = lambda shape, scale: jax.random.normal(next(ks), shape, f32) * scale
    unif = lambda shape, lo, hi: jax.random.uniform(next(ks), shape, f32, lo, hi)
    n_pages = PAST_LEN // PAGE_SIZE
    n_used = DEC_BATCH * n_pages
    n_pool = n_used + max(1, n_used // 4)
    D = D_MODEL
    inp = {}
    inp['x_prompt'] = nrm((BATCH, SEQ, D), 1.0)
    inp['x_sample'] = nrm((DEC_BATCH, DEC_SEQ, D), 1.0)
    inp['cache_k'] = nrm((N_ATTN_LAYERS, n_pool, PAGE_SIZE, F_HEADS, F_HEAD_DIM), 1.0)
    inp['cache_v'] = nrm((N_ATTN_LAYERS, n_pool, PAGE_SIZE, F_HEADS, F_HEAD_DIM), 1.0)
    inp['cache_lf'] = jax.nn.log_sigmoid(3.0 + nrm((N_ATTN_LAYERS, n_pool, PAGE_SIZE, F_HEADS), 0.5))
    inp['state_mlstm_conv'] = nrm((N_MIX_LAYERS, DEC_BATCH, CONV_WIDTH - 1, 2 * M_WIDTH), 1.0)
    inp['state_mlstm_C'] = nrm((N_MIX_LAYERS, DEC_BATCH, M_HEADS, M_HEAD_DIM, M_HEAD_DIM), 0.5)
    inp['state_mlstm_n'] = jnp.abs(nrm((N_MIX_LAYERS, DEC_BATCH, M_HEADS, M_HEAD_DIM), 0.5))
    inp['state_mlstm_m'] = unif((N_MIX_LAYERS, DEC_BATCH, M_HEADS), 0.0, 2.0)
    inp['state_rwkv_S'] = nrm((N_MIX_LAYERS, DEC_BATCH, R_HEADS, R_HEAD_DIM, R_HEAD_DIM), 0.3)
    inp['state_rwkv_shift'] = nrm((N_MIX_LAYERS, DEC_BATCH, R_COLS), 1.0)
    inp['page_table'] = jax.random.permutation(next(ks), n_pool)[:n_used].reshape(DEC_BATCH, n_pages).astype(jnp.int32)
    inp['p_prompt'] = nrm((DEPTH, BATCH, SEQ, PLE_DIM), 1.0)
    inp['p_sample'] = nrm((DEPTH, DEC_BATCH, DEC_SEQ, PLE_DIM), 1.0)
    inp['norm_w'] = 1.0 + nrm((DEPTH, D), 0.02)
    inp['final_norm'] = 1.0 + nrm((D,), 0.02)
    inp['w_in_mix'] = nrm((N_MIX_LAYERS, D, MIX_COLS), D ** -0.5)
    inp['w_out_mix'] = nrm((N_MIX_LAYERS, M_WIDTH + R_WIDTH, D), (M_WIDTH + R_WIDTH) ** -0.5)
    inp['m_conv_w'] = nrm((N_MIX_LAYERS, CONV_WIDTH, 2 * M_WIDTH), 0.5)
    inp['m_conv_b'] = nrm((N_MIX_LAYERS, 2 * M_WIDTH), 0.02)
    inp['m_b_i'] = nrm((N_MIX_LAYERS, M_HEADS), 0.1)
    inp['m_b_f'] = unif((N_MIX_LAYERS, M_HEADS), 3.0, 6.0)
    inp['m_norm'] = 1.0 + nrm((N_MIX_LAYERS, M_WIDTH), 0.02)
    inp['r_mu'] = unif((N_MIX_LAYERS, R_COLS), 0.0, 1.0)
    inp['r_w0'] = unif((N_MIX_LAYERS, R_WIDTH), -5.0, 0.0)
    inp['r_w2'] = nrm((N_MIX_LAYERS, R_DECAY_LORA, R_WIDTH), 0.5 * R_DECAY_LORA ** -0.5)
    inp['r_a0'] = nrm((N_MIX_LAYERS, R_WIDTH), 0.1)
    inp['r_a2'] = nrm((N_MIX_LAYERS, R_A_LORA, R_WIDTH), 0.5 * R_A_LORA ** -0.5)
    inp['r_k_k'] = 1.0 + nrm((N_MIX_LAYERS, R_WIDTH), 0.1)
    inp['r_k_a'] = 1.0 + nrm((N_MIX_LAYERS, R_WIDTH), 0.1)
    inp['r_r_k'] = nrm((N_MIX_LAYERS, R_WIDTH), 0.1)
    inp['r_gn_w'] = 1.0 + nrm((N_MIX_LAYERS, R_WIDTH), 0.02)
    inp['r_gn_b'] = nrm((N_MIX_LAYERS, R_WIDTH), 0.02)
    inp['w_in_fox'] = nrm((N_ATTN_LAYERS, D, FOX_COLS), D ** -0.5)
    inp['w_out_fox'] = nrm((N_ATTN_LAYERS, F_WIDTH, D), F_WIDTH ** -0.5)
    inp['f_b_f'] = 3.0 + nrm((N_ATTN_LAYERS, F_HEADS), 0.5)
    inp['f_q_norm'] = 1.0 + nrm((N_ATTN_LAYERS, F_HEAD_DIM), 0.02)
    inp['f_k_norm'] = 1.0 + nrm((N_ATTN_LAYERS, F_HEAD_DIM), 0.02)
    inp['ple_proj'] = nrm((DEPTH, PLE_DIM, D), PLE_DIM ** -0.5)
    inp['ple_gate_w'] = nrm((DEPTH, D, D), D ** -0.5)
    inp['ple_norm'] = 1.0 + nrm((DEPTH, D), 0.02)
    return inp


def reference(x_prompt, x_sample, cache_k, cache_v, cache_lf, state_mlstm_conv, state_mlstm_C, state_mlstm_n,
              state_mlstm_m, state_rwkv_S, state_rwkv_shift, page_table, p_prompt, p_sample,
              norm_w, final_norm, w_in_mix, w_out_mix, m_conv_w, m_conv_b, m_b_i, m_b_f, m_norm,
              r_mu, r_w0, r_w2, r_a0, r_a2, r_k_k, r_k_a, r_r_k, r_gn_w, r_gn_b,
              w_in_fox, w_out_fox, f_b_f, f_q_norm, f_k_norm, ple_proj, ple_gate_w, ple_norm):
    W = dict(norm_w=norm_w, final_norm=final_norm, w_in_mix=w_in_mix, w_out_mix=w_out_mix,
             m_conv_w=m_conv_w, m_conv_b=m_conv_b, m_b_i=m_b_i, m_b_f=m_b_f, m_norm=m_norm,
             r_mu=r_mu, r_w0=r_w0, r_w2=r_w2, r_a0=r_a0, r_a2=r_a2, r_k_k=r_k_k, r_k_a=r_k_a,
             r_r_k=r_r_k, r_gn_w=r_gn_w, r_gn_b=r_gn_b, w_in_fox=w_in_fox, w_out_fox=w_out_fox,
             f_b_f=f_b_f, f_q_norm=f_q_norm, f_k_norm=f_k_norm, ple_proj=ple_proj,
             ple_gate_w=ple_gate_w, ple_norm=ple_norm)
    Bp = x_prompt.shape[0]
    dt = x_prompt.dtype
    fresh_state = (jnp.zeros((N_MIX_LAYERS, Bp, CONV_WIDTH - 1, 2 * M_WIDTH), dt),
                   jnp.zeros((N_MIX_LAYERS, Bp, M_HEADS, M_HEAD_DIM, M_HEAD_DIM), dt),
                   jnp.zeros((N_MIX_LAYERS, Bp, M_HEADS, M_HEAD_DIM), dt),
                   jnp.zeros((N_MIX_LAYERS, Bp, M_HEADS), dt),
                   jnp.zeros((N_MIX_LAYERS, Bp, R_HEADS, R_HEAD_DIM, R_HEAD_DIM), dt),
                   jnp.zeros((N_MIX_LAYERS, Bp, R_COLS), dt))
    y_prompt, (k_p, v_p, lf_p), (conv_p, C_p, n_p, m_p, S_p, sh_p) = trunk(
        x_prompt, p_prompt, fresh_state, lambda q, k, v, lf, j: fox_prompt(q, k, v, lf), W)
    past_state = (state_mlstm_conv, state_mlstm_C, state_mlstm_n, state_mlstm_m, state_rwkv_S, state_rwkv_shift)
    y_sample, (k_s, v_s, lf_s), (conv_s, C_s, n_s, m_s, S_s, sh_s) = trunk(
        x_sample, p_sample, past_state,
        lambda q, k, v, lf, j: fox_sample(q, k, v, lf, cache_k, cache_v, cache_lf, page_table, j), W)
    return (y_prompt, y_sample, k_p, v_p, lf_p, conv_p, C_p, n_p, m_p, S_p, sh_p,
            k_s, v_s, lf_s, conv_s, C_s, n_s, m_s, S_s, sh_s)
```

```python
import functools

import jax
import jax.numpy as jnp
from jax import lax
from jax.experimental import pallas as pl
from jax.experimental.pallas import tpu as pltpu

F32 = jnp.float32
BF16 = jnp.bfloat16
HI = lax.Precision.HIGHEST

NORM_EPS = 1e-6
RWKV_GN_EPS = 64e-5
M_HEADS = 8
M_HEAD_DIM = 256
M_WIDTH = M_HEADS * M_HEAD_DIM
CONV_WIDTH = 4
M_CHUNK = 128
R_HEAD_DIM = 64
R_LORA = 64
R_CHUNK = 64
F_HEAD_DIM = 128
PAGE = 128
LANES = 128
SMALL_W = 512
VMEM_LIMIT = 48 * 1024 * 1024


def _cp(sem, vmem=VMEM_LIMIT):
    return pltpu.CompilerParams(dimension_semantics=sem, vmem_limit_bytes=vmem)


def _dot(a, b, prec=None):
    return jnp.dot(a, b, preferred_element_type=F32, precision=prec)


def _dot_nt(a, b, prec=None):
    return lax.dot_general(a, b, (((1,), (1,)), ((), ())), preferred_element_type=F32, precision=prec)


def _dot_tn(a, b, prec=None):
    return lax.dot_general(a, b, (((0,), (0,)), ((), ())), preferred_element_type=F32, precision=prec)


def _softplus(y):
    return jnp.maximum(y, 0.0) + jnp.log1p(jnp.exp(-jnp.abs(y)))


def _log_sigmoid(x):
    return -_softplus(-x)


def _silu(x):
    return x * jax.nn.sigmoid(x)


def _eye(n):
    return (lax.broadcasted_iota(jnp.int32, (n, n), 0) == lax.broadcasted_iota(jnp.int32, (n, n), 1)).astype(F32)


def _norm_mm_kernel(x_ref, g_ref, w_ref, o_ref, xn_ref):
    @pl.when(pl.program_id(1) == 0)
    def _():
        x = x_ref[...]
        ms = jnp.mean(x * x, axis=-1, keepdims=True)
        xn_ref[...] = (x * lax.rsqrt(ms + NORM_EPS) * g_ref[...]).astype(BF16)

    o_ref[...] = _dot(xn_ref[...], w_ref[...])


def norm_matmul(x, g, w, tm, tn):
    M, K = x.shape
    N = w.shape[1]
    return pl.pallas_call(
        _norm_mm_kernel,
        grid=(M // tm, N // tn),
        in_specs=[pl.BlockSpec((tm, K), lambda i, j: (i, 0)),
                  pl.BlockSpec((1, K), lambda i, j: (0, 0)),
                  pl.BlockSpec((K, tn), lambda i, j: (0, j))],
        out_specs=pl.BlockSpec((tm, tn), lambda i, j: (i, j)),
        out_shape=jax.ShapeDtypeStruct((M, N), F32),
        scratch_shapes=[pltpu.VMEM((tm, K), BF16)],
        compiler_params=_cp(("parallel", "arbitrary")),
        name="norm_matmul",
    )(x, g.reshape(1, K), w)


def _out_mix_kernel(a_ref, b_ref, wa_ref, wb_ref, res_ref, o_ref, abf, bbf):
    @pl.when(pl.program_id(1) == 0)
    def _():
        abf[...] = a_ref[...].astype(BF16)
        bbf[...] = b_ref[...].astype(BF16)

    o_ref[...] = res_ref[...] + (_dot(abf[...], wa_ref[...]) + _dot(bbf[...], wb_ref[...]))


def out_mix(a, b, w, res, tm, tn):
    M, K = a.shape
    N = w.shape[1]
    return pl.pallas_call(
        _out_mix_kernel,
        grid=(M // tm, N // tn),
        in_specs=[pl.BlockSpec((tm, K), lambda i, j: (i, 0)),
                  pl.BlockSpec((tm, K), lambda i, j: (i, 0)),
                  pl.BlockSpec((K, tn), lambda i, j: (0, j)),
                  pl.BlockSpec((K, tn), lambda i, j: (1, j)),
                  pl.BlockSpec((tm, tn), lambda i, j: (i, j))],
        out_specs=pl.BlockSpec((tm, tn), lambda i, j: (i, j)),
        out_shape=jax.ShapeDtypeStruct((M, N), F32),
        scratch_shapes=[pltpu.VMEM((tm, K), BF16), pltpu.VMEM((tm, K), BF16)],
        compiler_params=_cp(("parallel", "arbitrary")),
        name="out_mix",
    )(a, b, w, w, res)


def _out_fox_kernel(o_in_ref, z_ref, w_ref, res_ref, o_ref, gbf):
    @pl.when(pl.program_id(1) == 0)
    def _():
        gbf[...] = (o_in_ref[...] * _silu(z_ref[...])).astype(BF16)

    o_ref[...] = res_ref[...] + _dot(gbf[...], w_ref[...])


def out_fox(o, proj, z_blk, w, res, tm, tn):
    M, K = o.shape
    N = w.shape[1]
    return pl.pallas_call(
        _out_fox_kernel,
        grid=(M // tm, N // tn),
        in_specs=[pl.BlockSpec((tm, K), lambda i, j: (i, 0)),
                  pl.BlockSpec((tm, K), lambda i, j: (i, z_blk)),
                  pl.BlockSpec((K, tn), lambda i, j: (0, j)),
                  pl.BlockSpec((tm, tn), lambda i, j: (i, j))],
        out_specs=pl.BlockSpec((tm, tn), lambda i, j: (i, j)),
        out_shape=jax.ShapeDtypeStruct((M, N), F32),
        scratch_shapes=[pltpu.VMEM((tm, K), BF16)],
        compiler_params=_cp(("parallel", "arbitrary")),
        name="out_fox",
    )(o, proj, w, res)


def _ple_kernel(h_ref, g_ref, wg_ref, p_ref, pw_ref, ht_ref, o_ref, hn_ref):
    @pl.when(pl.program_id(1) == 0)
    def _():
        x = h_ref[...]
        ms = jnp.mean(x * x, axis=-1, keepdims=True)
        hn_ref[...] = (x * lax.rsqrt(ms + NORM_EPS) * g_ref[...]).astype(BF16)

    gate = jax.nn.sigmoid(_dot(hn_ref[...], wg_ref[...]))
    e = _dot(p_ref[...].astype(BF16), pw_ref[...])
    o_ref[...] = ht_ref[...] + gate * e


def ple_add(h, g, wg, p, pw, tm, tn):
    M, K = h.shape
    N = wg.shape[1]
    P = p.shape[1]
    return pl.pallas_call(
        _ple_kernel,
        grid=(M // tm, N // tn),
        in_specs=[pl.BlockSpec((tm, K), lambda i, j: (i, 0)),
                  pl.BlockSpec((1, K), lambda i, j: (0, 0)),
                  pl.BlockSpec((K, tn), lambda i, j: (0, j)),
                  pl.BlockSpec((tm, P), lambda i, j: (i, 0)),
                  pl.BlockSpec((P, tn), lambda i, j: (0, j)),
                  pl.BlockSpec((tm, tn), lambda i, j: (i, j))],
        out_specs=pl.BlockSpec((tm, tn), lambda i, j: (i, j)),
        out_shape=jax.ShapeDtypeStruct((M, N), F32),
        scratch_shapes=[pltpu.VMEM((tm, K), BF16)],
        compiler_params=_cp(("parallel", "arbitrary")),
        name="ple_add",
    )(h, g.reshape(1, K), wg, p, pw, h)


def _rmsnorm_kernel(x_ref, g_ref, o_ref):
    x = x_ref[...]
    ms = jnp.mean(x * x, axis=-1, keepdims=True)
    o_ref[...] = x * lax.rsqrt(ms + NORM_EPS) * g_ref[...]


def rmsnorm_rows(x, g, tm):
    M, K = x.shape
    return pl.pallas_call(
        _rmsnorm_kernel,
        grid=(M // tm,),
        in_specs=[pl.BlockSpec((tm, K), lambda i: (i, 0)), pl.BlockSpec((1, K), lambda i: (0, 0))],
        out_specs=pl.BlockSpec((tm, K), lambda i: (i, 0)),
        out_shape=jax.ShapeDtypeStruct((M, K), F32),
        compiler_params=_cp(("parallel",)),
        name="final_norm",
    )(x, g.reshape(1, K))


def _mlstm_chunk_kernel(qp_ref, kp_ref, v_ref, mo_ref, mz_ref, gir_ref, gfr_ref, gic_ref, gfc_ref,
                        cwq_ref, cwk_ref, cbq_ref, cbk_ref, bi_ref, bf_ref, mn_ref,
                        out_ref, C_out, n_out, m_out,
                        C_s, n_s, m_s, qbuf, kbuf):
    c = pl.program_id(2)
    nc = pl.num_programs(2)
    L = qp_ref.shape[0]
    Dh = qp_ref.shape[1]

    @pl.when(c == 0)
    def _():
        C_s[...] = jnp.zeros(C_s.shape, F32)
        n_s[...] = jnp.zeros(n_s.shape, F32)
        m_s[...] = jnp.zeros(m_s.shape, F32)
        qbuf[0:8, :] = jnp.zeros((8, Dh), F32)
        kbuf[0:8, :] = jnp.zeros((8, Dh), F32)

    qbuf[8:8 + L, :] = qp_ref[...]
    kbuf[8:8 + L, :] = kp_ref[...]

    def conv(buf, w_ref, b_ref):
        acc = b_ref[...]
        for j in range(CONV_WIDTH):
            acc = acc + buf[8 - (CONV_WIDTH - 1) + j: 8 - (CONV_WIDTH - 1) + j + L, :] * w_ref[j:j + 1, :]
        return acc

    q = _silu(conv(qbuf, cwq_ref, cbq_ref))
    k = _silu(conv(kbuf, cwk_ref, cbk_ref)) * (Dh ** -0.5)
    qbuf[0:8, :] = qbuf[L:L + 8, :]
    kbuf[0:8, :] = kbuf[L:L + 8, :]
    v = v_ref[...]

    li_r = gir_ref[...] + bi_ref[...]
    lf_r = _log_sigmoid(gfr_ref[...] + bf_ref[...])
    li_c = gic_ref[...] + bi_ref[...]
    lf_c = _log_sigmoid(gfc_ref[...] + bf_ref[...])
    row = lax.broadcasted_iota(jnp.int32, (L, L), 0)
    col = lax.broadcasted_iota(jnp.int32, (L, L), 1)
    tri = row >= col
    b_c = jnp.sum(jnp.where(tri, lf_r, 0.0), axis=1, keepdims=True)
    b_r = jnp.sum(jnp.where(row <= col, lf_c, 0.0), axis=0, keepdims=True)
    m_prev = m_s[...]
    dmat = jnp.where(tri, b_c - b_r + li_r, -jnp.inf)
    inter = b_c + m_prev
    m_t = jnp.maximum(inter, jnp.max(dmat, axis=1, keepdims=True))
    w_intra = jnp.exp(dmat - m_t)
    w_inter = jnp.exp(inter - m_t)
    qb = q.astype(BF16)
    kb = k.astype(BF16)
    vb = v.astype(BF16)
    C = C_s[...]
    n_row = n_s[...]
    s = _dot_nt(qb, kb) * w_intra
    num = _dot(s.astype(BF16), vb) + w_inter * _dot_nt(qb, C.astype(BF16))
    den = jnp.sum(s, axis=1, keepdims=True) + w_inter * jnp.sum(q * n_row, axis=1, keepdims=True)
    hval = num / jnp.maximum(jnp.abs(den), jnp.exp(-m_t))
    m_new = m_t[L - 1:L, :]
    bL = b_c[L - 1:L, :]
    cd = jnp.exp(bL + m_prev - m_new)
    w_c = jnp.exp(bL - b_c + li_c - m_new)
    w_r = jnp.exp(bL - b_r + li_r - m_new)
    C_new = cd * C + _dot_tn((v * w_c).astype(BF16), kb)
    n_new = cd * n_row + _dot(w_r.astype(BF16), kb)
    C_s[...] = C_new
    n_s[...] = n_new
    m_s[...] = m_new

    hn = hval * lax.rsqrt(jnp.mean(hval * hval, axis=-1, keepdims=True) + NORM_EPS) * mn_ref[...]
    out_ref[...] = hn * jax.nn.sigmoid(mo_ref[...]) * _silu(mz_ref[...])

    @pl.when(c == nc - 1)
    def _():
        C_out[...] = C_new
        n_out[...] = n_new
        m_out[...] = m_new


def mlstm_prompt(proj, B, S, gi, gf, conv_w, conv_b, b_i, b_f, m_norm):
    H, Dh, L = M_HEADS, M_HEAD_DIM, M_CHUNK
    nc = S // L
    M = B * S
    row = lambda t: t.reshape(B, nc, L, H).transpose(0, 3, 1, 2).reshape(B, H, nc, 1, L)
    colv = lambda t: t.reshape(B, nc, L, H).transpose(0, 3, 1, 2).reshape(B, H, nc, L, 1)
    pblk = lambda off: pl.BlockSpec((L, Dh), lambda b, h, c: (b * nc + c, off + h))
    grow = pl.BlockSpec((None, None, None, 1, L), lambda b, h, c: (b, h, c, 0, 0))
    gcol = pl.BlockSpec((None, None, None, L, 1), lambda b, h, c: (b, h, c, 0, 0))
    hb = M_WIDTH // Dh
    outs = pl.pallas_call(
        _mlstm_chunk_kernel,
        grid=(B, H, nc),
        in_specs=[pblk(0), pblk(hb), pblk(2 * hb), pblk(3 * hb), pblk(4 * hb),
                  grow, grow, gcol, gcol,
                  pl.BlockSpec((CONV_WIDTH, Dh), lambda b, h, c: (0, h)),
                  pl.BlockSpec((CONV_WIDTH, Dh), lambda b, h, c: (0, hb + h)),
                  pl.BlockSpec((1, Dh), lambda b, h, c: (0, h)),
                  pl.BlockSpec((1, Dh), lambda b, h, c: (0, hb + h)),
                  pl.BlockSpec((None, 1, 1), lambda b, h, c: (h, 0, 0)),
                  pl.BlockSpec((None, 1, 1), lambda b, h, c: (h, 0, 0)),
                  pl.BlockSpec((1, Dh), lambda b, h, c: (0, h))],
        out_specs=[pl.BlockSpec((L, Dh), lambda b, h, c: (b * nc + c, h)),
                   pl.BlockSpec((None, None, Dh, Dh), lambda b, h, c: (b, h, 0, 0)),
                   pl.BlockSpec((None, None, 1, Dh), lambda b, h, c: (b, h, 0, 0)),
                   pl.BlockSpec((None, None, 1, 1), lambda b, h, c: (b, h, 0, 0))],
        out_shape=[jax.ShapeDtypeStruct((M, M_WIDTH), F32),
                   jax.ShapeDtypeStruct((B, H, Dh, Dh), F32),
                   jax.ShapeDtypeStruct((B, H, 1, Dh), F32),
                   jax.ShapeDtypeStruct((B, H, 1, 1), F32)],
        scratch_shapes=[pltpu.VMEM((Dh, Dh), F32), pltpu.VMEM((1, Dh), F32), pltpu.VMEM((1, 1), F32),
                        pltpu.VMEM((L + 8, Dh), F32), pltpu.VMEM((L + 8, Dh), F32)],
        compiler_params=_cp(("parallel", "parallel", "arbitrary")),
        name="mlstm_prompt",
    )(proj, proj, proj, proj, proj, row(gi), row(gf), colv(gi), colv(gf),
      conv_w, conv_w, conv_b.reshape(1, -1), conv_b.reshape(1, -1),
      b_i.reshape(H, 1, 1), b_f.reshape(H, 1, 1), m_norm.reshape(1, -1))
    return outs


def _mlstm_step_kernel(x_ref, conv0_ref, cw_ref, cb_ref, v_ref, gi_ref, gf_ref, bi_ref, bf_ref,
                       mo_ref, mz_ref, mn_ref, C_ref, n_ref, m_ref,
                       out_ref, conv_out, C_out, n_out, m_out):
    H, Dh = v_ref.shape
    x = x_ref[...]
    qk = cb_ref[...]
    for j in range(CONV_WIDTH - 1):
        qk = qk + conv0_ref[j] * cw_ref[j]
    qk = qk + x * cw_ref[CONV_WIDTH - 1]
    for j in range(CONV_WIDTH - 2):
        conv_out[j] = conv0_ref[j + 1]
    conv_out[CONV_WIDTH - 2] = x
    qk = _silu(qk)
    q = qk[0:H]
    k = qk[H:2 * H] * (Dh ** -0.5)
    v = v_ref[...]
    li = gi_ref[...] + bi_ref[...]
    lf = _log_sigmoid(gf_ref[...] + bf_ref[...])
    m0 = m_ref[...]
    inter = lf + m0
    m_t = jnp.maximum(inter, li)
    w_intra = jnp.exp(li - m_t)
    w_inter = jnp.exp(inter - m_t)
    s = jnp.sum(q * k, axis=-1, keepdims=True) * w_intra
    n0 = n_ref[...]
    hrow = lax.broadcasted_iota(jnp.int32, (H, Dh), 0)
    cq = jnp.zeros((H, Dh), F32)
    vT = _dot_nt(_eye(Dh), v * w_intra, HI)
    for h in range(H):
        Ch = C_ref[h]
        cq = jnp.where(hrow == h, _dot_nt(q, Ch, HI), cq)
        C_out[h] = w_inter[h:h + 1, :] * Ch + vT[:, h:h + 1] * k[h:h + 1, :]
    num = s * v + w_inter * cq
    den = s + w_inter * jnp.sum(n0 * q, axis=-1, keepdims=True)
    hval = num / jnp.maximum(jnp.abs(den), jnp.exp(-m_t))
    n_out[...] = w_inter * n0 + w_intra * k
    m_out[...] = m_t
    hn = hval * lax.rsqrt(jnp.mean(hval * hval, axis=-1, keepdims=True) + NORM_EPS) * mn_ref[...]
    out_ref[...] = hn * jax.nn.sigmoid(mo_ref[...]) * _silu(mz_ref[...])


def mlstm_step(x16, conv0, conv_w, conv_b, v, gi, gf, b_i, b_f, mo, mz, m_norm, C0, n0, m0):
    B = x16.shape[0]
    H, Dh = M_HEADS, M_HEAD_DIM
    W = CONV_WIDTH
    per_b = lambda *shape: pl.BlockSpec((None,) + shape, lambda b: (b,) + (0,) * len(shape))
    const = lambda *shape: pl.BlockSpec(shape, lambda b: (0,) * len(shape))
    return pl.pallas_call(
        _mlstm_step_kernel,
        grid=(B,),
        in_specs=[per_b(2 * H, Dh), per_b(W - 1, 2 * H, Dh), const(W, 2 * H, Dh), const(2 * H, Dh),
                  per_b(H, Dh), per_b(H, 1), per_b(H, 1), const(H, 1), const(H, 1),
                  per_b(H, Dh), per_b(H, Dh), const(H, Dh),
                  per_b(H, Dh, Dh), per_b(H, Dh), per_b(H, 1)],
        out_specs=[per_b(H, Dh), per_b(W - 1, 2 * H, Dh), per_b(H, Dh, Dh), per_b(H, Dh), per_b(H, 1)],
        out_shape=[jax.ShapeDtypeStruct((B, H, Dh), F32),
                   jax.ShapeDtypeStruct((B, W - 1, 2 * H, Dh), F32),
                   jax.ShapeDtypeStruct((B, H, Dh, Dh), F32),
                   jax.ShapeDtypeStruct((B, H, Dh), F32),
                   jax.ShapeDtypeStruct((B, H, 1), F32)],
        compiler_params=_cp(("parallel",)),
        name="mlstm_step",
    )(x16, conv0, conv_w, conv_b, v, gi, gf, b_i, b_f, mo, mz, m_norm, C0, n0, m0)


def _rwkv_chunk_kernel(rr_ref, rk_ref, rv_ref, rz_ref, rw_ref, ra_ref, mu4_ref, mu2_ref, par_ref,
                       w2_ref, a2_ref, out_ref, S_out, S_s, carry_s):
    c = pl.program_id(2)
    nc = pl.num_programs(2)
    L = rr_ref.shape[0]
    W = rr_ref.shape[1]
    N = R_HEAD_DIM

    @pl.when(c == 0)
    def _():
        S_s[...] = jnp.zeros(S_s.shape, F32)
        carry_s[...] = jnp.zeros(carry_s.shape, F32)

    row = lax.broadcasted_iota(jnp.int32, (L, W), 0)
    lo = lax.broadcasted_iota(jnp.int32, (L, W), 1) < N

    def shift(x, idx, mu):
        prev = jnp.where(row == 0, carry_s[idx:idx + 1, :], pltpu.roll(x, 1, 0))
        carry_s[idx:idx + 1, :] = x[L - 1:L, :]
        return x + mu * (prev - x)

    r = shift(rr_ref[...], 0, mu4_ref[0:1, :])
    k = shift(rk_ref[...], 1, mu4_ref[1:2, :])
    v = shift(rv_ref[...], 2, mu4_ref[2:3, :])
    z = shift(rz_ref[...], 3, mu4_ref[3:4, :])
    xw = shift(rw_ref[...], 4, mu2_ref[0:1, :])
    xa = shift(ra_ref[...], 5, mu2_ref[1:2, :])
    w0, a0 = par_ref[0:1, :], par_ref[1:2, :]
    k_k, k_a, r_k = par_ref[2:3, :], par_ref[3:4, :], par_ref[4:5, :]
    gn_w, gn_b = par_ref[5:6, :], par_ref[6:7, :]

    def seg_sum(x):
        s0 = jnp.sum(jnp.where(lo, x, 0.0), axis=1, keepdims=True)
        s1 = jnp.sum(jnp.where(lo, 0.0, x), axis=1, keepdims=True)
        return jnp.where(lo, s0, s1)

    w_log = -_softplus(-(w0 + _dot(jnp.tanh(xw), w2_ref[...], HI))) - 0.5
    logw = -jnp.exp(w_log)
    a = jax.nn.sigmoid(a0 + _dot(xa, a2_ref[...], HI))
    kk = k * k_k
    kk = kk * lax.rsqrt(jnp.maximum(seg_sum(kk * kk), 1e-24))
    k2 = k * (1.0 + (a - 1.0) * k_a)
    alpha = -kk
    beta = kk * a

    tr = lax.broadcasted_iota(jnp.int32, (L, L), 0)
    tc = lax.broadcasted_iota(jnp.int32, (L, L), 1)
    b = _dot((tr >= tc).astype(F32), logw, HI)
    bL = b[L - 1:L, :]
    e_neg = jnp.exp(-b)
    e_end = jnp.exp(bL - b)
    a_t = alpha * jnp.exp(b - logw)
    r_t = r * jnp.exp(b)
    b_t = beta * e_neg
    k_t = k2 * e_neg
    b_h = beta * e_end
    k_h = k2 * e_end

    def stack(x):
        return jnp.concatenate([jnp.where(lo, x, 0.0), jnp.where(lo, 0.0, x)], axis=0)

    a_s, r_s, b_s, k_s, bh_s, kh_s, v_s = (stack(t) for t in (a_t, r_t, b_t, k_t, b_h, k_h, v))
    ri = lax.broadcasted_iota(jnp.int32, (2 * L, 2 * L), 0)
    ci = lax.broadcasted_iota(jnp.int32, (2 * L, 2 * L), 1)
    same = jnp.where(ri >= L, 1, 0) == jnp.where(ci >= L, 1, 0)
    strict = jnp.logical_and(same, ci < ri)
    incl = jnp.logical_and(same, ci <= ri)
    A = jnp.where(strict, _dot_nt(a_s, b_s, HI), 0.0)
    Bm = jnp.where(strict, _dot_nt(a_s, k_s, HI), 0.0)
    P = jnp.where(incl, _dot_nt(r_s, b_s, HI), 0.0)
    Q = jnp.where(incl, _dot_nt(r_s, k_s, HI), 0.0)
    T = jnp.where(ri == ci, 1.0, 0.0) + A
    Ap = A
    for _ in range(5):
        Ap = _dot(Ap, Ap, HI)
        T = T + _dot(Ap, T, HI)
    S = S_s[...]
    SA = _dot(T, _dot_nt(a_s, S, HI) + _dot(Bm, v_s, HI), HI)
    Ys = _dot_nt(r_s, S, HI) + _dot(P, SA, HI) + _dot(Q, v_s, HI)
    y = Ys[0:L] + Ys[L:2 * L]
    S_new = S * jnp.exp(bL) + _dot_tn(SA, bh_s, HI) + _dot_tn(v_s, kh_s, HI)
    S_s[...] = S_new

    mu_y = seg_sum(y) * (1.0 / N)
    d = y - mu_y
    var = seg_sum(d * d) * (1.0 / N)
    yn = d * lax.rsqrt(var + RWKV_GN_EPS) * gn_w + gn_b
    yn = yn + seg_sum(r * k2 * r_k) * v
    out_ref[...] = yn * _silu(z)

    @pl.when(c == nc - 1)
    def _():
        S_out[...] = S_new


def rwkv_prompt(proj, B, S, mu4, mu2, par, w2p, a2p, r_off):
    L = R_CHUNK
    W = 2 * R_HEAD_DIM
    nc = S // L
    M = B * S
    npair = mu4.shape[1] // W
    blk = lambda off: pl.BlockSpec((L, W), lambda b, p, c: (b * nc + c, off + p))
    cblk = lambda off: pl.BlockSpec((L, W), lambda b, p, c: (b * nc + c, off))
    return pl.pallas_call(
        _rwkv_chunk_kernel,
        grid=(B, npair, nc),
        in_specs=[blk(r_off), blk(r_off + npair), blk(r_off + 2 * npair), blk(r_off + 3 * npair),
                  cblk(r_off + 4 * npair), cblk(r_off + 4 * npair + 1),
                  pl.BlockSpec((8, W), lambda b, p, c: (0, p)),
                  pl.BlockSpec((8, W), lambda b, p, c: (0, 0)),
                  pl.BlockSpec((8, W), lambda b, p, c: (0, p)),
                  pl.BlockSpec((W, W), lambda b, p, c: (0, p)),
                  pl.BlockSpec((W, W), lambda b, p, c: (0, p))],
        out_specs=[pl.BlockSpec((L, W), lambda b, p, c: (b * nc + c, p)),
                   pl.BlockSpec((None, None, W, W), lambda b, p, c: (b, p, 0, 0))],
        out_shape=[jax.ShapeDtypeStruct((M, npair * W), F32),
                   jax.ShapeDtypeStruct((B, npair, W, W), F32)],
        scratch_shapes=[pltpu.VMEM((W, W), F32), pltpu.VMEM((8, W), F32)],
        compiler_params=_cp(("parallel", "parallel", "arbitrary")),
        name="rwkv_prompt",
    )(proj, proj, proj, proj, proj, proj, mu4, mu2, par, w2p, a2p)


def _rwkv_prep_kernel(rr_ref, rk_ref, rv_ref, rz_ref, rw_ref, ra_ref, sh4_ref, sh2_ref, mu4_ref, mu2_ref,
                      w0_ref, a0_ref, w2_ref, a2_ref, r_o, k_o, v_o, z_o, w_o, a_o):
    Wd = rr_ref.shape[1]
    sh = lambda x, prev, mu: x + mu * (prev - x)
    r_o[...] = sh(rr_ref[...], sh4_ref[:, 0:Wd], mu4_ref[:, 0:Wd])
    k_o[...] = sh(rk_ref[...], sh4_ref[:, Wd:2 * Wd], mu4_ref[:, Wd:2 * Wd])
    v_o[...] = sh(rv_ref[...], sh4_ref[:, 2 * Wd:3 * Wd], mu4_ref[:, 2 * Wd:3 * Wd])
    z_o[...] = sh(rz_ref[...], sh4_ref[:, 3 * Wd:4 * Wd], mu4_ref[:, 3 * Wd:4 * Wd])
    xw = sh(rw_ref[...], sh2_ref[:, 0:LANES], mu2_ref[:, 0:LANES])
    xa = sh(ra_ref[...], sh2_ref[:, LANES:2 * LANES], mu2_ref[:, LANES:2 * LANES])
    w_log = -_softplus(-(w0_ref[...] + _dot(jnp.tanh(xw), w2_ref[...], HI))) - 0.5
    w_o[...] = jnp.exp(-jnp.exp(w_log))
    a_o[...] = jax.nn.sigmoid(a0_ref[...] + _dot(xa, a2_ref[...], HI))


def rwkv_prep(proj, sh4, sh2, mu4, mu2, w0, a0, w2p, a2p, r_blk, s_blk):
    B = proj.shape[0]
    Wd = w0.shape[1]
    full = lambda a: pl.BlockSpec(a.shape, lambda i: (0,) * a.ndim)
    return pl.pallas_call(
        _rwkv_prep_kernel,
        grid=(1,),
        in_specs=[pl.BlockSpec((B, Wd), lambda i: (0, r_blk)), pl.BlockSpec((B, Wd), lambda i: (0, r_blk + 1)),
                  pl.BlockSpec((B, Wd), lambda i: (0, r_blk + 2)), pl.BlockSpec((B, Wd), lambda i: (0, r_blk + 3)),
                  pl.BlockSpec((B, LANES), lambda i: (0, s_blk)), pl.BlockSpec((B, LANES), lambda i: (0, s_blk + 1)),
                  full(sh4), full(sh2), full(mu4), full(mu2), full(w0), full(a0), full(w2p), full(a2p)],
        out_specs=[pl.BlockSpec((B, Wd), lambda i: (0, 0))] * 6,
        out_shape=[jax.ShapeDtypeStruct((B, Wd), F32)] * 6,
        compiler_params=_cp(("arbitrary",)),
        name="rwkv_prep",
    )(proj, proj, proj, proj, proj, proj, sh4, sh2, mu4, mu2, w0, a0, w2p, a2p)


def _rwkv_step_kernel(r_ref, k_ref, v_ref, z_ref, w_ref, a_ref, par_ref, S_ref, out_ref, S_out):
    H, N = r_ref.shape
    r, k, v, z, w, a = (t[...] for t in (r_ref, k_ref, v_ref, z_ref, w_ref, a_ref))
    k_k, k_a, r_k, gn_w, gn_b = (par_ref[i] for i in range(5))
    kk = k * k_k
    kk = kk * lax.rsqrt(jnp.maximum(jnp.sum(kk * kk, axis=-1, keepdims=True), 1e-24))
    k2 = k * (1.0 + (a - 1.0) * k_a)
    alpha = -kk
    beta = kk * a
    vT = _dot_nt(_eye(N), v, HI)
    lane_h = lax.broadcasted_iota(jnp.int32, (N, H), 1)
    yT = jnp.zeros((N, H), F32)
    for h in range(H):
        S = S_ref[h]
        sa = jnp.sum(S * alpha[h:h + 1, :], axis=-1, keepdims=True)
        Sn = S * w[h:h + 1, :] + sa * beta[h:h + 1, :] + vT[:, h:h + 1] * k2[h:h + 1, :]
        S_out[h] = Sn
        yT = jnp.where(lane_h == h, jnp.sum(Sn * r[h:h + 1, :], axis=-1, keepdims=True), yT)
    y = _dot_nt(_eye(H), yT, HI)
    mu_y = jnp.mean(y, axis=-1, keepdims=True)
    d = y - mu_y
    var = jnp.mean(d * d, axis=-1, keepdims=True)
    yn = d * lax.rsqrt(var + RWKV_GN_EPS) * gn_w + gn_b
    yn = yn + jnp.sum(r * k2 * r_k, axis=-1, keepdims=True) * v
    out_ref[...] = yn * _silu(z)


def rwkv_step(r, k, v, z, w, a, par, S0):
    B, H, N = r.shape
    vec = pl.BlockSpec((None, H, N), lambda b: (b, 0, 0))
    mat = pl.BlockSpec((None, H, N, N), lambda b: (b, 0, 0, 0))
    return pl.pallas_call(
        _rwkv_step_kernel,
        grid=(B,),
        in_specs=[vec] * 6 + [pl.BlockSpec(par.shape, lambda b: (0, 0, 0)), mat],
        out_specs=[vec, mat],
        out_shape=[jax.ShapeDtypeStruct((B, H, N), F32), jax.ShapeDtypeStruct((B, H, N, N), F32)],
        compiler_params=_cp(("parallel",)),
        name="rwkv_step",
    )(r, k, v, z, w, a, par, S0)


def _qk_norm_kernel(q_ref, k_ref, f_ref, gq_ref, gk_ref, bf_ref, qn_ref, kn_ref, lf_ref):
    D = F_HEAD_DIM
    for h in range(q_ref.shape[1] // D):
        sl = slice(h * D, (h + 1) * D)
        x = q_ref[:, sl]
        qn_ref[:, sl] = x * lax.rsqrt(jnp.mean(x * x, axis=-1, keepdims=True) + NORM_EPS) * gq_ref[...]
        y = k_ref[:, sl]
        kn_ref[:, sl] = y * lax.rsqrt(jnp.mean(y * y, axis=-1, keepdims=True) + NORM_EPS) * gk_ref[...]
    lf_ref[...] = _log_sigmoid(f_ref[...] + bf_ref[...])


def qk_norm(proj, gq, gk, bf_pad, tm, Wd, f_blk):
    M = proj.shape[0]
    return pl.pallas_call(
        _qk_norm_kernel,
        grid=(M // tm,),
        in_specs=[pl.BlockSpec((tm, Wd), lambda i: (i, 0)), pl.BlockSpec((tm, Wd), lambda i: (i, 1)),
                  pl.BlockSpec((tm, LANES), lambda i: (i, f_blk)),
                  pl.BlockSpec((1, F_HEAD_DIM), lambda i: (0, 0)), pl.BlockSpec((1, F_HEAD_DIM), lambda i: (0, 0)),
                  pl.BlockSpec((1, LANES), lambda i: (0, 0))],
        out_specs=[pl.BlockSpec((tm, Wd), lambda i: (i, 0)), pl.BlockSpec((tm, Wd), lambda i: (i, 0)),
                   pl.BlockSpec((tm, LANES), lambda i: (i, 0))],
        out_shape=[jax.ShapeDtypeStruct((M, Wd), F32), jax.ShapeDtypeStruct((M, Wd), F32),
                   jax.ShapeDtypeStruct((M, LANES), F32)],
        compiler_params=_cp(("parallel",)),
        name="qk_norm",
    )(proj, proj, proj, gq.reshape(1, -1), gk.reshape(1, -1), bf_pad)


def _cumsum_kernel(x_ref, c_ref):
    H, S = x_ref.shape
    U = (lax.broadcasted_iota(jnp.int32, (LANES, LANES), 0) <= lax.broadcasted_iota(jnp.int32, (LANES, LANES), 1)).astype(F32)
    off = jnp.zeros((H, 1), F32)
    for t in range(S // LANES):
        sl = slice(t * LANES, (t + 1) * LANES)
        w = _dot(x_ref[:, sl], U, HI) + off
        c_ref[:, sl] = w
        off = w[:, LANES - 1:LANES]


def cumsum_lanes(x):
    B, H, S = x.shape
    return pl.pallas_call(
        _cumsum_kernel,
        grid=(B,),
        in_specs=[pl.BlockSpec((None, H, S), lambda b: (b, 0, 0))],
        out_specs=pl.BlockSpec((None, H, S), lambda b: (b, 0, 0)),
        out_shape=jax.ShapeDtypeStruct((B, H, S), F32),
        compiler_params=_cp(("parallel",)),
        name="forget_cumsum",
    )(x)


def _fox_flash_kernel(q_ref, k_ref, v_ref, cq_ref, ck_ref, o_ref, m_s, l_s, acc_s):
    i = pl.program_id(2)
    j = pl.program_id(3)
    tq, D = q_ref.shape
    tk = k_ref.shape[0]

    @pl.when(j == 0)
    def _():
        m_s[...] = jnp.full(m_s.shape, -jnp.inf, F32)
        l_s[...] = jnp.zeros(l_s.shape, F32)
        acc_s[...] = jnp.zeros(acc_s.shape, F32)

    @pl.when(j <= i)
    def _():
        s = _dot_nt(q_ref[...].astype(BF16), k_ref[...].astype(BF16)) * (D ** -0.5)
        s = s + (cq_ref[...] - ck_ref[...])
        qpos = i * tq + lax.broadcasted_iota(jnp.int32, (tq, tk), 0)
        kpos = j * tk + lax.broadcasted_iota(jnp.int32, (tq, tk), 1)
        s = jnp.where(kpos <= qpos, s, -jnp.inf)
        m_prev = m_s[...]
        m_new = jnp.maximum(m_prev, jnp.max(s, axis=1, keepdims=True))
        alpha = jnp.exp(m_prev - m_new)
        p = jnp.exp(s - m_new)
        l_s[...] = alpha * l_s[...] + jnp.sum(p, axis=1, keepdims=True)
        acc_s[...] = alpha * acc_s[...] + _dot(p.astype(BF16), v_ref[...].astype(BF16))
        m_s[...] = m_new

    @pl.when(j == pl.num_programs(3) - 1)
    def _():
        o_ref[...] = acc_s[...] / l_s[...]


def fox_prompt_attn(qn, kn, proj, v_blk, cq, ck, B, S, t):
    D = F_HEAD_DIM
    H = qn.shape[1] // D
    nb = S // t
    return pl.pallas_call(
        _fox_flash_kernel,
        grid=(B, H, nb, nb),
        in_specs=[pl.BlockSpec((t, D), lambda b, h, i, j: (b * nb + i, h)),
                  pl.BlockSpec((t, D), lambda b, h, i, j: (b * nb + jnp.minimum(j, i), h)),
                  pl.BlockSpec((t, D), lambda b, h, i, j: (b * nb + jnp.minimum(j, i), v_blk + h)),
                  pl.BlockSpec((None, None, t, 1), lambda b, h, i, j: (b, h, i, 0)),
                  pl.BlockSpec((None, None, 1, t), lambda b, h, i, j: (b, h, 0, jnp.minimum(j, i)))],
        out_specs=pl.BlockSpec((t, D), lambda b, h, i, j: (b * nb + i, h)),
        out_shape=jax.ShapeDtypeStruct((B * S, H * D), F32),
        scratch_shapes=[pltpu.VMEM((t, 1), F32), pltpu.VMEM((t, 1), F32), pltpu.VMEM((t, D), F32)],
        compiler_params=_cp(("parallel", "parallel", "parallel", "arbitrary")),
        name="fox_prompt_attn",
    )(qn, kn, proj, cq, ck)


def _fox_decode_kernel(pt_ref, q_ref, kn_ref, vn_ref, lfn_ref, sel_ref, k_ref, v_ref, lf_ref, o_ref,
                       m_s, l_s, acc_s, off_s):
    p = pl.program_id(1)
    H, D = q_ref.shape
    T = k_ref.shape[0]

    @pl.when(p == 0)
    def _():
        m_s[...] = jnp.full(m_s.shape, -jnp.inf, F32)
        l_s[...] = jnp.zeros(l_s.shape, F32)
        acc_s[...] = jnp.zeros(acc_s.shape, F32)
        off_s[...] = jnp.zeros(off_s.shape, F32)

    q = q_ref[...] * (D ** -0.5)
    emb = (lax.broadcasted_iota(jnp.int32, (H, LANES), 0) == lax.broadcasted_iota(jnp.int32, (H, LANES), 1)).astype(F32)
    lf_pad = _dot(lf_ref[...], emb, HI)
    hi = lf_pad.astype(BF16)
    r1 = lf_pad - hi.astype(F32)
    mid = r1.astype(BF16)
    low = (r1 - mid.astype(F32)).astype(BF16)
    sel = sel_ref[...]
    ct = (_dot(sel, hi) + _dot(sel, mid)) + _dot(sel, low)
    ct3 = ct.reshape(T, H, LANES)
    hsel = lax.broadcasted_iota(jnp.int32, (1, H, LANES), 1) == lax.broadcasted_iota(jnp.int32, (1, H, LANES), 2)
    off = off_s[:, 0:1]
    s3 = jnp.sum(k_ref[...] * q[None] - jnp.where(hsel, ct3, 0.0), axis=-1, keepdims=True) - off[None]
    tot = jnp.sum(jnp.where(hsel[0], ct3[T - 1], 0.0), axis=-1, keepdims=True)
    m_prev = m_s[:, 0:1]
    m_new = jnp.maximum(m_prev, jnp.max(s3, axis=0))
    alpha = jnp.exp(m_prev - m_new)
    p3 = jnp.exp(s3 - m_new[None])
    l_new = alpha * l_s[:, 0:1] + jnp.sum(p3, axis=0)
    acc_new = alpha * acc_s[...] + jnp.sum(p3 * v_ref[...], axis=0)
    off_new = off + tot
    m_s[...] = jnp.broadcast_to(m_new, m_s.shape)
    l_s[...] = jnp.broadcast_to(l_new, l_s.shape)
    acc_s[...] = acc_new
    off_s[...] = jnp.broadcast_to(off_new, off_s.shape)

    @pl.when(p == pl.num_programs(1) - 1)
    def _():
        cq = off_new + lfn_ref[:, 0:1]
        s_self = jnp.sum(q * kn_ref[...], axis=-1, keepdims=True) - cq
        m_f = jnp.maximum(m_new, s_self)
        a2 = jnp.exp(m_new - m_f)
        ps = jnp.exp(s_self - m_f)
        o_ref[...] = (a2 * acc_new + ps * vn_ref[...]) / (a2 * l_new + ps)


def fox_decode_attn(q, k_new, v_new, lf_new, cache_k, cache_v, cache_lf, page_table):
    B, H, D = q.shape
    n_pages = page_table.shape[1]
    T = cache_k.shape[1]
    rows = jnp.arange(T * H, dtype=jnp.int32)[:, None] // H
    sel = (jnp.arange(T, dtype=jnp.int32)[None, :] <= rows).astype(BF16)
    vec = pl.BlockSpec((None, H, D), lambda b, p, pt: (b, 0, 0))
    grid_spec = pltpu.PrefetchScalarGridSpec(
        num_scalar_prefetch=1,
        grid=(B, n_pages),
        in_specs=[vec, vec, vec, vec,
                  pl.BlockSpec((T * H, T), lambda b, p, pt: (0, 0)),
                  pl.BlockSpec((None, T, H, D), lambda b, p, pt: (pt[b, p], 0, 0, 0)),
                  pl.BlockSpec((None, T, H, D), lambda b, p, pt: (pt[b, p], 0, 0, 0)),
                  pl.BlockSpec((None, T, H), lambda b, p, pt: (pt[b, p], 0, 0))],
        out_specs=pl.BlockSpec((None, H, D), lambda b, p, pt: (b, 0, 0)),
        scratch_shapes=[pltpu.VMEM((H, D), F32)] * 4,
    )
    return pl.pallas_call(
        _fox_decode_kernel,
        grid_spec=grid_spec,
        out_shape=jax.ShapeDtypeStruct((B, H, D), F32),
        compiler_params=_cp(("parallel", "arbitrary")),
        name="fox_decode_attn",
    )(page_table, q, k_new, v_new, lf_new, sel, cache_k, cache_v, cache_lf)


def _pad_cols(a, width):
    return jnp.pad(a, [(0, 0)] * (a.ndim - 1) + [(0, width - a.shape[-1])])


def _split_mix(a):
    W, H, D = M_WIDTH, M_HEADS, a.shape[-1]
    o = 0
    parts = {}
    for name, n in (("mqk", 2 * W), ("mv", W), ("mi", H), ("mf", H), ("mo", W), ("mz", W)):
        parts[name] = a[..., o:o + n]
        o += n
    Wr = (D - o - 2 * R_LORA) // 4
    for name, n in (("rr", Wr), ("rk", Wr), ("rv", Wr), ("rw", R_LORA), ("ra", R_LORA), ("rz", Wr)):
        parts[name] = a[..., o:o + n]
        o += n
    return parts


def _small_mix(p):
    z = jnp.zeros(p["rw"].shape[:-1] + (LANES - R_LORA,), p["rw"].dtype)
    return _pad_cols(jnp.concatenate([p["rw"], z, p["ra"], z, p["mi"], p["mf"]], axis=-1), SMALL_W)


def _arrange_mix(a):
    p = _split_mix(a)
    return jnp.concatenate([p["mqk"], p["mv"], p["mo"], p["mz"], p["rr"], p["rk"], p["rv"], p["rz"], _small_mix(p)],
                           axis=-1)


def _trunk(x, p_in, state, attn, Wt, tm):
    B, T, D = x.shape
    M = B * T
    x2 = x.reshape(M, D)
    tn = 512
    Wr = D
    mixw = Wt["mix_w"]
    proj = norm_matmul(x2, Wt["norm_w"][0], mixw, tm, tn)
    W = M_WIDTH
    small0 = 3 * W + 2 * W + 4 * Wr
    gi = proj[:, small0 + 2 * LANES: small0 + 2 * LANES + M_HEADS]
    gf = proj[:, small0 + 2 * LANES + M_HEADS: small0 + 2 * LANES + 2 * M_HEADS]
    r_col0 = 5 * W

    if state is None:
        bm, C, n, m = mlstm_prompt(proj, B, T, gi.reshape(B, T, M_HEADS), gf.reshape(B, T, M_HEADS),
                                   Wt["m_conv_w"], Wt["m_conv_b"], Wt["m_b_i"], Wt["m_b_f"], Wt["m_norm"])
        conv_new = proj[:, 0:2 * W].reshape(B, T, 2 * W)[:, T - (CONV_WIDTH - 1):]
        C = C
        n = n.reshape(B, M_HEADS, M_HEAD_DIM)
        m = m.reshape(B, M_HEADS)
        br, Sbd = rwkv_prompt(proj, B, T, Wt["mu4"], Wt["mu2"], Wt["r_par"], Wt["w2p"], Wt["a2p"],
                              r_col0 // LANES)
        N = R_HEAD_DIM
        S_new = jnp.stack([Sbd[:, :, 0:N, 0:N], Sbd[:, :, N:2 * N, N:2 * N]], axis=2).reshape(B, Wr // N, N, N)
    else:
        conv0, C0, n0, m0, S0, sh0 = state
        H, Dh = M_HEADS, M_HEAD_DIM
        bm, conv_new, C, n, m = mlstm_step(
            proj[:, 0:2 * W].reshape(B, 2 * H, Dh), conv0.reshape(B, CONV_WIDTH - 1, 2 * H, Dh),
            Wt["m_conv_w"].reshape(CONV_WIDTH, 2 * H, Dh), Wt["m_conv_b"].reshape(2 * H, Dh),
            proj[:, 2 * W:3 * W].reshape(B, H, Dh), gi.reshape(B, H, 1), gf.reshape(B, H, 1),
            Wt["m_b_i"].reshape(H, 1), Wt["m_b_f"].reshape(H, 1),
            proj[:, 3 * W:4 * W].reshape(B, H, Dh), proj[:, 4 * W:5 * W].reshape(B, H, Dh),
            Wt["m_norm"].reshape(H, Dh), C0, n0, m0.reshape(B, H, 1))
        bm = bm.reshape(M, W)
        conv_new = conv_new.reshape(B, CONV_WIDTH - 1, 2 * W)
        m = m.reshape(B, H)
        shp = _split_mix(jnp.concatenate([jnp.zeros((B, 5 * W + 2 * M_HEADS), F32), sh0], axis=-1))
        sh4 = jnp.concatenate([shp["rr"], shp["rk"], shp["rv"], shp["rz"]], axis=-1)
        sh2 = _small_mix(shp)[:, 0:2 * LANES]
        N = R_HEAD_DIM
        Hr = Wr // N
        r_, k_, v_, z_, w_, a_ = rwkv_prep(proj, sh4, sh2, Wt["mu4_flat"],
                                           Wt["mu2_flat"], Wt["r_par"][0:1], Wt["r_par"][1:2], Wt["w2p"], Wt["a2p"],
                                           r_col0 // Wr, small0 // LANES)
        hs = lambda t: t.reshape(B, Hr, N)
        par5 = Wt["r_par"][2:7].reshape(5, Hr, N)
        br, S_new = rwkv_step(hs(r_), hs(k_), hs(v_), hs(z_), hs(w_), hs(a_), par5, S0)
        br = br.reshape(M, Wr)
    sh_new = jnp.concatenate([proj[:, r_col0:r_col0 + 3 * Wr], proj[:, small0:small0 + R_LORA],
                              proj[:, small0 + LANES:small0 + LANES + R_LORA],
                              proj[:, r_col0 + 3 * Wr:r_col0 + 4 * Wr]], axis=-1).reshape(B, T, -1)[:, T - 1]

    h1 = out_mix(bm, br, Wt["out_mix_w"], x2, tm, tn)
    h1 = ple_add(h1, Wt["ple_norm"][0], Wt["ple_gate_w"][0], p_in[0].reshape(M, -1), Wt["ple_proj"][0], tm, tn)

    proj1 = norm_matmul(h1, Wt["norm_w"][1], Wt["fox_w"], tm, tn)
    qn, kn, lf = qk_norm(proj1, Wt["f_q_norm"], Wt["f_k_norm"], Wt["f_b_f_pad"], min(tm, 256), D, 4 * D // LANES)
    Hf = D // F_HEAD_DIM
    lf = lf[:, 0:Hf]
    o = attn(qn, kn, proj1, lf, B, T)
    h2 = out_fox(o, proj1, 3, Wt["out_fox_w"], h1, tm, tn)
    h2 = ple_add(h2, Wt["ple_norm"][1], Wt["ple_gate_w"][1], p_in[1].reshape(M, -1), Wt["ple_proj"][1], tm, tn)
    y = rmsnorm_rows(h2, Wt["final_norm"], min(tm, 256)).reshape(B, T, D)

    k_rows = kn.reshape(1, B, T, Hf, F_HEAD_DIM)
    v_rows = proj1[:, 2 * D:3 * D].reshape(1, B, T, Hf, F_HEAD_DIM)
    lf_rows = lf.reshape(1, B, T, Hf)
    mix_state = tuple(t[None] for t in (conv_new, C, n, m, S_new, sh_new))
    return y, (k_rows, v_rows, lf_rows), mix_state


def kernel(x_prompt, x_sample, cache_k, cache_v, cache_lf, state_mlstm_conv, state_mlstm_C, state_mlstm_n,
           state_mlstm_m, state_rwkv_S, state_rwkv_shift, page_table, p_prompt, p_sample,
           norm_w, final_norm, w_in_mix, w_out_mix, m_conv_w, m_conv_b, m_b_i, m_b_f, m_norm,
           r_mu, r_w0, r_w2, r_a0, r_a2, r_k_k, r_k_a, r_r_k, r_gn_w, r_gn_b,
           w_in_fox, w_out_fox, f_b_f, f_q_norm, f_k_norm, ple_proj, ple_gate_w, ple_norm):
    D = x_prompt.shape[-1]
    Wr = r_w0.shape[-1]
    Hf = D // F_HEAD_DIM
    mix_w = _arrange_mix(w_in_mix[0]).astype(BF16)
    fw = w_in_fox[0]
    fox_w = jnp.concatenate([fw[:, 0:3 * D], fw[:, 3 * D + Hf:4 * D + Hf], _pad_cols(fw[:, 3 * D:3 * D + Hf], SMALL_W)],
                            axis=-1).astype(BF16)
    mu_full = jnp.concatenate([jnp.zeros((5 * M_WIDTH + 2 * M_HEADS,), F32), r_mu[0]])
    mup = _split_mix(mu_full)
    mu4_flat = jnp.concatenate([mup["rr"], mup["rk"], mup["rv"], mup["rz"]])[None]
    mu2_flat = _small_mix(mup)[None, 0:2 * LANES]
    mu4 = _pad_cols(jnp.stack([mup["rr"], mup["rk"], mup["rv"], mup["rz"]]).T, 8).T
    mu2 = _pad_cols(mu2_flat.reshape(2, LANES).T, 8).T
    r_par = jnp.stack([r_w0[0], r_a0[0], r_k_k[0], r_k_a[0], r_r_k[0], r_gn_w[0], r_gn_b[0], jnp.zeros((Wr,), F32)])
    pad_rows = lambda a: jnp.pad(a, ((0, LANES - a.shape[0]), (0, 0)))
    Wt = dict(mix_w=mix_w, fox_w=fox_w, norm_w=norm_w, final_norm=final_norm,
              out_mix_w=w_out_mix[0].astype(BF16), out_fox_w=w_out_fox[0].astype(BF16),
              m_conv_w=m_conv_w[0], m_conv_b=m_conv_b[0], m_b_i=m_b_i[0], m_b_f=m_b_f[0], m_norm=m_norm[0],
              mu4=mu4, mu2=mu2, mu4_flat=mu4_flat, mu2_flat=mu2_flat, r_par=r_par,
              w2p=pad_rows(r_w2[0]), a2p=pad_rows(r_a2[0]),
              f_q_norm=f_q_norm[0], f_k_norm=f_k_norm[0], f_b_f_pad=_pad_cols(f_b_f[0][None], LANES),
              ple_proj=ple_proj.astype(BF16), ple_gate_w=ple_gate_w.astype(BF16), ple_norm=ple_norm)

    def attn_prompt(qn, kn, proj1, lf, B, T):
        lfT = lf.reshape(B, T, Hf).transpose(0, 2, 1)
        c = cumsum_lanes(lfT)
        return fox_prompt_attn(qn, kn, proj1, 2 * D // F_HEAD_DIM, c[..., None], c[:, :, None, :], B, T, 512)

    def attn_sample(qn, kn, proj1, lf, B, T):
        r3 = lambda t: t.reshape(B, Hf, F_HEAD_DIM)
        lfb = jnp.broadcast_to(lf.reshape(B, Hf, 1), (B, Hf, F_HEAD_DIM))
        o = fox_decode_attn(r3(qn), r3(kn), r3(proj1[:, 2 * D:3 * D]), lfb,
                            cache_k[0], cache_v[0], cache_lf[0], page_table)
        return o.reshape(B, D)

    Bp, Tp, _ = x_prompt.shape
    y_p, (k_p, v_p, lf_p), mix_p = _trunk(x_prompt, p_prompt.reshape(p_prompt.shape[0], Bp * Tp, -1), None,
                                          attn_prompt, Wt, 512)
    Bs, Ts, _ = x_sample.shape
    state = (state_mlstm_conv[0], state_mlstm_C[0], state_mlstm_n[0], state_mlstm_m[0], state_rwkv_S[0],
             state_rwkv_shift[0])
    y_s, (k_s, v_s, lf_s), mix_s = _trunk(x_sample, p_sample.reshape(p_sample.shape[0], Bs * Ts, -1), state,
                                          attn_sample, Wt, Bs * Ts)
    return (y_p, y_s, k_p, v_p, lf_p) + mix_p + (k_s, v_s, lf_s) + mix_s
```

```python
import functools

import jax
import jax.numpy as jnp
from jax import lax
from jax.experimental import pallas as pl
from jax.experimental.pallas import tpu as pltpu

F32 = jnp.float32
BF16 = jnp.bfloat16
HI = lax.Precision.HIGHEST

NORM_EPS = 1e-6
RWKV_GN_EPS = 64e-5
M_HEADS = 8
M_HEAD_DIM = 256
M_WIDTH = M_HEADS * M_HEAD_DIM
CONV_WIDTH = 4
M_CHUNK = 128
R_HEAD_DIM = 64
R_LORA = 64
R_CHUNK = 64
R_STEP_TOKENS = 256
F_HEAD_DIM = 128
DEC_PAGES_PER_STEP = 4
LOG2E = 1.4426950408889634
FOX_BLOCK = 512
LANES = 128
SMALL_W = 512
VMEM_LIMIT = 48 * 1024 * 1024


def _cp(sem, vmem=VMEM_LIMIT):
    return pltpu.CompilerParams(dimension_semantics=sem, vmem_limit_bytes=vmem)


def _dot(a, b, prec=None):
    return jnp.dot(a, b, preferred_element_type=F32, precision=prec)


def _dot_nt(a, b, prec=None):
    return lax.dot_general(a, b, (((1,), (1,)), ((), ())), preferred_element_type=F32, precision=prec)


def _dot_tn(a, b, prec=None):
    return lax.dot_general(a, b, (((0,), (0,)), ((), ())), preferred_element_type=F32, precision=prec)


def _split_bf16(x):
    hi = x.astype(BF16)
    return hi, (x - hi.astype(F32)).astype(BF16)


def _lhs3(x, axis):
    hi, lo = _split_bf16(x)
    return jnp.concatenate([hi, hi, lo], axis=axis)


def _rhs3(x, axis):
    hi, lo = _split_bf16(x)
    return jnp.concatenate([hi, lo, hi], axis=axis)


def _dot3(a, b):
    return _dot(_lhs3(a, 1), _rhs3(b, 0))


def _dot3_nt(a, b):
    return _dot_nt(_lhs3(a, 1), _rhs3(b, 1))


def _dot3_tn(a, b):
    return _dot_tn(_lhs3(a, 0), _rhs3(b, 0))


def _softplus(y):
    return jnp.maximum(y, 0.0) + jnp.log1p(jnp.exp(-jnp.abs(y)))


def _log_sigmoid(x):
    return -_softplus(-x)


def _silu(x):
    return x * jax.nn.sigmoid(x)


def _eye(n):
    return (lax.broadcasted_iota(jnp.int32, (n, n), 0) == lax.broadcasted_iota(jnp.int32, (n, n), 1)).astype(F32)


def _norm_mm_kernel(x_ref, g_ref, w_ref, o_ref, xn_ref):
    @pl.when(pl.program_id(1) == 0)
    def _():
        x = x_ref[...]
        ms = jnp.mean(x * x, axis=-1, keepdims=True)
        xn_ref[...] = (x * lax.rsqrt(ms + NORM_EPS) * g_ref[...]).astype(BF16)

    o_ref[...] = _dot(xn_ref[...], w_ref[...])


def norm_matmul(x, g, w, tm, tn):
    M, K = x.shape
    N = w.shape[1]
    return pl.pallas_call(
        _norm_mm_kernel,
        grid=(M // tm, N // tn),
        in_specs=[pl.BlockSpec((tm, K), lambda i, j: (i, 0)),
                  pl.BlockSpec((1, K), lambda i, j: (0, 0)),
                  pl.BlockSpec((K, tn), lambda i, j: (0, j))],
        out_specs=pl.BlockSpec((tm, tn), lambda i, j: (i, j)),
        out_shape=jax.ShapeDtypeStruct((M, N), F32),
        scratch_shapes=[pltpu.VMEM((tm, K), BF16)],
        compiler_params=_cp(("parallel", "arbitrary")),
        name="norm_matmul",
    )(x, g.reshape(1, K), w)


def _out_mix_kernel(a_ref, b_ref, wa_ref, wb_ref, res_ref, o_ref, abf, bbf):
    @pl.when(pl.program_id(1) == 0)
    def _():
        abf[...] = a_ref[...].astype(BF16)
        bbf[...] = b_ref[...].astype(BF16)

    o_ref[...] = res_ref[...] + (_dot(abf[...], wa_ref[...]) + _dot(bbf[...], wb_ref[...]))


def out_mix(a, b, w, res, tm, tn):
    M, K = a.shape
    N = w.shape[1]
    return pl.pallas_call(
        _out_mix_kernel,
        grid=(M // tm, N // tn),
        in_specs=[pl.BlockSpec((tm, K), lambda i, j: (i, 0)),
                  pl.BlockSpec((tm, K), lambda i, j: (i, 0)),
                  pl.BlockSpec((K, tn), lambda i, j: (0, j)),
                  pl.BlockSpec((K, tn), lambda i, j: (1, j)),
                  pl.BlockSpec((tm, tn), lambda i, j: (i, j))],
        out_specs=pl.BlockSpec((tm, tn), lambda i, j: (i, j)),
        out_shape=jax.ShapeDtypeStruct((M, N), F32),
        scratch_shapes=[pltpu.VMEM((tm, K), BF16), pltpu.VMEM((tm, K), BF16)],
        compiler_params=_cp(("parallel", "arbitrary")),
        name="out_mix",
    )(a, b, w, w, res)


def _out_fox_kernel(o_in_ref, z_ref, w_ref, res_ref, o_ref, gbf):
    @pl.when(pl.program_id(1) == 0)
    def _():
        gbf[...] = (o_in_ref[...] * _silu(z_ref[...])).astype(BF16)

    o_ref[...] = res_ref[...] + _dot(gbf[...], w_ref[...])


def out_fox(o, proj, z_blk, w, res, tm, tn):
    M, K = o.shape
    N = w.shape[1]
    return pl.pallas_call(
        _out_fox_kernel,
        grid=(M // tm, N // tn),
        in_specs=[pl.BlockSpec((tm, K), lambda i, j: (i, 0)),
                  pl.BlockSpec((tm, K), lambda i, j: (i, z_blk)),
                  pl.BlockSpec((K, tn), lambda i, j: (0, j)),
                  pl.BlockSpec((tm, tn), lambda i, j: (i, j))],
        out_specs=pl.BlockSpec((tm, tn), lambda i, j: (i, j)),
        out_shape=jax.ShapeDtypeStruct((M, N), F32),
        scratch_shapes=[pltpu.VMEM((tm, K), BF16)],
        compiler_params=_cp(("parallel", "arbitrary")),
        name="out_fox",
    )(o, proj, w, res)


def _ple_kernel(h_ref, g_ref, wg_ref, p_ref, pw_ref, ht_ref, o_ref, hn_ref):
    @pl.when(pl.program_id(1) == 0)
    def _():
        x = h_ref[...]
        ms = jnp.mean(x * x, axis=-1, keepdims=True)
        hn_ref[...] = (x * lax.rsqrt(ms + NORM_EPS) * g_ref[...]).astype(BF16)

    gate = jax.nn.sigmoid(_dot(hn_ref[...], wg_ref[...]))
    e = _dot(p_ref[...].astype(BF16), pw_ref[...])
    o_ref[...] = ht_ref[...] + gate * e


def ple_add(h, g, wg, p, pw, tm, tn):
    M, K = h.shape
    N = wg.shape[1]
    P = p.shape[1]
    return pl.pallas_call(
        _ple_kernel,
        grid=(M // tm, N // tn),
        in_specs=[pl.BlockSpec((tm, K), lambda i, j: (i, 0)),
                  pl.BlockSpec((1, K), lambda i, j: (0, 0)),
                  pl.BlockSpec((K, tn), lambda i, j: (0, j)),
                  pl.BlockSpec((tm, P), lambda i, j: (i, 0)),
                  pl.BlockSpec((P, tn), lambda i, j: (0, j)),
                  pl.BlockSpec((tm, tn), lambda i, j: (i, j))],
        out_specs=pl.BlockSpec((tm, tn), lambda i, j: (i, j)),
        out_shape=jax.ShapeDtypeStruct((M, N), F32),
        scratch_shapes=[pltpu.VMEM((tm, K), BF16)],
        compiler_params=_cp(("parallel", "arbitrary")),
        name="ple_add",
    )(h, g.reshape(1, K), wg, p, pw, h)


def _rmsnorm_kernel(x_ref, g_ref, o_ref):
    x = x_ref[...]
    ms = jnp.mean(x * x, axis=-1, keepdims=True)
    o_ref[...] = x * lax.rsqrt(ms + NORM_EPS) * g_ref[...]


def rmsnorm_rows(x, g, tm):
    M, K = x.shape
    return pl.pallas_call(
        _rmsnorm_kernel,
        grid=(M // tm,),
        in_specs=[pl.BlockSpec((tm, K), lambda i: (i, 0)), pl.BlockSpec((1, K), lambda i: (0, 0))],
        out_specs=pl.BlockSpec((tm, K), lambda i: (i, 0)),
        out_shape=jax.ShapeDtypeStruct((M, K), F32),
        compiler_params=_cp(("parallel",)),
        name="final_norm",
    )(x, g.reshape(1, K))


def _mlstm_chunk_kernel(qp_ref, kp_ref, v_ref, mo_ref, mz_ref, gir_ref, gfr_ref, gic_ref, gfc_ref,
                        cwq_ref, cwk_ref, cbq_ref, cbk_ref, bi_ref, bf_ref, mn_ref,
                        out_ref, C_out, n_out, m_out,
                        C_s, n_s, m_s, qbuf, kbuf):
    c = pl.program_id(2)
    nc = pl.num_programs(2)
    L = qp_ref.shape[0]
    Dh = qp_ref.shape[1]

    @pl.when(c == 0)
    def _():
        C_s[...] = jnp.zeros(C_s.shape, F32)
        n_s[...] = jnp.zeros(n_s.shape, F32)
        m_s[...] = jnp.zeros(m_s.shape, F32)
        qbuf[0:8, :] = jnp.zeros((8, Dh), F32)
        kbuf[0:8, :] = jnp.zeros((8, Dh), F32)

    qbuf[8:8 + L, :] = qp_ref[...]
    kbuf[8:8 + L, :] = kp_ref[...]

    def conv(buf, w_ref, b_ref):
        acc = b_ref[...]
        for j in range(CONV_WIDTH):
            acc = acc + buf[8 - (CONV_WIDTH - 1) + j: 8 - (CONV_WIDTH - 1) + j + L, :] * w_ref[j:j + 1, :]
        return acc

    q = _silu(conv(qbuf, cwq_ref, cbq_ref))
    k = _silu(conv(kbuf, cwk_ref, cbk_ref)) * (Dh ** -0.5)
    qbuf[0:8, :] = qbuf[L:L + 8, :]
    kbuf[0:8, :] = kbuf[L:L + 8, :]
    v = v_ref[...]

    li_r = gir_ref[...] + bi_ref[...]
    lf_r = _log_sigmoid(gfr_ref[...] + bf_ref[...])
    li_c = gic_ref[...] + bi_ref[...]
    lf_c = _log_sigmoid(gfc_ref[...] + bf_ref[...])
    row = lax.broadcasted_iota(jnp.int32, (L, L), 0)
    col = lax.broadcasted_iota(jnp.int32, (L, L), 1)
    tri = row >= col
    b_c = jnp.sum(jnp.where(tri, lf_r, 0.0), axis=1, keepdims=True)
    b_r = jnp.sum(jnp.where(row <= col, lf_c, 0.0), axis=0, keepdims=True)
    m_prev = m_s[...]
    dmat = jnp.where(tri, b_c - b_r + li_r, -jnp.inf)
    inter = b_c + m_prev
    m_t = jnp.maximum(inter, jnp.max(dmat, axis=1, keepdims=True))
    w_intra = jnp.exp(dmat - m_t)
    w_inter = jnp.exp(inter - m_t)
    qb = q.astype(BF16)
    kb = k.astype(BF16)
    vb = v.astype(BF16)
    C = C_s[...]
    n_row = n_s[...]
    s = _dot_nt(qb, kb) * w_intra
    num = _dot(s.astype(BF16), vb) + w_inter * _dot_nt(qb, C.astype(BF16))
    den = jnp.sum(s, axis=1, keepdims=True) + w_inter * jnp.sum(q * n_row, axis=1, keepdims=True)
    hval = num / jnp.maximum(jnp.abs(den), jnp.exp(-m_t))
    m_new = m_t[L - 1:L, :]
    bL = b_c[L - 1:L, :]
    cd = jnp.exp(bL + m_prev - m_new)
    w_c = jnp.exp(bL - b_c + li_c - m_new)
    w_r = jnp.exp(bL - b_r + li_r - m_new)
    C_new = cd * C + _dot_tn((v * w_c).astype(BF16), kb)
    n_new = cd * n_row + _dot(w_r.astype(BF16), kb)
    C_s[...] = C_new
    n_s[...] = n_new
    m_s[...] = m_new

    hn = hval * lax.rsqrt(jnp.mean(hval * hval, axis=-1, keepdims=True) + NORM_EPS) * mn_ref[...]
    out_ref[...] = hn * jax.nn.sigmoid(mo_ref[...]) * _silu(mz_ref[...])

    @pl.when(c == nc - 1)
    def _():
        C_out[...] = C_new
        n_out[...] = n_new
        m_out[...] = m_new


def mlstm_prompt(proj, B, S, gi, gf, conv_w, conv_b, b_i, b_f, m_norm):
    H, Dh, L = M_HEADS, M_HEAD_DIM, M_CHUNK
    nc = S // L
    M = B * S
    row = lambda t: t.reshape(B, nc, L, H).transpose(0, 3, 1, 2).reshape(B, H, nc, 1, L)
    colv = lambda t: t.reshape(B, nc, L, H).transpose(0, 3, 1, 2).reshape(B, H, nc, L, 1)
    pblk = lambda off: pl.BlockSpec((L, Dh), lambda b, h, c: (b * nc + c, off + h))
    grow = pl.BlockSpec((None, None, None, 1, L), lambda b, h, c: (b, h, c, 0, 0))
    gcol = pl.BlockSpec((None, None, None, L, 1), lambda b, h, c: (b, h, c, 0, 0))
    hb = M_WIDTH // Dh
    outs = pl.pallas_call(
        _mlstm_chunk_kernel,
        grid=(B, H, nc),
        in_specs=[pblk(0), pblk(hb), pblk(2 * hb), pblk(3 * hb), pblk(4 * hb),
                  grow, grow, gcol, gcol,
                  pl.BlockSpec((CONV_WIDTH, Dh), lambda b, h, c: (0, h)),
                  pl.BlockSpec((CONV_WIDTH, Dh), lambda b, h, c: (0, hb + h)),
                  pl.BlockSpec((1, Dh), lambda b, h, c: (0, h)),
                  pl.BlockSpec((1, Dh), lambda b, h, c: (0, hb + h)),
                  pl.BlockSpec((None, 1, 1), lambda b, h, c: (h, 0, 0)),
                  pl.BlockSpec((None, 1, 1), lambda b, h, c: (h, 0, 0)),
                  pl.BlockSpec((1, Dh), lambda b, h, c: (0, h))],
        out_specs=[pl.BlockSpec((L, Dh), lambda b, h, c: (b * nc + c, h)),
                   pl.BlockSpec((None, None, Dh, Dh), lambda b, h, c: (b, h, 0, 0)),
                   pl.BlockSpec((None, None, 1, Dh), lambda b, h, c: (b, h, 0, 0)),
                   pl.BlockSpec((None, None, 1, 1), lambda b, h, c: (b, h, 0, 0))],
        out_shape=[jax.ShapeDtypeStruct((M, M_WIDTH), F32),
                   jax.ShapeDtypeStruct((B, H, Dh, Dh), F32),
                   jax.ShapeDtypeStruct((B, H, 1, Dh), F32),
                   jax.ShapeDtypeStruct((B, H, 1, 1), F32)],
        scratch_shapes=[pltpu.VMEM((Dh, Dh), F32), pltpu.VMEM((1, Dh), F32), pltpu.VMEM((1, 1), F32),
                        pltpu.VMEM((L + 8, Dh), F32), pltpu.VMEM((L + 8, Dh), F32)],
        compiler_params=_cp(("parallel", "parallel", "arbitrary")),
        name="mlstm_prompt",
    )(proj, proj, proj, proj, proj, row(gi), row(gf), colv(gi), colv(gf),
      conv_w, conv_w, conv_b.reshape(1, -1), conv_b.reshape(1, -1),
      b_i.reshape(H, 1, 1), b_f.reshape(H, 1, 1), m_norm.reshape(1, -1))
    return outs


def _mlstm_step_kernel(x_ref, conv0_ref, cw_ref, cb_ref, v_ref, gi_ref, gf_ref, bi_ref, bf_ref,
                       mo_ref, mz_ref, mn_ref, C_ref, n_ref, m_ref,
                       out_ref, conv_out, C_out, n_out, m_out):
    H, Dh = v_ref.shape
    x = x_ref[...]
    qk = cb_ref[...]
    for j in range(CONV_WIDTH - 1):
        qk = qk + conv0_ref[j] * cw_ref[j]
    qk = qk + x * cw_ref[CONV_WIDTH - 1]
    for j in range(CONV_WIDTH - 2):
        conv_out[j] = conv0_ref[j + 1]
    conv_out[CONV_WIDTH - 2] = x
    qk = _silu(qk)
    q = qk[0:H]
    k = qk[H:2 * H] * (Dh ** -0.5)
    v = v_ref[...]
    li = gi_ref[...] + bi_ref[...]
    lf = _log_sigmoid(gf_ref[...] + bf_ref[...])
    m0 = m_ref[...]
    inter = lf + m0
    m_t = jnp.maximum(inter, li)
    w_intra = jnp.exp(li - m_t)
    w_inter = jnp.exp(inter - m_t)
    s = jnp.sum(q * k, axis=-1, keepdims=True) * w_intra
    n0 = n_ref[...]
    hrow = lax.broadcasted_iota(jnp.int32, (H, Dh), 0)
    cq = jnp.zeros((H, Dh), F32)
    vT = _dot_nt(_eye(Dh), v * w_intra, HI)
    for h in range(H):
        Ch = C_ref[h]
        cq = jnp.where(hrow == h, _dot_nt(q, Ch, HI), cq)
        C_out[h] = w_inter[h:h + 1, :] * Ch + vT[:, h:h + 1] * k[h:h + 1, :]
    num = s * v + w_inter * cq
    den = s + w_inter * jnp.sum(n0 * q, axis=-1, keepdims=True)
    hval = num / jnp.maximum(jnp.abs(den), jnp.exp(-m_t))
    n_out[...] = w_inter * n0 + w_intra * k
    m_out[...] = m_t
    hn = hval * lax.rsqrt(jnp.mean(hval * hval, axis=-1, keepdims=True) + NORM_EPS) * mn_ref[...]
    out_ref[...] = hn * jax.nn.sigmoid(mo_ref[...]) * _silu(mz_ref[...])


def mlstm_step(x16, conv0, conv_w, conv_b, v, gi, gf, b_i, b_f, mo, mz, m_norm, C0, n0, m0):
    B = x16.shape[0]
    H, Dh = M_HEADS, M_HEAD_DIM
    W = CONV_WIDTH
    per_b = lambda *shape: pl.BlockSpec((None,) + shape, lambda b: (b,) + (0,) * len(shape))
    const = lambda *shape: pl.BlockSpec(shape, lambda b: (0,) * len(shape))
    return pl.pallas_call(
        _mlstm_step_kernel,
        grid=(B,),
        in_specs=[per_b(2 * H, Dh), per_b(W - 1, 2 * H, Dh), const(W, 2 * H, Dh), const(2 * H, Dh),
                  per_b(H, Dh), per_b(H, 1), per_b(H, 1), const(H, 1), const(H, 1),
                  per_b(H, Dh), per_b(H, Dh), const(H, Dh),
                  per_b(H, Dh, Dh), per_b(H, Dh), per_b(H, 1)],
        out_specs=[per_b(H, Dh), per_b(W - 1, 2 * H, Dh), per_b(H, Dh, Dh), per_b(H, Dh), per_b(H, 1)],
        out_shape=[jax.ShapeDtypeStruct((B, H, Dh), F32),
                   jax.ShapeDtypeStruct((B, W - 1, 2 * H, Dh), F32),
                   jax.ShapeDtypeStruct((B, H, Dh, Dh), F32),
                   jax.ShapeDtypeStruct((B, H, Dh), F32),
                   jax.ShapeDtypeStruct((B, H, 1), F32)],
        compiler_params=_cp(("parallel",)),
        name="mlstm_step",
    )(x16, conv0, conv_w, conv_b, v, gi, gf, b_i, b_f, mo, mz, m_norm, C0, n0, m0)


def _rwkv_chunk_kernel(rr_ref, rk_ref, rv_ref, rz_ref, rw_ref, ra_ref, mu4_ref, mu2_ref, par_ref,
                       w2_ref, a2_ref, out_ref, S_out, S_s, carry_s):
    c = pl.program_id(2)
    nc = pl.num_programs(2)
    TB = rr_ref.shape[0]
    W = rr_ref.shape[1]
    N = R_HEAD_DIM
    L = R_CHUNK

    @pl.when(c == 0)
    def _():
        S_s[...] = jnp.zeros(S_s.shape, F32)
        carry_s[...] = jnp.zeros(carry_s.shape, F32)

    row = lax.broadcasted_iota(jnp.int32, (TB, W), 0)
    lo = lax.broadcasted_iota(jnp.int32, (TB, W), 1) < N

    def shift(x, idx, mu):
        prev = jnp.where(row == 0, carry_s[idx:idx + 1, :], pltpu.roll(x, 1, 0))
        carry_s[idx:idx + 1, :] = x[TB - 1:TB, :]
        return x + mu * (prev - x)

    r = shift(rr_ref[...], 0, mu4_ref[0:1, :])
    k = shift(rk_ref[...], 1, mu4_ref[1:2, :])
    v = shift(rv_ref[...], 2, mu4_ref[2:3, :])
    z = shift(rz_ref[...], 3, mu4_ref[3:4, :])
    xw = shift(rw_ref[...], 4, mu2_ref[0:1, :])
    xa = shift(ra_ref[...], 5, mu2_ref[1:2, :])
    w0, a0 = par_ref[0:1, :], par_ref[1:2, :]
    k_k, k_a, r_k = par_ref[2:3, :], par_ref[3:4, :], par_ref[4:5, :]
    gn_w, gn_b = par_ref[5:6, :], par_ref[6:7, :]

    def seg_sum(x):
        s0 = jnp.sum(jnp.where(lo, x, 0.0), axis=1, keepdims=True)
        s1 = jnp.sum(jnp.where(lo, 0.0, x), axis=1, keepdims=True)
        return jnp.where(lo, s0, s1)

    w_log = -_softplus(-(w0 + _dot3(jnp.tanh(xw), w2_ref[...]))) - 0.5
    logw = -jnp.exp(w_log)
    a = jax.nn.sigmoid(a0 + _dot3(xa, a2_ref[...]))
    kk = k * k_k
    kk = kk * lax.rsqrt(jnp.maximum(seg_sum(kk * kk), 1e-24))
    k2 = k * (1.0 + (a - 1.0) * k_a)
    alpha = -kk
    beta = kk * a

    tr = lax.broadcasted_iota(jnp.int32, (TB, TB), 0)
    tc = lax.broadcasted_iota(jnp.int32, (TB, TB), 1)
    in_chunk = jnp.logical_and(tc <= tr, tc >= tr - jnp.bitwise_and(tr, L - 1))
    l_hi = logw.astype(BF16)
    l_r = logw - l_hi.astype(F32)
    l_mid = l_r.astype(BF16)
    l_lo = (l_r - l_mid.astype(F32)).astype(BF16)
    b3 = _dot(jnp.where(in_chunk, 1.0, 0.0).astype(BF16), jnp.concatenate([l_hi, l_mid, l_lo], axis=1))
    b = (b3[:, 0:W] + b3[:, W:2 * W]) + b3[:, 2 * W:3 * W]

    lo_c = lax.broadcasted_iota(jnp.int32, (L, W), 1) < N

    def stack(x):
        return jnp.concatenate([jnp.where(lo_c, x, 0.0), jnp.where(lo_c, 0.0, x)], axis=0)

    ri = lax.broadcasted_iota(jnp.int32, (2 * L, 2 * L), 0)
    ci = lax.broadcasted_iota(jnp.int32, (2 * L, 2 * L), 1)
    same = jnp.where(ri >= L, 1, 0) == jnp.where(ci >= L, 1, 0)
    strict = jnp.logical_and(same, ci < ri)
    incl = jnp.logical_and(same, ci <= ri)
    eye = jnp.where(ri == ci, 1.0, 0.0)

    chunks = range(TB // L)
    sls = [slice(ch * L, (ch + 1) * L) for ch in chunks]
    bLs = [b[sl][L - 1:L, :] for sl in sls]
    a_s = [stack(alpha[sl] * jnp.exp(b[sl] - logw[sl])) for sl in sls]
    r_s = [stack(r[sl] * jnp.exp(b[sl])) for sl in sls]
    Ys_ = [jnp.concatenate([stack(beta[sl] * jnp.exp(-b[sl])), stack(k2[sl] * jnp.exp(-b[sl]))], axis=0) for sl in sls]
    bh_s = [stack(beta[sl] * jnp.exp(bL - b[sl])) for sl, bL in zip(sls, bLs)]
    kh_s = [stack(k2[sl] * jnp.exp(bL - b[sl])) for sl, bL in zip(sls, bLs)]
    v_s = [stack(v[sl]) for sl in sls]
    G = [_dot3_nt(jnp.concatenate([a_s[i], r_s[i]], axis=0), Ys_[i]) for i in chunks]
    A = [jnp.where(strict, G[i][0:2 * L, 0:2 * L], 0.0) for i in chunks]
    P = [jnp.where(incl, G[i][2 * L:4 * L, 0:2 * L], 0.0) for i in chunks]
    BQ = [jnp.concatenate([jnp.where(strict, G[i][0:2 * L, 2 * L:4 * L], 0.0),
                           jnp.where(incl, G[i][2 * L:4 * L, 2 * L:4 * L], 0.0)], axis=0) for i in chunks]
    BQV = [_dot3(BQ[i], v_s[i]) for i in chunks]
    T = [eye + A[i] for i in chunks]
    Ap = [_dot3(A[i], A[i]) for i in chunks]
    for _ in range(4):
        R = [_dot(_lhs3(Ap[i], 1), _rhs3(jnp.concatenate([Ap[i], T[i]], axis=1), 0)) for i in chunks]
        T = [T[i] + R[i][:, 2 * L:4 * L] for i in chunks]
        Ap = [R[i][:, 0:2 * L] for i in chunks]
    T = [T[i] + _dot3(Ap[i], T[i]) for i in chunks]
    TXB = [_dot3(T[i], jnp.concatenate([a_s[i], BQV[i][0:2 * L]], axis=1)) for i in chunks]
    PTX = [_dot3(P[i], TXB[i]) for i in chunks]
    MN = [_dot3_tn(TXB[i], bh_s[i]) for i in chunks]
    VK = [_dot3_tn(v_s[i], kh_s[i]) for i in chunks]
    pre = [(r_s[i] + PTX[i][:, 0:W], BQV[i][2 * L:4 * L] + PTX[i][:, W:2 * W],
            MN[i][0:W], MN[i][W:2 * W] + VK[i], jnp.exp(bLs[i])) for i in chunks]

    S = S_s[...]
    ys = []
    for Rp, Y0, Mc, Nc, gL in pre:
        Ys = _dot3_nt(Rp, S) + Y0
        ys.append(Ys[0:L] + Ys[L:2 * L])
        S = (S * gL + _dot3(S, Mc)) + Nc
    S_new = S
    S_s[...] = S_new
    y = jnp.concatenate(ys, axis=0)

    mu_y = seg_sum(y) * (1.0 / N)
    d = y - mu_y
    var = seg_sum(d * d) * (1.0 / N)
    yn = d * lax.rsqrt(var + RWKV_GN_EPS) * gn_w + gn_b
    yn = yn + seg_sum(r * k2 * r_k) * v
    out_ref[...] = yn * _silu(z)

    @pl.when(c == nc - 1)
    def _():
        S_out[...] = S_new


def rwkv_prompt(proj, B, S, mu4, mu2, par, w2p, a2p, r_off):
    L = R_STEP_TOKENS
    W = 2 * R_HEAD_DIM
    nc = S // L
    M = B * S
    npair = mu4.shape[1] // W
    blk = lambda off: pl.BlockSpec((L, W), lambda b, p, c: (b * nc + c, off + p))
    cblk = lambda off: pl.BlockSpec((L, W), lambda b, p, c: (b * nc + c, off))
    return pl.pallas_call(
        _rwkv_chunk_kernel,
        grid=(B, npair, nc),
        in_specs=[blk(r_off), blk(r_off + npair), blk(r_off + 2 * npair), blk(r_off + 3 * npair),
                  cblk(r_off + 4 * npair), cblk(r_off + 4 * npair + 1),
                  pl.BlockSpec((8, W), lambda b, p, c: (0, p)),
                  pl.BlockSpec((8, W), lambda b, p, c: (0, 0)),
                  pl.BlockSpec((8, W), lambda b, p, c: (0, p)),
                  pl.BlockSpec((W, W), lambda b, p, c: (0, p)),
                  pl.BlockSpec((W, W), lambda b, p, c: (0, p))],
        out_specs=[pl.BlockSpec((L, W), lambda b, p, c: (b * nc + c, p)),
                   pl.BlockSpec((None, None, W, W), lambda b, p, c: (b, p, 0, 0))],
        out_shape=[jax.ShapeDtypeStruct((M, npair * W), F32),
                   jax.ShapeDtypeStruct((B, npair, W, W), F32)],
        scratch_shapes=[pltpu.VMEM((W, W), F32), pltpu.VMEM((8, W), F32)],
        compiler_params=_cp(("parallel", "parallel", "arbitrary")),
        name="rwkv_prompt",
    )(proj, proj, proj, proj, proj, proj, mu4, mu2, par, w2p, a2p)


def _rwkv_prep_kernel(rr_ref, rk_ref, rv_ref, rz_ref, rw_ref, ra_ref, sh4_ref, sh2_ref, mu4_ref, mu2_ref,
                      w0_ref, a0_ref, w2_ref, a2_ref, r_o, k_o, v_o, z_o, w_o, a_o):
    Wd = rr_ref.shape[1]
    sh = lambda x, prev, mu: x + mu * (prev - x)
    r_o[...] = sh(rr_ref[...], sh4_ref[:, 0:Wd], mu4_ref[:, 0:Wd])
    k_o[...] = sh(rk_ref[...], sh4_ref[:, Wd:2 * Wd], mu4_ref[:, Wd:2 * Wd])
    v_o[...] = sh(rv_ref[...], sh4_ref[:, 2 * Wd:3 * Wd], mu4_ref[:, 2 * Wd:3 * Wd])
    z_o[...] = sh(rz_ref[...], sh4_ref[:, 3 * Wd:4 * Wd], mu4_ref[:, 3 * Wd:4 * Wd])
    xw = sh(rw_ref[...], sh2_ref[:, 0:LANES], mu2_ref[:, 0:LANES])
    xa = sh(ra_ref[...], sh2_ref[:, LANES:2 * LANES], mu2_ref[:, LANES:2 * LANES])
    w_log = -_softplus(-(w0_ref[...] + _dot(jnp.tanh(xw), w2_ref[...], HI))) - 0.5
    w_o[...] = jnp.exp(-jnp.exp(w_log))
    a_o[...] = jax.nn.sigmoid(a0_ref[...] + _dot(xa, a2_ref[...], HI))


def rwkv_prep(proj, sh4, sh2, mu4, mu2, w0, a0, w2p, a2p, r_blk, s_blk):
    B = proj.shape[0]
    Wd = w0.shape[1]
    full = lambda a: pl.BlockSpec(a.shape, lambda i: (0,) * a.ndim)
    return pl.pallas_call(
        _rwkv_prep_kernel,
        grid=(1,),
        in_specs=[pl.BlockSpec((B, Wd), lambda i: (0, r_blk)), pl.BlockSpec((B, Wd), lambda i: (0, r_blk + 1)),
                  pl.BlockSpec((B, Wd), lambda i: (0, r_blk + 2)), pl.BlockSpec((B, Wd), lambda i: (0, r_blk + 3)),
                  pl.BlockSpec((B, LANES), lambda i: (0, s_blk)), pl.BlockSpec((B, LANES), lambda i: (0, s_blk + 1)),
                  full(sh4), full(sh2), full(mu4), full(mu2), full(w0), full(a0), full(w2p), full(a2p)],
        out_specs=[pl.BlockSpec((B, Wd), lambda i: (0, 0))] * 6,
        out_shape=[jax.ShapeDtypeStruct((B, Wd), F32)] * 6,
        compiler_params=_cp(("arbitrary",)),
        name="rwkv_prep",
    )(proj, proj, proj, proj, proj, proj, sh4, sh2, mu4, mu2, w0, a0, w2p, a2p)


def _rwkv_step_kernel(r_ref, k_ref, v_ref, z_ref, w_ref, a_ref, par_ref, S_ref, out_ref, S_out):
    H, N = r_ref.shape
    r, k, v, z, w, a = (t[...] for t in (r_ref, k_ref, v_ref, z_ref, w_ref, a_ref))
    k_k, k_a, r_k, gn_w, gn_b = (par_ref[i] for i in range(5))
    kk = k * k_k
    kk = kk * lax.rsqrt(jnp.maximum(jnp.sum(kk * kk, axis=-1, keepdims=True), 1e-24))
    k2 = k * (1.0 + (a - 1.0) * k_a)
    alpha = -kk
    beta = kk * a
    vT = _dot_nt(_eye(N), v, HI)
    lane_h = lax.broadcasted_iota(jnp.int32, (N, H), 1)
    yT = jnp.zeros((N, H), F32)
    for h in range(H):
        S = S_ref[h]
        sa = jnp.sum(S * alpha[h:h + 1, :], axis=-1, keepdims=True)
        Sn = S * w[h:h + 1, :] + sa * beta[h:h + 1, :] + vT[:, h:h + 1] * k2[h:h + 1, :]
        S_out[h] = Sn
        yT = jnp.where(lane_h == h, jnp.sum(Sn * r[h:h + 1, :], axis=-1, keepdims=True), yT)
    y = _dot_nt(_eye(H), yT, HI)
    mu_y = jnp.mean(y, axis=-1, keepdims=True)
    d = y - mu_y
    var = jnp.mean(d * d, axis=-1, keepdims=True)
    yn = d * lax.rsqrt(var + RWKV_GN_EPS) * gn_w + gn_b
    yn = yn + jnp.sum(r * k2 * r_k, axis=-1, keepdims=True) * v
    out_ref[...] = yn * _silu(z)


def rwkv_step(r, k, v, z, w, a, par, S0):
    B, H, N = r.shape
    vec = pl.BlockSpec((None, H, N), lambda b: (b, 0, 0))
    mat = pl.BlockSpec((None, H, N, N), lambda b: (b, 0, 0, 0))
    return pl.pallas_call(
        _rwkv_step_kernel,
        grid=(B,),
        in_specs=[vec] * 6 + [pl.BlockSpec(par.shape, lambda b: (0, 0, 0)), mat],
        out_specs=[vec, mat],
        out_shape=[jax.ShapeDtypeStruct((B, H, N), F32), jax.ShapeDtypeStruct((B, H, N, N), F32)],
        compiler_params=_cp(("parallel",)),
        name="rwkv_step",
    )(r, k, v, z, w, a, par, S0)


def _qk_norm_kernel(q_ref, k_ref, v_ref, f_ref, gq_ref, gk_ref, bf_ref, qn_ref, kn_ref, lf_ref, *mxu_refs):
    D = F_HEAD_DIM
    for h in range(q_ref.shape[1] // D):
        sl = slice(h * D, (h + 1) * D)
        x = q_ref[:, sl]
        qn = x * lax.rsqrt(jnp.mean(x * x, axis=-1, keepdims=True) + NORM_EPS) * gq_ref[...]
        y = k_ref[:, sl]
        kn = y * lax.rsqrt(jnp.mean(y * y, axis=-1, keepdims=True) + NORM_EPS) * gk_ref[...]
        qn_ref[:, sl] = qn
        kn_ref[:, sl] = kn
        if mxu_refs:
            qb_ref, kb_ref, vb_ref = mxu_refs
            qb_ref[:, sl] = (qn * (D ** -0.5 * LOG2E)).astype(BF16)
            kb_ref[:, sl] = kn.astype(BF16)
            vb_ref[:, sl] = v_ref[:, sl].astype(BF16)
    lf_ref[...] = _log_sigmoid(f_ref[...] + bf_ref[...])


def qk_norm(proj, gq, gk, bf_pad, tm, Wd, f_blk, mxu_copies):
    M = proj.shape[0]
    wide = pl.BlockSpec((tm, Wd), lambda i: (i, 0))
    n_extra = 3 if mxu_copies else 0
    return pl.pallas_call(
        _qk_norm_kernel,
        grid=(M // tm,),
        in_specs=[pl.BlockSpec((tm, Wd), lambda i: (i, 0)), pl.BlockSpec((tm, Wd), lambda i: (i, 1)),
                  pl.BlockSpec((tm, Wd), lambda i: (i, 2)),
                  pl.BlockSpec((tm, LANES), lambda i: (i, f_blk)),
                  pl.BlockSpec((1, F_HEAD_DIM), lambda i: (0, 0)), pl.BlockSpec((1, F_HEAD_DIM), lambda i: (0, 0)),
                  pl.BlockSpec((1, LANES), lambda i: (0, 0))],
        out_specs=[wide, wide, pl.BlockSpec((tm, LANES), lambda i: (i, 0))] + [wide] * n_extra,
        out_shape=[jax.ShapeDtypeStruct((M, Wd), F32), jax.ShapeDtypeStruct((M, Wd), F32),
                   jax.ShapeDtypeStruct((M, LANES), F32)] + [jax.ShapeDtypeStruct((M, Wd), BF16)] * n_extra,
        compiler_params=_cp(("parallel",)),
        name="qk_norm",
    )(proj, proj, proj, proj, gq.reshape(1, -1), gk.reshape(1, -1), bf_pad)


def _cumsum_kernel(x_ref, c_ref):
    H, S = x_ref.shape
    U = (lax.broadcasted_iota(jnp.int32, (LANES, LANES), 0) <= lax.broadcasted_iota(jnp.int32, (LANES, LANES), 1)).astype(F32)
    off = jnp.zeros((H, 1), F32)
    for t in range(S // LANES):
        sl = slice(t * LANES, (t + 1) * LANES)
        w = _dot(x_ref[:, sl], U, HI) + off
        c_ref[:, sl] = w
        off = w[:, LANES - 1:LANES]


def cumsum_lanes(x):
    B, H, S = x.shape
    return pl.pallas_call(
        _cumsum_kernel,
        grid=(B,),
        in_specs=[pl.BlockSpec((None, H, S), lambda b: (b, 0, 0))],
        out_specs=pl.BlockSpec((None, H, S), lambda b: (b, 0, 0)),
        out_shape=jax.ShapeDtypeStruct((B, H, S), F32),
        compiler_params=_cp(("parallel",)),
        name="forget_cumsum",
    )(x)


def _fox_flash_kernel(q_ref, k_ref, v_ref, ck_ref, o_ref, m_s, l_s, acc_s):
    i = pl.program_id(2)
    t, D = q_ref.shape
    q = q_ref[...]
    m_s[...] = jnp.full(m_s.shape, -jnp.inf, F32)
    l_s[...] = jnp.zeros(l_s.shape, F32)
    acc_s[...] = jnp.zeros(acc_s.shape, F32)

    def block(j, masked):
        start = pl.multiple_of(j * t, t)
        s = _dot_nt(q, k_ref[pl.ds(start, t), :]) - ck_ref[j] * LOG2E
        if masked:
            row = lax.broadcasted_iota(jnp.int32, (t, t), 0)
            col = lax.broadcasted_iota(jnp.int32, (t, t), 1)
            s = jnp.where(col <= row, s, -jnp.inf)
        m_prev = m_s[...]
        m_new = jnp.maximum(m_prev, jnp.max(s, axis=1, keepdims=True))
        alpha = jnp.exp2(m_prev - m_new)
        p = jnp.exp2(s - pltpu.repeat(m_new, t // LANES, axis=1))
        l_s[...] = alpha * l_s[...] + jnp.sum(p, axis=1, keepdims=True)
        acc_s[...] = alpha * acc_s[...] + _dot(p.astype(BF16), v_ref[pl.ds(start, t), :])
        m_s[...] = m_new

    def body(j, carry):
        block(j, False)
        return carry

    lax.fori_loop(0, i, body, 0)
    block(i, True)
    o_ref[...] = acc_s[...] / l_s[...]


def fox_prompt_attn(qb, kb, vb, ck, B, S, t):
    D = F_HEAD_DIM
    H = qb.shape[1] // D
    nb = S // t
    full = pl.BlockSpec((S, D), lambda b, h, i: (b, h))
    return pl.pallas_call(
        _fox_flash_kernel,
        grid=(B, H, nb),
        in_specs=[pl.BlockSpec((t, D), lambda b, h, i: (b * nb + i, h)), full, full,
                  pl.BlockSpec((None, None, nb, 1, t), lambda b, h, i: (b, h, 0, 0, 0))],
        out_specs=pl.BlockSpec((t, D), lambda b, h, i: (b * nb + i, h)),
        out_shape=jax.ShapeDtypeStruct((B * S, H * D), F32),
        scratch_shapes=[pltpu.VMEM((t, LANES), F32), pltpu.VMEM((t, LANES), F32), pltpu.VMEM((t, D), F32)],
        compiler_params=_cp(("parallel", "parallel", "parallel")),
        name="fox_prompt_attn",
    )(qb, kb, vb, ck)


def _fox_decode_kernel(pt_ref, q_ref, kn_ref, vn_ref, lfn_ref, *refs, npg):
    k_refs, v_refs, lf_refs = refs[0:npg], refs[npg:2 * npg], refs[2 * npg:3 * npg]
    o_ref = refs[3 * npg]
    m_s, l_s, acc_s, off_s, cp_s = refs[3 * npg + 1:]
    p = pl.program_id(1)
    H, D = q_ref.shape
    T = k_refs[0].shape[0]
    pages = range(npg)

    @pl.when(p == 0)
    def _():
        m_s[...] = jnp.full(m_s.shape, -jnp.inf, F32)
        l_s[...] = jnp.zeros(l_s.shape, F32)
        acc_s[...] = jnp.zeros(acc_s.shape, F32)
        off_s[...] = jnp.zeros(off_s.shape, F32)
        cp_s[...] = jnp.zeros(cp_s.shape, F32)

    q = q_ref[...] * (D ** -0.5 * LOG2E)
    hmask = (lax.broadcasted_iota(jnp.int32, (H, LANES), 0) == lax.broadcasted_iota(jnp.int32, (H, LANES), 1)).astype(F32)
    hmask2 = hmask * LOG2E
    trow = lax.broadcasted_iota(jnp.int32, (T, LANES), 0)
    for g in pages:
        cp_s[g, :, 0:H] = lf_refs[g][...]
    cps = [cp_s[g] for g in pages]
    sh = 1
    while sh < T:
        cps = [x + jnp.where(trow >= sh, pltpu.roll(x, sh, 0), 0.0) for x in cps]
        sh *= 2
    for g in pages:
        cp_s[g] = cps[g]
    cbm = [jnp.stack([jnp.broadcast_to(cp_s[g, t:t + 1, :], (H, LANES)) for t in range(T)], axis=0) * hmask2[None]
           for g in pages]
    s = [jnp.sum(k_refs[g][...] * q[None] - cbm[g], axis=-1, keepdims=True) for g in pages]
    tot = [jnp.sum(jnp.broadcast_to(cp_s[g, T - 1:T, :], (H, LANES)) * hmask, axis=-1, keepdims=True) for g in pages]
    mg = [jnp.max(s[g], axis=0) for g in pages]
    pr = [jnp.exp2(s[g] - mg[g][None]) for g in pages]
    lg = [jnp.sum(pr[g], axis=0) for g in pages]
    ag = [jnp.sum(pr[g] * v_refs[g][...], axis=0) for g in pages]
    off = off_s[:, 0:1]
    m_prev = m_s[:, 0:1]
    offs = []
    off_new = off
    for g in pages:
        offs.append(off_new * LOG2E)
        off_new = off_new + tot[g]
    m_new = m_prev
    for g in pages:
        m_new = jnp.maximum(m_new, mg[g] - offs[g])
    alpha = jnp.exp2(m_prev - m_new)
    l_new = alpha * l_s[:, 0:1]
    acc_new = alpha * acc_s[...]
    for g in pages:
        wg = jnp.exp2(mg[g] - offs[g] - m_new)
        l_new = l_new + wg * lg[g]
        acc_new = acc_new + wg * ag[g]
    m_s[...] = jnp.broadcast_to(m_new, m_s.shape)
    l_s[...] = jnp.broadcast_to(l_new, l_s.shape)
    acc_s[...] = acc_new
    off_s[...] = jnp.broadcast_to(off_new, off_s.shape)

    @pl.when(p == pl.num_programs(1) - 1)
    def _():
        cq = (off_new + lfn_ref[:, 0:1]) * LOG2E
        s_self = jnp.sum(q * kn_ref[...], axis=-1, keepdims=True) - cq
        m_f = jnp.maximum(m_new, s_self)
        a2 = jnp.exp2(m_new - m_f)
        ps = jnp.exp2(s_self - m_f)
        o_ref[...] = (a2 * acc_new + ps * vn_ref[...]) / (a2 * l_new + ps)


def fox_decode_attn(q, k_new, v_new, lf_new, cache_k, cache_v, cache_lf, page_table):
    B, H, D = q.shape
    n_pages = page_table.shape[1]
    T = cache_k.shape[1]
    npg = DEC_PAGES_PER_STEP if n_pages % DEC_PAGES_PER_STEP == 0 else 1
    vec = pl.BlockSpec((None, H, D), lambda b, p, pt: (b, 0, 0))
    kv = lambda g: pl.BlockSpec((None, T, H, D), lambda b, p, pt: (pt[b, p * npg + g], 0, 0, 0))
    lfs = lambda g: pl.BlockSpec((None, T, H), lambda b, p, pt: (pt[b, p * npg + g], 0, 0))
    grid_spec = pltpu.PrefetchScalarGridSpec(
        num_scalar_prefetch=1,
        grid=(B, n_pages // npg),
        in_specs=[vec, vec, vec, vec] + [kv(g) for g in range(npg)] * 2 + [lfs(g) for g in range(npg)],
        out_specs=pl.BlockSpec((None, H, D), lambda b, p, pt: (b, 0, 0)),
        scratch_shapes=[pltpu.VMEM((H, D), F32)] * 4 + [pltpu.VMEM((npg, T, LANES), F32)],
    )
    return pl.pallas_call(
        functools.partial(_fox_decode_kernel, npg=npg),
        grid_spec=grid_spec,
        out_shape=jax.ShapeDtypeStruct((B, H, D), F32),
        compiler_params=_cp(("parallel", "arbitrary")),
        name="fox_decode_attn",
    )(page_table, q, k_new, v_new, lf_new, *([cache_k] * npg), *([cache_v] * npg), *([cache_lf] * npg))


def _pad_cols(a, width):
    return jnp.pad(a, [(0, 0)] * (a.ndim - 1) + [(0, width - a.shape[-1])])


def _split_mix(a):
    W, H, D = M_WIDTH, M_HEADS, a.shape[-1]
    o = 0
    parts = {}
    for name, n in (("mqk", 2 * W), ("mv", W), ("mi", H), ("mf", H), ("mo", W), ("mz", W)):
        parts[name] = a[..., o:o + n]
        o += n
    Wr = (D - o - 2 * R_LORA) // 4
    for name, n in (("rr", Wr), ("rk", Wr), ("rv", Wr), ("rw", R_LORA), ("ra", R_LORA), ("rz", Wr)):
        parts[name] = a[..., o:o + n]
        o += n
    return parts


def _small_mix(p):
    z = jnp.zeros(p["rw"].shape[:-1] + (LANES - R_LORA,), p["rw"].dtype)
    return _pad_cols(jnp.concatenate([p["rw"], z, p["ra"], z, p["mi"], p["mf"]], axis=-1), SMALL_W)


def _arrange_mix(a):
    p = _split_mix(a)
    return jnp.concatenate([p["mqk"], p["mv"], p["mo"], p["mz"], p["rr"], p["rk"], p["rv"], p["rz"], _small_mix(p)],
                           axis=-1)


def _trunk(x, p_in, state, attn, Wt, tm):
    B, T, D = x.shape
    M = B * T
    x2 = x.reshape(M, D)
    tn = 512
    tm_in = 2 * tm if M % (2 * tm) == 0 else tm
    Wr = D
    mixw = Wt["mix_w"]
    proj = norm_matmul(x2, Wt["norm_w"][0], mixw, tm_in, tn)
    W = M_WIDTH
    small0 = 3 * W + 2 * W + 4 * Wr
    gi = proj[:, small0 + 2 * LANES: small0 + 2 * LANES + M_HEADS]
    gf = proj[:, small0 + 2 * LANES + M_HEADS: small0 + 2 * LANES + 2 * M_HEADS]
    r_col0 = 5 * W

    if state is None:
        bm, C, n, m = mlstm_prompt(proj, B, T, gi.reshape(B, T, M_HEADS), gf.reshape(B, T, M_HEADS),
                                   Wt["m_conv_w"], Wt["m_conv_b"], Wt["m_b_i"], Wt["m_b_f"], Wt["m_norm"])
        conv_new = proj[:, 0:2 * W].reshape(B, T, 2 * W)[:, T - (CONV_WIDTH - 1):]
        C = C
        n = n.reshape(B, M_HEADS, M_HEAD_DIM)
        m = m.reshape(B, M_HEADS)
        br, Sbd = rwkv_prompt(proj, B, T, Wt["mu4"], Wt["mu2"], Wt["r_par"], Wt["w2p"], Wt["a2p"],
                              r_col0 // LANES)
        N = R_HEAD_DIM
        S_new = jnp.stack([Sbd[:, :, 0:N, 0:N], Sbd[:, :, N:2 * N, N:2 * N]], axis=2).reshape(B, Wr // N, N, N)
    else:
        conv0, C0, n0, m0, S0, sh0 = state
        H, Dh = M_HEADS, M_HEAD_DIM
        bm, conv_new, C, n, m = mlstm_step(
            proj[:, 0:2 * W].reshape(B, 2 * H, Dh), conv0.reshape(B, CONV_WIDTH - 1, 2 * H, Dh),
            Wt["m_conv_w"].reshape(CONV_WIDTH, 2 * H, Dh), Wt["m_conv_b"].reshape(2 * H, Dh),
            proj[:, 2 * W:3 * W].reshape(B, H, Dh), gi.reshape(B, H, 1), gf.reshape(B, H, 1),
            Wt["m_b_i"].reshape(H, 1), Wt["m_b_f"].reshape(H, 1),
            proj[:, 3 * W:4 * W].reshape(B, H, Dh), proj[:, 4 * W:5 * W].reshape(B, H, Dh),
            Wt["m_norm"].reshape(H, Dh), C0, n0, m0.reshape(B, H, 1))
        bm = bm.reshape(M, W)
        conv_new = conv_new.reshape(B, CONV_WIDTH - 1, 2 * W)
        m = m.reshape(B, H)
        shp = _split_mix(jnp.concatenate([jnp.zeros((B, 5 * W + 2 * M_HEADS), F32), sh0], axis=-1))
        sh4 = jnp.concatenate([shp["rr"], shp["rk"], shp["rv"], shp["rz"]], axis=-1)
        sh2 = _small_mix(shp)[:, 0:2 * LANES]
        N = R_HEAD_DIM
        Hr = Wr // N
        r_, k_, v_, z_, w_, a_ = rwkv_prep(proj, sh4, sh2, Wt["mu4_flat"],
                                           Wt["mu2_flat"], Wt["r_par"][0:1], Wt["r_par"][1:2], Wt["w2p"], Wt["a2p"],
                                           r_col0 // Wr, small0 // LANES)
        hs = lambda t: t.reshape(B, Hr, N)
        par5 = Wt["r_par"][2:7].reshape(5, Hr, N)
        br, S_new = rwkv_step(hs(r_), hs(k_), hs(v_), hs(z_), hs(w_), hs(a_), par5, S0)
        br = br.reshape(M, Wr)
    sh_new = jnp.concatenate([proj[:, r_col0:r_col0 + 3 * Wr], proj[:, small0:small0 + R_LORA],
                              proj[:, small0 + LANES:small0 + LANES + R_LORA],
                              proj[:, r_col0 + 3 * Wr:r_col0 + 4 * Wr]], axis=-1).reshape(B, T, -1)[:, T - 1]

    h1 = out_mix(bm, br, Wt["out_mix_w"], x2, tm, tn)
    h1 = ple_add(h1, Wt["ple_norm"][0], Wt["ple_gate_w"][0], p_in[0].reshape(M, -1), Wt["ple_proj"][0], tm, tn)

    proj1 = norm_matmul(h1, Wt["norm_w"][1], Wt["fox_w"], tm_in, tn)
    qn, kn, lf, *mxu = qk_norm(proj1, Wt["f_q_norm"], Wt["f_k_norm"], Wt["f_b_f_pad"], min(tm, 256), D,
                               4 * D // LANES, state is None)
    Hf = D // F_HEAD_DIM
    lf = lf[:, 0:Hf]
    o = attn(qn, kn, proj1, lf, mxu, B, T)
    h2 = out_fox(o, proj1, 3, Wt["out_fox_w"], h1, tm, tn)
    h2 = ple_add(h2, Wt["ple_norm"][1], Wt["ple_gate_w"][1], p_in[1].reshape(M, -1), Wt["ple_proj"][1], tm, tn)
    y = rmsnorm_rows(h2, Wt["final_norm"], min(tm, 256)).reshape(B, T, D)

    k_rows = kn.reshape(1, B, T, Hf, F_HEAD_DIM)
    v_rows = proj1[:, 2 * D:3 * D].reshape(1, B, T, Hf, F_HEAD_DIM)
    lf_rows = lf.reshape(1, B, T, Hf)
    mix_state = tuple(t[None] for t in (conv_new, C, n, m, S_new, sh_new))
    return y, (k_rows, v_rows, lf_rows), mix_state


def kernel(x_prompt, x_sample, cache_k, cache_v, cache_lf, state_mlstm_conv, state_mlstm_C, state_mlstm_n,
           state_mlstm_m, state_rwkv_S, state_rwkv_shift, page_table, p_prompt, p_sample,
           norm_w, final_norm, w_in_mix, w_out_mix, m_conv_w, m_conv_b, m_b_i, m_b_f, m_norm,
           r_mu, r_w0, r_w2, r_a0, r_a2, r_k_k, r_k_a, r_r_k, r_gn_w, r_gn_b,
           w_in_fox, w_out_fox, f_b_f, f_q_norm, f_k_norm, ple_proj, ple_gate_w, ple_norm):
    D = x_prompt.shape[-1]
    Wr = r_w0.shape[-1]
    Hf = D // F_HEAD_DIM
    mix_w = _arrange_mix(w_in_mix[0]).astype(BF16)
    fw = w_in_fox[0]
    fox_w = jnp.concatenate([fw[:, 0:3 * D], fw[:, 3 * D + Hf:4 * D + Hf], _pad_cols(fw[:, 3 * D:3 * D + Hf], SMALL_W)],
                            axis=-1).astype(BF16)
    mu_full = jnp.concatenate([jnp.zeros((5 * M_WIDTH + 2 * M_HEADS,), F32), r_mu[0]])
    mup = _split_mix(mu_full)
    mu4_flat = jnp.concatenate([mup["rr"], mup["rk"], mup["rv"], mup["rz"]])[None]
    mu2_flat = _small_mix(mup)[None, 0:2 * LANES]
    mu4 = _pad_cols(jnp.stack([mup["rr"], mup["rk"], mup["rv"], mup["rz"]]).T, 8).T
    mu2 = _pad_cols(mu2_flat.reshape(2, LANES).T, 8).T
    r_par = jnp.stack([r_w0[0], r_a0[0], r_k_k[0], r_k_a[0], r_r_k[0], r_gn_w[0], r_gn_b[0], jnp.zeros((Wr,), F32)])
    pad_rows = lambda a: jnp.pad(a, ((0, LANES - a.shape[0]), (0, 0)))
    Wt = dict(mix_w=mix_w, fox_w=fox_w, norm_w=norm_w, final_norm=final_norm,
              out_mix_w=w_out_mix[0].astype(BF16), out_fox_w=w_out_fox[0].astype(BF16),
              m_conv_w=m_conv_w[0], m_conv_b=m_conv_b[0], m_b_i=m_b_i[0], m_b_f=m_b_f[0], m_norm=m_norm[0],
              mu4=mu4, mu2=mu2, mu4_flat=mu4_flat, mu2_flat=mu2_flat, r_par=r_par,
              w2p=pad_rows(r_w2[0]), a2p=pad_rows(r_a2[0]),
              f_q_norm=f_q_norm[0], f_k_norm=f_k_norm[0], f_b_f_pad=_pad_cols(f_b_f[0][None], LANES),
              ple_proj=ple_proj.astype(BF16), ple_gate_w=ple_gate_w.astype(BF16), ple_norm=ple_norm)

    def attn_prompt(qn, kn, proj1, lf, mxu, B, T):
        lfT = lf.reshape(B, T, Hf).transpose(0, 2, 1)
        c = cumsum_lanes(lfT)
        t = FOX_BLOCK if T % FOX_BLOCK == 0 else T
        return fox_prompt_attn(*mxu, c.reshape(B, Hf, T // t, 1, t), B, T, t)

    def attn_sample(qn, kn, proj1, lf, mxu, B, T):
        r3 = lambda t: t.reshape(B, Hf, F_HEAD_DIM)
        lfb = jnp.broadcast_to(lf.reshape(B, Hf, 1), (B, Hf, F_HEAD_DIM))
        o = fox_decode_attn(r3(qn), r3(kn), r3(proj1[:, 2 * D:3 * D]), lfb,
                            cache_k[0], cache_v[0], cache_lf[0], page_table)
        return o.reshape(B, D)

    Bp, Tp, _ = x_prompt.shape
    y_p, (k_p, v_p, lf_p), mix_p = _trunk(x_prompt, p_prompt.reshape(p_prompt.shape[0], Bp * Tp, -1), None,
                                          attn_prompt, Wt, 512)
    Bs, Ts, _ = x_sample.shape
    state = (state_mlstm_conv[0], state_mlstm_C[0], state_mlstm_n[0], state_mlstm_m[0], state_rwkv_S[0],
             state_rwkv_shift[0])
    y_s, (k_s, v_s, lf_s), mix_s = _trunk(x_sample, p_sample.reshape(p_sample.shape[0], Bs * Ts, -1), state,
                                          attn_sample, Wt, Bs * Ts)
    return (y_p, y_s, k_p, v_p, lf_p) + mix_p + (k_s, v_s, lf_s) + mix_s
```

```python
import functools

import jax
import jax.numpy as jnp
from jax import lax
from jax.experimental import pallas as pl
from jax.experimental.pallas import tpu as pltpu

F32 = jnp.float32
BF16 = jnp.bfloat16
HI = lax.Precision.HIGHEST

NORM_EPS = 1e-6
RWKV_GN_EPS = 64e-5
M_HEADS = 8
M_HEAD_DIM = 256
M_WIDTH = M_HEADS * M_HEAD_DIM
CONV_WIDTH = 4
M_CHUNK = 128
R_HEAD_DIM = 64
R_LORA = 64
R_CHUNK = 64
R_STEP_TOKENS = 512
F_HEAD_DIM = 128
DEC_PAGES_PER_STEP = 4
LOG2E = 1.4426950408889634
FOX_BLOCK = 512
LANES = 128
SMALL_W = 512
VMEM_LIMIT = 48 * 1024 * 1024


def _cp(sem, vmem=VMEM_LIMIT):
    return pltpu.CompilerParams(dimension_semantics=sem, vmem_limit_bytes=vmem)


def _dot(a, b, prec=None):
    return jnp.dot(a, b, preferred_element_type=F32, precision=prec)


def _dot_nt(a, b, prec=None):
    return lax.dot_general(a, b, (((1,), (1,)), ((), ())), preferred_element_type=F32, precision=prec)


def _dot_tn(a, b, prec=None):
    return lax.dot_general(a, b, (((0,), (0,)), ((), ())), preferred_element_type=F32, precision=prec)


def _split_bf16(x):
    hi = x.astype(BF16)
    return hi, (x - hi.astype(F32)).astype(BF16)


def _lhs3(x, axis):
    hi, lo = _split_bf16(x)
    return jnp.concatenate([hi, hi, lo], axis=axis)


def _rhs3(x, axis):
    hi, lo = _split_bf16(x)
    return jnp.concatenate([hi, lo, hi], axis=axis)


def _dot3(a, b):
    return _dot(_lhs3(a, 1), _rhs3(b, 0))


def _dot3_nt(a, b):
    return _dot_nt(_lhs3(a, 1), _rhs3(b, 1))


def _dot3_tn(a, b):
    return _dot_tn(_lhs3(a, 0), _rhs3(b, 0))


def _softplus(y):
    return jnp.maximum(y, 0.0) + jnp.log1p(jnp.exp(-jnp.abs(y)))


def _log_sigmoid(x):
    return -_softplus(-x)


def _silu(x):
    return x * jax.nn.sigmoid(x)


def _eye(n):
    return (lax.broadcasted_iota(jnp.int32, (n, n), 0) == lax.broadcasted_iota(jnp.int32, (n, n), 1)).astype(F32)


def _norm_mm_kernel(x_ref, g_ref, w_ref, o_ref, xn_ref):
    @pl.when(pl.program_id(1) == 0)
    def _():
        x = x_ref[...]
        ms = jnp.mean(x * x, axis=-1, keepdims=True)
        xn_ref[...] = (x * lax.rsqrt(ms + NORM_EPS) * g_ref[...]).astype(BF16)

    o_ref[...] = _dot(xn_ref[...], w_ref[...])


def norm_matmul(x, g, w, tm, tn):
    M, K = x.shape
    N = w.shape[1]
    return pl.pallas_call(
        _norm_mm_kernel,
        grid=(M // tm, N // tn),
        in_specs=[pl.BlockSpec((tm, K), lambda i, j: (i, 0)),
                  pl.BlockSpec((1, K), lambda i, j: (0, 0)),
                  pl.BlockSpec((K, tn), lambda i, j: (0, j))],
        out_specs=pl.BlockSpec((tm, tn), lambda i, j: (i, j)),
        out_shape=jax.ShapeDtypeStruct((M, N), F32),
        scratch_shapes=[pltpu.VMEM((tm, K), BF16)],
        compiler_params=_cp(("parallel", "arbitrary")),
        name="norm_matmul",
    )(x, g.reshape(1, K), w)


def _out_mix_kernel(a_ref, b_ref, wa_ref, wb_ref, res_ref, o_ref):
    o_ref[...] = res_ref[...] + (_dot(a_ref[...].astype(BF16), wa_ref[...]) + _dot(b_ref[...].astype(BF16), wb_ref[...]))


def out_mix(a, b, w, res, tm):
    M, K = a.shape
    N = w.shape[1]
    rows = lambda n: pl.BlockSpec((tm, n), lambda i: (i, 0))
    return pl.pallas_call(
        _out_mix_kernel,
        grid=(M // tm,),
        in_specs=[rows(K), rows(K),
                  pl.BlockSpec((K, N), lambda i: (0, 0)),
                  pl.BlockSpec((K, N), lambda i: (1, 0)),
                  rows(N)],
        out_specs=rows(N),
        out_shape=jax.ShapeDtypeStruct((M, N), F32),
        compiler_params=_cp(("parallel",)),
        name="out_mix",
    )(a, b, w, w, res)


def _out_fox_kernel(o_in_ref, z_ref, w_ref, res_ref, o_ref):
    g = (o_in_ref[...] * _silu(z_ref[...])).astype(BF16)
    o_ref[...] = res_ref[...] + _dot(g, w_ref[...])


def out_fox(o, proj, z_blk, w, res, tm):
    M, K = o.shape
    N = w.shape[1]
    return pl.pallas_call(
        _out_fox_kernel,
        grid=(M // tm,),
        in_specs=[pl.BlockSpec((tm, K), lambda i: (i, 0)),
                  pl.BlockSpec((tm, K), lambda i: (i, z_blk)),
                  pl.BlockSpec((K, N), lambda i: (0, 0)),
                  pl.BlockSpec((tm, N), lambda i: (i, 0))],
        out_specs=pl.BlockSpec((tm, N), lambda i: (i, 0)),
        out_shape=jax.ShapeDtypeStruct((M, N), F32),
        compiler_params=_cp(("parallel",)),
        name="out_fox",
    )(o, proj, w, res)


def _ple_kernel(h_ref, g_ref, wg_ref, p_ref, pw_ref, *rest, final):
    x = h_ref[...]
    hn = (x * lax.rsqrt(jnp.mean(x * x, axis=-1, keepdims=True) + NORM_EPS) * g_ref[...]).astype(BF16)
    gate = jax.nn.sigmoid(_dot(hn, wg_ref[...]))
    e = _dot(p_ref[...].astype(BF16), pw_ref[...])
    out = x + gate * e
    if final:
        fg_ref, o_ref = rest
        out = out * lax.rsqrt(jnp.mean(out * out, axis=-1, keepdims=True) + NORM_EPS) * fg_ref[...]
    else:
        (o_ref,) = rest
    o_ref[...] = out


def ple_add(h, g, wg, p, pw, tm, final_g=None):
    M, K = h.shape
    N = wg.shape[1]
    P = p.shape[1]
    final = final_g is not None
    rows = pl.BlockSpec((tm, N), lambda i: (i, 0))
    vec = pl.BlockSpec((1, K), lambda i: (0, 0))
    outs = pl.pallas_call(
        functools.partial(_ple_kernel, final=final),
        grid=(M // tm,),
        in_specs=[pl.BlockSpec((tm, K), lambda i: (i, 0)), vec,
                  pl.BlockSpec((K, N), lambda i: (0, 0)),
                  pl.BlockSpec((tm, P), lambda i: (i, 0)),
                  pl.BlockSpec((P, N), lambda i: (0, 0))] + ([vec] if final else []),
        out_specs=rows,
        out_shape=jax.ShapeDtypeStruct((M, N), F32),
        compiler_params=_cp(("parallel",)),
        name="ple_add",
    )(h, g.reshape(1, K), wg, p, pw, *([final_g.reshape(1, N)] if final else []))
    return outs


def _mlstm_chunk_kernel(qp_ref, kp_ref, v_ref, mo_ref, mz_ref, g_ref, gb_ref, cwq_ref, cwk_ref, mn_ref,
                        out_ref, C_out, n_out, m_out,
                        C_s, n_s, m_s, qbuf, kbuf):
    c = pl.program_id(2)
    nc = pl.num_programs(2)
    L = qp_ref.shape[0]
    Dh = qp_ref.shape[1]

    @pl.when(c == 0)
    def _():
        C_s[...] = jnp.zeros(C_s.shape, F32)
        n_s[...] = jnp.zeros(n_s.shape, F32)
        m_s[...] = jnp.zeros(m_s.shape, F32)
        qbuf[0:8, :] = jnp.zeros((8, Dh), F32)
        kbuf[0:8, :] = jnp.zeros((8, Dh), F32)

    qbuf[8:8 + L, :] = qp_ref[...]
    kbuf[8:8 + L, :] = kp_ref[...]

    def conv(buf, w_ref):
        acc = w_ref[CONV_WIDTH:CONV_WIDTH + 1, :]
        for j in range(CONV_WIDTH):
            acc = acc + buf[8 - (CONV_WIDTH - 1) + j: 8 - (CONV_WIDTH - 1) + j + L, :] * w_ref[j:j + 1, :]
        return acc

    q = _silu(conv(qbuf, cwq_ref))
    k = _silu(conv(kbuf, cwk_ref)) * (Dh ** -0.5)
    qbuf[0:8, :] = qbuf[L:L + 8, :]
    kbuf[0:8, :] = kbuf[L:L + 8, :]
    v = v_ref[...]

    g = g_ref[...] + gb_ref[...]
    li_r = g[0:1, :]
    lf_r = _log_sigmoid(g[1:2, :])
    row = lax.broadcasted_iota(jnp.int32, (L, L), 0)
    col = lax.broadcasted_iota(jnp.int32, (L, L), 1)
    tri = row >= col
    diag = row == col
    b_c = jnp.sum(jnp.where(tri, lf_r, 0.0), axis=1, keepdims=True)
    b_r = jnp.sum(jnp.where(diag, b_c, 0.0), axis=0, keepdims=True)
    li_c = jnp.sum(jnp.where(diag, li_r, 0.0), axis=1, keepdims=True)
    m_prev = m_s[...]
    dmat = jnp.where(tri, b_c - b_r + li_r, -jnp.inf)
    inter = b_c + m_prev
    m_t = jnp.maximum(inter, jnp.max(dmat, axis=1, keepdims=True))
    w_intra = jnp.exp(dmat - m_t)
    w_inter = jnp.exp(inter - m_t)
    qb = q.astype(BF16)
    kb = k.astype(BF16)
    vb = v.astype(BF16)
    C = C_s[...]
    n_row = n_s[...]
    s = _dot_nt(qb, kb) * w_intra
    num = _dot(s.astype(BF16), vb) + w_inter * _dot_nt(qb, C.astype(BF16))
    den = jnp.sum(s, axis=1, keepdims=True) + w_inter * jnp.sum(q * n_row, axis=1, keepdims=True)
    hval = num / jnp.maximum(jnp.abs(den), jnp.exp(-m_t))
    m_new = m_t[L - 1:L, :]
    bL = b_c[L - 1:L, :]
    cd = jnp.exp(bL + m_prev - m_new)
    w_c = jnp.exp(bL - b_c + li_c - m_new)
    w_r = jnp.exp(bL - b_r + li_r - m_new)
    C_new = cd * C + _dot_tn((v * w_c).astype(BF16), kb)
    n_new = cd * n_row + _dot(w_r.astype(BF16), kb)
    C_s[...] = C_new
    n_s[...] = n_new
    m_s[...] = m_new

    hn = hval * lax.rsqrt(jnp.mean(hval * hval, axis=-1, keepdims=True) + NORM_EPS) * mn_ref[...]
    out_ref[...] = hn * jax.nn.sigmoid(mo_ref[...]) * _silu(mz_ref[...])

    @pl.when(c == nc - 1)
    def _():
        C_out[...] = C_new
        n_out[...] = n_new
        m_out[...] = m_new


def mlstm_prompt(proj, B, S, gi, gf, conv_w, conv_b, b_i, b_f, m_norm):
    H, Dh, L = M_HEADS, M_HEAD_DIM, M_CHUNK
    nc = S // L
    M = B * S
    g2 = jnp.stack([gi, gf], axis=-1).reshape(B, nc, L, H, 2).transpose(0, 3, 1, 4, 2)
    gb = jnp.stack([b_i, b_f], axis=-1).reshape(H, 2, 1)
    cw = jnp.concatenate([conv_w, conv_b.reshape(1, -1)], axis=0)
    pblk = lambda off: pl.BlockSpec((L, Dh), lambda b, h, c: (b * nc + c, off + h))
    hb = M_WIDTH // Dh
    outs = pl.pallas_call(
        _mlstm_chunk_kernel,
        grid=(B, H, nc),
        in_specs=[pblk(0), pblk(hb), pblk(2 * hb), pblk(3 * hb), pblk(4 * hb),
                  pl.BlockSpec((None, None, None, 2, L), lambda b, h, c: (b, h, c, 0, 0)),
                  pl.BlockSpec((None, 2, 1), lambda b, h, c: (h, 0, 0)),
                  pl.BlockSpec((CONV_WIDTH + 1, Dh), lambda b, h, c: (0, h)),
                  pl.BlockSpec((CONV_WIDTH + 1, Dh), lambda b, h, c: (0, hb + h)),
                  pl.BlockSpec((1, Dh), lambda b, h, c: (0, h))],
        out_specs=[pl.BlockSpec((L, Dh), lambda b, h, c: (b * nc + c, h)),
                   pl.BlockSpec((None, None, Dh, Dh), lambda b, h, c: (b, h, 0, 0)),
                   pl.BlockSpec((None, None, 1, Dh), lambda b, h, c: (b, h, 0, 0)),
                   pl.BlockSpec((None, None, 1, 1), lambda b, h, c: (b, h, 0, 0))],
        out_shape=[jax.ShapeDtypeStruct((M, M_WIDTH), F32),
                   jax.ShapeDtypeStruct((B, H, Dh, Dh), F32),
                   jax.ShapeDtypeStruct((B, H, 1, Dh), F32),
                   jax.ShapeDtypeStruct((B, H, 1, 1), F32)],
        scratch_shapes=[pltpu.VMEM((Dh, Dh), F32), pltpu.VMEM((1, Dh), F32), pltpu.VMEM((1, 1), F32),
                        pltpu.VMEM((L + 8, Dh), F32), pltpu.VMEM((L + 8, Dh), F32)],
        compiler_params=_cp(("parallel", "parallel", "arbitrary")),
        name="mlstm_prompt",
    )(proj, proj, proj, proj, proj, g2, gb, cw, cw, m_norm.reshape(1, -1))
    return outs


def _mlstm_step_kernel(x_ref, conv0_ref, cw_ref, cb_ref, v_ref, gi_ref, gf_ref, bi_ref, bf_ref,
                       mo_ref, mz_ref, mn_ref, C_ref, n_ref, m_ref,
                       out_ref, conv_out, C_out, n_out, m_out):
    H, Dh = v_ref.shape
    x = x_ref[...]
    qk = cb_ref[...]
    for j in range(CONV_WIDTH - 1):
        qk = qk + conv0_ref[j] * cw_ref[j]
    qk = qk + x * cw_ref[CONV_WIDTH - 1]
    for j in range(CONV_WIDTH - 2):
        conv_out[j] = conv0_ref[j + 1]
    conv_out[CONV_WIDTH - 2] = x
    qk = _silu(qk)
    q = qk[0:H]
    k = qk[H:2 * H] * (Dh ** -0.5)
    v = v_ref[...]
    li = gi_ref[...] + bi_ref[...]
    lf = _log_sigmoid(gf_ref[...] + bf_ref[...])
    m0 = m_ref[...]
    inter = lf + m0
    m_t = jnp.maximum(inter, li)
    w_intra = jnp.exp(li - m_t)
    w_inter = jnp.exp(inter - m_t)
    s = jnp.sum(q * k, axis=-1, keepdims=True) * w_intra
    n0 = n_ref[...]
    hrow = lax.broadcasted_iota(jnp.int32, (H, Dh), 0)
    cq = jnp.zeros((H, Dh), F32)
    vT = _dot_nt(_eye(Dh), v * w_intra, HI)
    for h in range(H):
        Ch = C_ref[h]
        cq = jnp.where(hrow == h, _dot_nt(q, Ch, HI), cq)
        C_out[h] = w_inter[h:h + 1, :] * Ch + vT[:, h:h + 1] * k[h:h + 1, :]
    num = s * v + w_inter * cq
    den = s + w_inter * jnp.sum(n0 * q, axis=-1, keepdims=True)
    hval = num / jnp.maximum(jnp.abs(den), jnp.exp(-m_t))
    n_out[...] = w_inter * n0 + w_intra * k
    m_out[...] = m_t
    hn = hval * lax.rsqrt(jnp.mean(hval * hval, axis=-1, keepdims=True) + NORM_EPS) * mn_ref[...]
    out_ref[...] = hn * jax.nn.sigmoid(mo_ref[...]) * _silu(mz_ref[...])


def mlstm_step(x16, conv0, conv_w, conv_b, v, gi, gf, b_i, b_f, mo, mz, m_norm, C0, n0, m0):
    B = x16.shape[0]
    H, Dh = M_HEADS, M_HEAD_DIM
    W = CONV_WIDTH
    per_b = lambda *shape: pl.BlockSpec((None,) + shape, lambda b: (b,) + (0,) * len(shape))
    const = lambda *shape: pl.BlockSpec(shape, lambda b: (0,) * len(shape))
    return pl.pallas_call(
        _mlstm_step_kernel,
        grid=(B,),
        in_specs=[per_b(2 * H, Dh), per_b(W - 1, 2 * H, Dh), const(W, 2 * H, Dh), const(2 * H, Dh),
                  per_b(H, Dh), per_b(H, 1), per_b(H, 1), const(H, 1), const(H, 1),
                  per_b(H, Dh), per_b(H, Dh), const(H, Dh),
                  per_b(H, Dh, Dh), per_b(H, Dh), per_b(H, 1)],
        out_specs=[per_b(H, Dh), per_b(W - 1, 2 * H, Dh), per_b(H, Dh, Dh), per_b(H, Dh), per_b(H, 1)],
        out_shape=[jax.ShapeDtypeStruct((B, H, Dh), F32),
                   jax.ShapeDtypeStruct((B, W - 1, 2 * H, Dh), F32),
                   jax.ShapeDtypeStruct((B, H, Dh, Dh), F32),
                   jax.ShapeDtypeStruct((B, H, Dh), F32),
                   jax.ShapeDtypeStruct((B, H, 1), F32)],
        compiler_params=_cp(("parallel",)),
        name="mlstm_step",
    )(x16, conv0, conv_w, conv_b, v, gi, gf, b_i, b_f, mo, mz, m_norm, C0, n0, m0)


def _rwkv_chunk_kernel(rr_ref, rk_ref, rv_ref, rz_ref, rw_ref, ra_ref, mu4_ref, mu2_ref, par_ref,
                       w2_ref, a2_ref, out_ref, S_out, S_s, carry_s):
    c = pl.program_id(2)
    nc = pl.num_programs(2)
    TB = rr_ref.shape[0]
    W = rr_ref.shape[1]
    N = R_HEAD_DIM
    L = R_CHUNK

    @pl.when(c == 0)
    def _():
        S_s[...] = jnp.zeros(S_s.shape, F32)
        carry_s[...] = jnp.zeros(carry_s.shape, F32)

    row = lax.broadcasted_iota(jnp.int32, (TB, W), 0)
    lo = lax.broadcasted_iota(jnp.int32, (TB, W), 1) < N

    def shift(x, idx, mu):
        prev = jnp.where(row == 0, carry_s[idx:idx + 1, :], pltpu.roll(x, 1, 0))
        carry_s[idx:idx + 1, :] = x[TB - 1:TB, :]
        return x + mu * (prev - x)

    r = shift(rr_ref[...], 0, mu4_ref[0:1, :])
    k = shift(rk_ref[...], 1, mu4_ref[1:2, :])
    v = shift(rv_ref[...], 2, mu4_ref[2:3, :])
    z = shift(rz_ref[...], 3, mu4_ref[3:4, :])
    xw = shift(rw_ref[...], 4, mu2_ref[0:1, :])
    xa = shift(ra_ref[...], 5, mu2_ref[1:2, :])
    w0, a0 = par_ref[0:1, :], par_ref[1:2, :]
    k_k, k_a, r_k = par_ref[2:3, :], par_ref[3:4, :], par_ref[4:5, :]
    gn_w, gn_b = par_ref[5:6, :], par_ref[6:7, :]

    def seg_sum(x):
        s0 = jnp.sum(jnp.where(lo, x, 0.0), axis=1, keepdims=True)
        s1 = jnp.sum(jnp.where(lo, 0.0, x), axis=1, keepdims=True)
        return jnp.where(lo, s0, s1)

    w_log = -_softplus(-(w0 + _dot3(jnp.tanh(xw), w2_ref[...]))) - 0.5
    logw = -jnp.exp(w_log)
    a = jax.nn.sigmoid(a0 + _dot3(xa, a2_ref[...]))
    kk = k * k_k
    kk = kk * lax.rsqrt(jnp.maximum(seg_sum(kk * kk), 1e-24))
    k2 = k * (1.0 + (a - 1.0) * k_a)
    alpha = -kk
    beta = kk * a

    tr = lax.broadcasted_iota(jnp.int32, (TB, TB), 0)
    tc = lax.broadcasted_iota(jnp.int32, (TB, TB), 1)
    in_chunk = jnp.logical_and(tc <= tr, tc >= tr - jnp.bitwise_and(tr, L - 1))
    l_hi = logw.astype(BF16)
    l_r = logw - l_hi.astype(F32)
    l_mid = l_r.astype(BF16)
    l_lo = (l_r - l_mid.astype(F32)).astype(BF16)
    b3 = _dot(jnp.where(in_chunk, 1.0, 0.0).astype(BF16), jnp.concatenate([l_hi, l_mid, l_lo], axis=1))
    b = (b3[:, 0:W] + b3[:, W:2 * W]) + b3[:, 2 * W:3 * W]

    lo_c = lax.broadcasted_iota(jnp.int32, (L, W), 1) < N

    def stack(x):
        return jnp.concatenate([jnp.where(lo_c, x, 0.0), jnp.where(lo_c, 0.0, x)], axis=0)

    ri = lax.broadcasted_iota(jnp.int32, (2 * L, 2 * L), 0)
    ci = lax.broadcasted_iota(jnp.int32, (2 * L, 2 * L), 1)
    same = jnp.where(ri >= L, 1, 0) == jnp.where(ci >= L, 1, 0)
    strict = jnp.logical_and(same, ci < ri)
    incl = jnp.logical_and(same, ci <= ri)
    eye = jnp.where(ri == ci, 1.0, 0.0)

    chunks = range(TB // L)
    sls = [slice(ch * L, (ch + 1) * L) for ch in chunks]
    bLs = [b[sl][L - 1:L, :] for sl in sls]
    a_s = [stack(alpha[sl] * jnp.exp(b[sl] - logw[sl])) for sl in sls]
    r_s = [stack(r[sl] * jnp.exp(b[sl])) for sl in sls]
    Ys_ = [jnp.concatenate([stack(beta[sl] * jnp.exp(-b[sl])), stack(k2[sl] * jnp.exp(-b[sl]))], axis=0) for sl in sls]
    bh_s = [stack(beta[sl] * jnp.exp(bL - b[sl])) for sl, bL in zip(sls, bLs)]
    kh_s = [stack(k2[sl] * jnp.exp(bL - b[sl])) for sl, bL in zip(sls, bLs)]
    v_s = [stack(v[sl]) for sl in sls]
    Gs = [_dot3_nt(a_s[i], Ys_[i]) for i in chunks]
    Go = [_dot_nt(r_s[i].astype(BF16), Ys_[i].astype(BF16)) for i in chunks]
    A = [jnp.where(strict, Gs[i][:, 0:2 * L], 0.0) for i in chunks]
    Bm = [jnp.where(strict, Gs[i][:, 2 * L:4 * L], 0.0) for i in chunks]
    P = [jnp.where(incl, Go[i][:, 0:2 * L], 0.0).astype(BF16) for i in chunks]
    Q = [jnp.where(incl, Go[i][:, 2 * L:4 * L], 0.0).astype(BF16) for i in chunks]
    BV = [_dot3(Bm[i], v_s[i]) for i in chunks]
    QV = [_dot(Q[i], v_s[i].astype(BF16)) for i in chunks]
    T = [eye + A[i] for i in chunks]
    Ap = [_dot3(A[i], A[i]) for i in chunks]
    for _ in range(4):
        R = [_dot(_lhs3(Ap[i], 1), _rhs3(jnp.concatenate([Ap[i], T[i]], axis=1), 0)) for i in chunks]
        T = [T[i] + R[i][:, 2 * L:4 * L] for i in chunks]
        Ap = [R[i][:, 0:2 * L] for i in chunks]
    T = [T[i] + _dot3(Ap[i], T[i]) for i in chunks]
    TXB = [_dot3(T[i], jnp.concatenate([a_s[i], BV[i]], axis=1)) for i in chunks]
    PTX = [_dot(P[i], TXB[i].astype(BF16)) for i in chunks]
    MN = [_dot3_tn(TXB[i], bh_s[i]) for i in chunks]
    VK = [_dot3_tn(v_s[i], kh_s[i]) for i in chunks]
    pre = [((r_s[i] + PTX[i][:, 0:W]).astype(BF16), QV[i] + PTX[i][:, W:2 * W],
            MN[i][0:W], MN[i][W:2 * W] + VK[i], jnp.exp(bLs[i])) for i in chunks]

    S = S_s[...]
    ys = []
    for Rp, Y0, Mc, Nc, gL in pre:
        Ys = _dot_nt(Rp, S.astype(BF16)) + Y0
        ys.append(Ys[0:L] + Ys[L:2 * L])
        S = (S * gL + _dot3(S, Mc)) + Nc
    S_new = S
    S_s[...] = S_new
    y = jnp.concatenate(ys, axis=0)

    mu_y = seg_sum(y) * (1.0 / N)
    d = y - mu_y
    var = seg_sum(d * d) * (1.0 / N)
    yn = d * lax.rsqrt(var + RWKV_GN_EPS) * gn_w + gn_b
    yn = yn + seg_sum(r * k2 * r_k) * v
    out_ref[...] = yn * _silu(z)

    @pl.when(c == nc - 1)
    def _():
        S_out[...] = S_new


def rwkv_prompt(proj, B, S, mu4, mu2, par, w2p, a2p, r_off):
    L = R_STEP_TOKENS
    W = 2 * R_HEAD_DIM
    nc = S // L
    M = B * S
    npair = mu4.shape[1] // W
    blk = lambda off: pl.BlockSpec((L, W), lambda b, p, c: (b * nc + c, off + p))
    cblk = lambda off: pl.BlockSpec((L, W), lambda b, p, c: (b * nc + c, off))
    return pl.pallas_call(
        _rwkv_chunk_kernel,
        grid=(B, npair, nc),
        in_specs=[blk(r_off), blk(r_off + npair), blk(r_off + 2 * npair), blk(r_off + 3 * npair),
                  cblk(r_off + 4 * npair), cblk(r_off + 4 * npair + 1),
                  pl.BlockSpec((8, W), lambda b, p, c: (0, p)),
                  pl.BlockSpec((8, W), lambda b, p, c: (0, 0)),
                  pl.BlockSpec((8, W), lambda b, p, c: (0, p)),
                  pl.BlockSpec((W, W), lambda b, p, c: (0, p)),
                  pl.BlockSpec((W, W), lambda b, p, c: (0, p))],
        out_specs=[pl.BlockSpec((L, W), lambda b, p, c: (b * nc + c, p)),
                   pl.BlockSpec((None, None, W, W), lambda b, p, c: (b, p, 0, 0))],
        out_shape=[jax.ShapeDtypeStruct((M, npair * W), F32),
                   jax.ShapeDtypeStruct((B, npair, W, W), F32)],
        scratch_shapes=[pltpu.VMEM((W, W), F32), pltpu.VMEM((8, W), F32)],
        compiler_params=_cp(("parallel", "parallel", "arbitrary")),
        name="rwkv_prompt",
    )(proj, proj, proj, proj, proj, proj, mu4, mu2, par, w2p, a2p)


def _rwkv_prep_kernel(rr_ref, rk_ref, rv_ref, rz_ref, rw_ref, ra_ref, sh4_ref, sh2_ref, mu4_ref, mu2_ref,
                      w0_ref, a0_ref, w2_ref, a2_ref, r_o, k_o, v_o, z_o, w_o, a_o):
    Wd = rr_ref.shape[1]
    sh = lambda x, prev, mu: x + mu * (prev - x)
    r_o[...] = sh(rr_ref[...], sh4_ref[:, 0:Wd], mu4_ref[:, 0:Wd])
    k_o[...] = sh(rk_ref[...], sh4_ref[:, Wd:2 * Wd], mu4_ref[:, Wd:2 * Wd])
    v_o[...] = sh(rv_ref[...], sh4_ref[:, 2 * Wd:3 * Wd], mu4_ref[:, 2 * Wd:3 * Wd])
    z_o[...] = sh(rz_ref[...], sh4_ref[:, 3 * Wd:4 * Wd], mu4_ref[:, 3 * Wd:4 * Wd])
    xw = sh(rw_ref[...], sh2_ref[:, 0:LANES], mu2_ref[:, 0:LANES])
    xa = sh(ra_ref[...], sh2_ref[:, LANES:2 * LANES], mu2_ref[:, LANES:2 * LANES])
    w_log = -_softplus(-(w0_ref[...] + _dot(jnp.tanh(xw), w2_ref[...], HI))) - 0.5
    w_o[...] = jnp.exp(-jnp.exp(w_log))
    a_o[...] = jax.nn.sigmoid(a0_ref[...] + _dot(xa, a2_ref[...], HI))


def rwkv_prep(proj, sh4, sh2, mu4, mu2, w0, a0, w2p, a2p, r_blk, s_blk):
    B = proj.shape[0]
    Wd = w0.shape[1]
    full = lambda a: pl.BlockSpec(a.shape, lambda i: (0,) * a.ndim)
    return pl.pallas_call(
        _rwkv_prep_kernel,
        grid=(1,),
        in_specs=[pl.BlockSpec((B, Wd), lambda i: (0, r_blk)), pl.BlockSpec((B, Wd), lambda i: (0, r_blk + 1)),
                  pl.BlockSpec((B, Wd), lambda i: (0, r_blk + 2)), pl.BlockSpec((B, Wd), lambda i: (0, r_blk + 3)),
                  pl.BlockSpec((B, LANES), lambda i: (0, s_blk)), pl.BlockSpec((B, LANES), lambda i: (0, s_blk + 1)),
                  full(sh4), full(sh2), full(mu4), full(mu2), full(w0), full(a0), full(w2p), full(a2p)],
        out_specs=[pl.BlockSpec((B, Wd), lambda i: (0, 0))] * 6,
        out_shape=[jax.ShapeDtypeStruct((B, Wd), F32)] * 6,
        compiler_params=_cp(("arbitrary",)),
        name="rwkv_prep",
    )(proj, proj, proj, proj, proj, proj, sh4, sh2, mu4, mu2, w0, a0, w2p, a2p)


def _rwkv_step_kernel(r_ref, k_ref, v_ref, z_ref, w_ref, a_ref, par_ref, S_ref, out_ref, S_out):
    H, N = r_ref.shape
    r, k, v, z, w, a = (t[...] for t in (r_ref, k_ref, v_ref, z_ref, w_ref, a_ref))
    k_k, k_a, r_k, gn_w, gn_b = (par_ref[i] for i in range(5))
    kk = k * k_k
    kk = kk * lax.rsqrt(jnp.maximum(jnp.sum(kk * kk, axis=-1, keepdims=True), 1e-24))
    k2 = k * (1.0 + (a - 1.0) * k_a)
    alpha = -kk
    beta = kk * a
    vT = _dot_nt(_eye(N), v, HI)
    lane_h = lax.broadcasted_iota(jnp.int32, (N, H), 1)
    yT = jnp.zeros((N, H), F32)
    for h in range(H):
        S = S_ref[h]
        sa = jnp.sum(S * alpha[h:h + 1, :], axis=-1, keepdims=True)
        Sn = S * w[h:h + 1, :] + sa * beta[h:h + 1, :] + vT[:, h:h + 1] * k2[h:h + 1, :]
        S_out[h] = Sn
        yT = jnp.where(lane_h == h, jnp.sum(Sn * r[h:h + 1, :], axis=-1, keepdims=True), yT)
    y = _dot_nt(_eye(H), yT, HI)
    mu_y = jnp.mean(y, axis=-1, keepdims=True)
    d = y - mu_y
    var = jnp.mean(d * d, axis=-1, keepdims=True)
    yn = d * lax.rsqrt(var + RWKV_GN_EPS) * gn_w + gn_b
    yn = yn + jnp.sum(r * k2 * r_k, axis=-1, keepdims=True) * v
    out_ref[...] = yn * _silu(z)


def rwkv_step(r, k, v, z, w, a, par, S0):
    B, H, N = r.shape
    vec = pl.BlockSpec((None, H, N), lambda b: (b, 0, 0))
    mat = pl.BlockSpec((None, H, N, N), lambda b: (b, 0, 0, 0))
    return pl.pallas_call(
        _rwkv_step_kernel,
        grid=(B,),
        in_specs=[vec] * 6 + [pl.BlockSpec(par.shape, lambda b: (0, 0, 0)), mat],
        out_specs=[vec, mat],
        out_shape=[jax.ShapeDtypeStruct((B, H, N), F32), jax.ShapeDtypeStruct((B, H, N, N), F32)],
        compiler_params=_cp(("parallel",)),
        name="rwkv_step",
    )(r, k, v, z, w, a, par, S0)


def _qk_norm_kernel(q_ref, k_ref, v_ref, f_ref, gq_ref, gk_ref, bf_ref, kn_ref, v_out_ref, lf_ref, *q_refs,
                    mxu_copies):
    D = F_HEAD_DIM
    for h in range(q_ref.shape[1] // D):
        sl = slice(h * D, (h + 1) * D)
        x = q_ref[:, sl]
        qn = x * lax.rsqrt(jnp.mean(x * x, axis=-1, keepdims=True) + NORM_EPS) * gq_ref[...]
        y = k_ref[:, sl]
        kn = y * lax.rsqrt(jnp.mean(y * y, axis=-1, keepdims=True) + NORM_EPS) * gk_ref[...]
        v = v_ref[:, sl]
        kn_ref[:, sl] = kn
        v_out_ref[:, sl] = v
        if mxu_copies:
            qb_ref, kb_ref, vb_ref = q_refs
            qb_ref[:, sl] = (qn * (D ** -0.5 * LOG2E)).astype(BF16)
            kb_ref[:, sl] = kn.astype(BF16)
            vb_ref[:, sl] = v.astype(BF16)
        else:
            q_refs[0][:, sl] = qn
    lf_ref[...] = _log_sigmoid(f_ref[...] + bf_ref[...])


def qk_norm(proj, gq, gk, bf_pad, tm, Wd, f_blk, mxu_copies):
    M = proj.shape[0]
    wide = pl.BlockSpec((tm, Wd), lambda i: (i, 0))
    wide_sds = lambda dt: jax.ShapeDtypeStruct((M, Wd), dt)
    extra = [wide_sds(BF16)] * 3 if mxu_copies else [wide_sds(F32)]
    return pl.pallas_call(
        functools.partial(_qk_norm_kernel, mxu_copies=mxu_copies),
        grid=(M // tm,),
        in_specs=[pl.BlockSpec((tm, Wd), lambda i: (i, 0)), pl.BlockSpec((tm, Wd), lambda i: (i, 1)),
                  pl.BlockSpec((tm, Wd), lambda i: (i, 2)),
                  pl.BlockSpec((tm, LANES), lambda i: (i, f_blk)),
                  pl.BlockSpec((1, F_HEAD_DIM), lambda i: (0, 0)), pl.BlockSpec((1, F_HEAD_DIM), lambda i: (0, 0)),
                  pl.BlockSpec((1, LANES), lambda i: (0, 0))],
        out_specs=[wide, wide, pl.BlockSpec((tm, LANES), lambda i: (i, 0))] + [wide] * len(extra),
        out_shape=[wide_sds(F32), wide_sds(F32), jax.ShapeDtypeStruct((M, LANES), F32)] + extra,
        compiler_params=_cp(("parallel",)),
        name="qk_norm",
    )(proj, proj, proj, proj, gq.reshape(1, -1), gk.reshape(1, -1), bf_pad)


def _cumsum_kernel(x_ref, c_ref):
    H, S = x_ref.shape
    U = (lax.broadcasted_iota(jnp.int32, (LANES, LANES), 0) <= lax.broadcasted_iota(jnp.int32, (LANES, LANES), 1)).astype(F32)
    off = jnp.zeros((H, 1), F32)
    for t in range(S // LANES):
        sl = slice(t * LANES, (t + 1) * LANES)
        w = _dot(x_ref[:, sl], U, HI) + off
        c_ref[:, sl] = w
        off = w[:, LANES - 1:LANES]


def cumsum_lanes(x):
    B, H, S = x.shape
    return pl.pallas_call(
        _cumsum_kernel,
        grid=(B,),
        in_specs=[pl.BlockSpec((None, H, S), lambda b: (b, 0, 0))],
        out_specs=pl.BlockSpec((None, H, S), lambda b: (b, 0, 0)),
        out_shape=jax.ShapeDtypeStruct((B, H, S), F32),
        compiler_params=_cp(("parallel",)),
        name="forget_cumsum",
    )(x)


def _fox_flash_kernel(q_ref, k_ref, v_ref, ck_ref, o_ref, m_s, l_s, acc_s):
    i = pl.program_id(2)
    t, D = q_ref.shape
    q = q_ref[...]
    m_s[...] = jnp.full(m_s.shape, -jnp.inf, F32)
    l_s[...] = jnp.zeros(l_s.shape, F32)
    acc_s[...] = jnp.zeros(acc_s.shape, F32)

    def block(j, masked):
        start = pl.multiple_of(j * t, t)
        s = _dot_nt(q, k_ref[pl.ds(start, t), :]) - ck_ref[j] * LOG2E
        if masked:
            row = lax.broadcasted_iota(jnp.int32, (t, t), 0)
            col = lax.broadcasted_iota(jnp.int32, (t, t), 1)
            s = jnp.where(col <= row, s, -jnp.inf)
        m_prev = m_s[...]
        m_new = jnp.maximum(m_prev, jnp.max(s, axis=1, keepdims=True))
        alpha = jnp.exp2(m_prev - m_new)
        p = jnp.exp2(s - jnp.concatenate([m_new] * (t // LANES), axis=1))
        l_s[...] = alpha * l_s[...] + jnp.sum(p, axis=1, keepdims=True)
        acc_s[...] = alpha * acc_s[...] + _dot(p.astype(BF16), v_ref[pl.ds(start, t), :])
        m_s[...] = m_new

    def body(j, carry):
        block(j, False)
        return carry

    lax.fori_loop(0, i, body, 0)
    block(i, True)
    o_ref[...] = acc_s[...] / l_s[...]


def fox_prompt_attn(qb, kb, vb, ck, B, S, t):
    D = F_HEAD_DIM
    H = qb.shape[1] // D
    nb = S // t
    full = pl.BlockSpec((S, D), lambda b, h, i: (b, h))
    return pl.pallas_call(
        _fox_flash_kernel,
        grid=(B, H, nb),
        in_specs=[pl.BlockSpec((t, D), lambda b, h, i: (b * nb + i, h)), full, full,
                  pl.BlockSpec((None, None, nb, 1, t), lambda b, h, i: (b, h, 0, 0, 0))],
        out_specs=pl.BlockSpec((t, D), lambda b, h, i: (b * nb + i, h)),
        out_shape=jax.ShapeDtypeStruct((B * S, H * D), F32),
        scratch_shapes=[pltpu.VMEM((t, LANES), F32), pltpu.VMEM((t, LANES), F32), pltpu.VMEM((t, D), F32)],
        compiler_params=_cp(("parallel", "parallel", "parallel")),
        name="fox_prompt_attn",
    )(qb, kb, vb, ck)


def _fox_decode_kernel(pt_ref, q_ref, kn_ref, vn_ref, lfn_ref, *refs, npg):
    k_refs, v_refs, lf_refs = refs[0:npg], refs[npg:2 * npg], refs[2 * npg:3 * npg]
    o_ref = refs[3 * npg]
    m_s, l_s, acc_s, off_s, cp_s = refs[3 * npg + 1:]
    p = pl.program_id(1)
    H, D = q_ref.shape
    T = k_refs[0].shape[0]
    pages = range(npg)

    @pl.when(p == 0)
    def _():
        m_s[...] = jnp.full(m_s.shape, -jnp.inf, F32)
        l_s[...] = jnp.zeros(l_s.shape, F32)
        acc_s[...] = jnp.zeros(acc_s.shape, F32)
        off_s[...] = jnp.zeros(off_s.shape, F32)
        cp_s[...] = jnp.zeros(cp_s.shape, F32)

    q = q_ref[...] * (D ** -0.5 * LOG2E)
    hmask = (lax.broadcasted_iota(jnp.int32, (H, LANES), 0) == lax.broadcasted_iota(jnp.int32, (H, LANES), 1)).astype(F32)
    hmask2 = hmask * LOG2E
    trow = lax.broadcasted_iota(jnp.int32, (T, LANES), 0)
    for g in pages:
        cp_s[g, :, 0:H] = lf_refs[g][...]
    cps = [cp_s[g] for g in pages]
    sh = 1
    while sh < T:
        cps = [x + jnp.where(trow >= sh, pltpu.roll(x, sh, 0), 0.0) for x in cps]
        sh *= 2
    for g in pages:
        cp_s[g] = cps[g]
    cbm = [jnp.stack([jnp.broadcast_to(cp_s[g, t:t + 1, :], (H, LANES)) for t in range(T)], axis=0) * hmask2[None]
           for g in pages]
    s = [jnp.sum(k_refs[g][...] * q[None] - cbm[g], axis=-1, keepdims=True) for g in pages]
    tot = [jnp.sum(jnp.broadcast_to(cp_s[g, T - 1:T, :], (H, LANES)) * hmask, axis=-1, keepdims=True) for g in pages]
    mg = [jnp.max(s[g], axis=0) for g in pages]
    pr = [jnp.exp2(s[g] - mg[g][None]) for g in pages]
    lg = [jnp.sum(pr[g], axis=0) for g in pages]
    ag = [jnp.sum(pr[g] * v_refs[g][...], axis=0) for g in pages]
    off = off_s[:, 0:1]
    m_prev = m_s[:, 0:1]
    offs = []
    off_new = off
    for g in pages:
        offs.append(off_new * LOG2E)
        off_new = off_new + tot[g]
    m_new = m_prev
    for g in pages:
        m_new = jnp.maximum(m_new, mg[g] - offs[g])
    alpha = jnp.exp2(m_prev - m_new)
    l_new = alpha * l_s[:, 0:1]
    acc_new = alpha * acc_s[...]
    for g in pages:
        wg = jnp.exp2(mg[g] - offs[g] - m_new)
        l_new = l_new + wg * lg[g]
        acc_new = acc_new + wg * ag[g]
    m_s[...] = jnp.broadcast_to(m_new, m_s.shape)
    l_s[...] = jnp.broadcast_to(l_new, l_s.shape)
    acc_s[...] = acc_new
    off_s[...] = jnp.broadcast_to(off_new, off_s.shape)

    @pl.when(p == pl.num_programs(1) - 1)
    def _():
        cq = (off_new + lfn_ref[:, 0:1]) * LOG2E
        s_self = jnp.sum(q * kn_ref[...], axis=-1, keepdims=True) - cq
        m_f = jnp.maximum(m_new, s_self)
        a2 = jnp.exp2(m_new - m_f)
        ps = jnp.exp2(s_self - m_f)
        o_ref[...] = (a2 * acc_new + ps * vn_ref[...]) / (a2 * l_new + ps)


def fox_decode_attn(q, k_new, v_new, lf_new, cache_k, cache_v, cache_lf, page_table):
    B, H, D = q.shape
    n_pages = page_table.shape[1]
    T = cache_k.shape[1]
    npg = DEC_PAGES_PER_STEP if n_pages % DEC_PAGES_PER_STEP == 0 else 1
    vec = pl.BlockSpec((None, H, D), lambda b, p, pt: (b, 0, 0))
    kv = lambda g: pl.BlockSpec((None, T, H, D), lambda b, p, pt: (pt[b, p * npg + g], 0, 0, 0))
    lfs = lambda g: pl.BlockSpec((None, T, H), lambda b, p, pt: (pt[b, p * npg + g], 0, 0))
    grid_spec = pltpu.PrefetchScalarGridSpec(
        num_scalar_prefetch=1,
        grid=(B, n_pages // npg),
        in_specs=[vec, vec, vec, vec] + [kv(g) for g in range(npg)] * 2 + [lfs(g) for g in range(npg)],
        out_specs=pl.BlockSpec((None, H, D), lambda b, p, pt: (b, 0, 0)),
        scratch_shapes=[pltpu.VMEM((H, D), F32)] * 4 + [pltpu.VMEM((npg, T, LANES), F32)],
    )
    return pl.pallas_call(
        functools.partial(_fox_decode_kernel, npg=npg),
        grid_spec=grid_spec,
        out_shape=jax.ShapeDtypeStruct((B, H, D), F32),
        compiler_params=_cp(("parallel", "arbitrary")),
        name="fox_decode_attn",
    )(page_table, q, k_new, v_new, lf_new, *([cache_k] * npg), *([cache_v] * npg), *([cache_lf] * npg))


def _pad_cols(a, width):
    return jnp.pad(a, [(0, 0)] * (a.ndim - 1) + [(0, width - a.shape[-1])])


def _split_mix(a):
    W, H, D = M_WIDTH, M_HEADS, a.shape[-1]
    o = 0
    parts = {}
    for name, n in (("mqk", 2 * W), ("mv", W), ("mi", H), ("mf", H), ("mo", W), ("mz", W)):
        parts[name] = a[..., o:o + n]
        o += n
    Wr = (D - o - 2 * R_LORA) // 4
    for name, n in (("rr", Wr), ("rk", Wr), ("rv", Wr), ("rw", R_LORA), ("ra", R_LORA), ("rz", Wr)):
        parts[name] = a[..., o:o + n]
        o += n
    return parts


def _small_mix(p):
    z = jnp.zeros(p["rw"].shape[:-1] + (LANES - R_LORA,), p["rw"].dtype)
    return _pad_cols(jnp.concatenate([p["rw"], z, p["ra"], z, p["mi"], p["mf"]], axis=-1), SMALL_W)


def _arrange_mix(a):
    p = _split_mix(a)
    return jnp.concatenate([p["mqk"], p["mv"], p["mo"], p["mz"], p["rr"], p["rk"], p["rv"], p["rz"], _small_mix(p)],
                           axis=-1)


def _trunk(x, p_in, state, attn, Wt, tm):
    B, T, D = x.shape
    M = B * T
    x2 = x.reshape(M, D)
    tn = 512
    tm_in = 2 * tm if M % (2 * tm) == 0 else tm
    Wr = D
    mixw = Wt["mix_w"]
    proj = norm_matmul(x2, Wt["norm_w"][0], mixw, tm_in, tn)
    W = M_WIDTH
    small0 = 3 * W + 2 * W + 4 * Wr
    gi = proj[:, small0 + 2 * LANES: small0 + 2 * LANES + M_HEADS]
    gf = proj[:, small0 + 2 * LANES + M_HEADS: small0 + 2 * LANES + 2 * M_HEADS]
    r_col0 = 5 * W

    if state is None:
        bm, C, n, m = mlstm_prompt(proj, B, T, gi.reshape(B, T, M_HEADS), gf.reshape(B, T, M_HEADS),
                                   Wt["m_conv_w"], Wt["m_conv_b"], Wt["m_b_i"], Wt["m_b_f"], Wt["m_norm"])
        conv_new = proj[:, 0:2 * W].reshape(B, T, 2 * W)[:, T - (CONV_WIDTH - 1):]
        C = C
        n = n.reshape(B, M_HEADS, M_HEAD_DIM)
        m = m.reshape(B, M_HEADS)
        br, Sbd = rwkv_prompt(proj, B, T, Wt["mu4"], Wt["mu2"], Wt["r_par"], Wt["w2p"], Wt["a2p"],
                              r_col0 // LANES)
        N = R_HEAD_DIM
        S_new = jnp.stack([Sbd[:, :, 0:N, 0:N], Sbd[:, :, N:2 * N, N:2 * N]], axis=2).reshape(B, Wr // N, N, N)
    else:
        conv0, C0, n0, m0, S0, sh0 = state
        H, Dh = M_HEADS, M_HEAD_DIM
        bm, conv_new, C, n, m = mlstm_step(
            proj[:, 0:2 * W].reshape(B, 2 * H, Dh), conv0.reshape(B, CONV_WIDTH - 1, 2 * H, Dh),
            Wt["m_conv_w"].reshape(CONV_WIDTH, 2 * H, Dh), Wt["m_conv_b"].reshape(2 * H, Dh),
            proj[:, 2 * W:3 * W].reshape(B, H, Dh), gi.reshape(B, H, 1), gf.reshape(B, H, 1),
            Wt["m_b_i"].reshape(H, 1), Wt["m_b_f"].reshape(H, 1),
            proj[:, 3 * W:4 * W].reshape(B, H, Dh), proj[:, 4 * W:5 * W].reshape(B, H, Dh),
            Wt["m_norm"].reshape(H, Dh), C0, n0, m0.reshape(B, H, 1))
        bm = bm.reshape(M, W)
        conv_new = conv_new.reshape(B, CONV_WIDTH - 1, 2 * W)
        m = m.reshape(B, H)
        shp = _split_mix(jnp.concatenate([jnp.zeros((B, 5 * W + 2 * M_HEADS), F32), sh0], axis=-1))
        sh4 = jnp.concatenate([shp["rr"], shp["rk"], shp["rv"], shp["rz"]], axis=-1)
        sh2 = _small_mix(shp)[:, 0:2 * LANES]
        N = R_HEAD_DIM
        Hr = Wr // N
        r_, k_, v_, z_, w_, a_ = rwkv_prep(proj, sh4, sh2, Wt["mu4_flat"],
                                           Wt["mu2_flat"], Wt["r_par"][0:1], Wt["r_par"][1:2], Wt["w2p"], Wt["a2p"],
                                           r_col0 // Wr, small0 // LANES)
        hs = lambda t: t.reshape(B, Hr, N)
        par5 = Wt["r_par"][2:7].reshape(5, Hr, N)
        br, S_new = rwkv_step(hs(r_), hs(k_), hs(v_), hs(z_), hs(w_), hs(a_), par5, S0)
        br = br.reshape(M, Wr)
    last = proj.reshape(B, T, -1)[:, T - 1]
    sh_new = jnp.concatenate([last[:, r_col0:r_col0 + 3 * Wr], last[:, small0:small0 + R_LORA],
                              last[:, small0 + LANES:small0 + LANES + R_LORA],
                              last[:, r_col0 + 3 * Wr:r_col0 + 4 * Wr]], axis=-1)

    h1 = out_mix(bm, br, Wt["out_mix_w"], x2, min(tm, 128))
    tm_w = min(tm, 256)
    h1 = ple_add(h1, Wt["ple_norm"][0], Wt["ple_gate_w"][0], p_in[0].reshape(M, -1), Wt["ple_proj"][0], tm_w)

    proj1 = norm_matmul(h1, Wt["norm_w"][1], Wt["fox_w"], tm_in, tn)
    kn, vv, lf, *qs = qk_norm(proj1, Wt["f_q_norm"], Wt["f_k_norm"], Wt["f_b_f_pad"], min(tm, 256), D,
                              4 * D // LANES, state is None)
    Hf = D // F_HEAD_DIM
    lf = lf[:, 0:Hf]
    o = attn(qs, kn, vv, lf, B, T)
    h2 = out_fox(o, proj1, 3, Wt["out_fox_w"], h1, tm_w)
    y = ple_add(h2, Wt["ple_norm"][1], Wt["ple_gate_w"][1], p_in[1].reshape(M, -1), Wt["ple_proj"][1], tm_w,
                final_g=Wt["final_norm"]).reshape(B, T, D)

    k_rows = kn.reshape(1, B, T, Hf, F_HEAD_DIM)
    v_rows = vv.reshape(1, B, T, Hf, F_HEAD_DIM)
    lf_rows = lf.reshape(1, B, T, Hf)
    mix_state = tuple(t[None] for t in (conv_new, C, n, m, S_new, sh_new))
    return y, (k_rows, v_rows, lf_rows), mix_state


def kernel(x_prompt, x_sample, cache_k, cache_v, cache_lf, state_mlstm_conv, state_mlstm_C, state_mlstm_n,
           state_mlstm_m, state_rwkv_S, state_rwkv_shift, page_table, p_prompt, p_sample,
           norm_w, final_norm, w_in_mix, w_out_mix, m_conv_w, m_conv_b, m_b_i, m_b_f, m_norm,
           r_mu, r_w0, r_w2, r_a0, r_a2, r_k_k, r_k_a, r_r_k, r_gn_w, r_gn_b,
           w_in_fox, w_out_fox, f_b_f, f_q_norm, f_k_norm, ple_proj, ple_gate_w, ple_norm):
    D = x_prompt.shape[-1]
    Wr = r_w0.shape[-1]
    Hf = D // F_HEAD_DIM
    mix_w = _arrange_mix(w_in_mix[0]).astype(BF16)
    fw = w_in_fox[0]
    fox_w = jnp.concatenate([fw[:, 0:3 * D], fw[:, 3 * D + Hf:4 * D + Hf], _pad_cols(fw[:, 3 * D:3 * D + Hf], SMALL_W)],
                            axis=-1).astype(BF16)
    mu_full = jnp.concatenate([jnp.zeros((5 * M_WIDTH + 2 * M_HEADS,), F32), r_mu[0]])
    mup = _split_mix(mu_full)
    mu4_flat = jnp.concatenate([mup["rr"], mup["rk"], mup["rv"], mup["rz"]])[None]
    mu2_flat = _small_mix(mup)[None, 0:2 * LANES]
    mu4 = _pad_cols(jnp.stack([mup["rr"], mup["rk"], mup["rv"], mup["rz"]]).T, 8).T
    mu2 = _pad_cols(mu2_flat.reshape(2, LANES).T, 8).T
    r_par = jnp.stack([r_w0[0], r_a0[0], r_k_k[0], r_k_a[0], r_r_k[0], r_gn_w[0], r_gn_b[0], jnp.zeros((Wr,), F32)])
    pad_rows = lambda a: jnp.pad(a, ((0, LANES - a.shape[0]), (0, 0)))
    Wt = dict(mix_w=mix_w, fox_w=fox_w, norm_w=norm_w, final_norm=final_norm,
              out_mix_w=w_out_mix[0].astype(BF16), out_fox_w=w_out_fox[0].astype(BF16),
              m_conv_w=m_conv_w[0], m_conv_b=m_conv_b[0], m_b_i=m_b_i[0], m_b_f=m_b_f[0], m_norm=m_norm[0],
              mu4=mu4, mu2=mu2, mu4_flat=mu4_flat, mu2_flat=mu2_flat, r_par=r_par,
              w2p=pad_rows(r_w2[0]), a2p=pad_rows(r_a2[0]),
              f_q_norm=f_q_norm[0], f_k_norm=f_k_norm[0], f_b_f_pad=_pad_cols(f_b_f[0][None], LANES),
              ple_proj=ple_proj.astype(BF16), ple_gate_w=ple_gate_w.astype(BF16), ple_norm=ple_norm)

    def attn_prompt(qs, kn, vv, lf, B, T):
        lfT = lf.reshape(B, T, Hf).transpose(0, 2, 1)
        c = cumsum_lanes(lfT)
        t = FOX_BLOCK if T % FOX_BLOCK == 0 else T
        return fox_prompt_attn(*qs, c.reshape(B, Hf, T // t, 1, t), B, T, t)

    def attn_sample(qs, kn, vv, lf, B, T):
        r3 = lambda t: t.reshape(B, Hf, F_HEAD_DIM)
        lfb = jnp.broadcast_to(lf.reshape(B, Hf, 1), (B, Hf, F_HEAD_DIM))
        o = fox_decode_attn(r3(qs[0]), r3(kn), r3(vv), lfb, cache_k[0], cache_v[0], cache_lf[0], page_table)
        return o.reshape(B, D)

    Bp, Tp, _ = x_prompt.shape
    y_p, (k_p, v_p, lf_p), mix_p = _trunk(x_prompt, p_prompt.reshape(p_prompt.shape[0], Bp * Tp, -1), None,
                                          attn_prompt, Wt, 512)
    Bs, Ts, _ = x_sample.shape
    state = (state_mlstm_conv[0], state_mlstm_C[0], state_mlstm_n[0], state_mlstm_m[0], state_rwkv_S[0],
             state_rwkv_shift[0])
    y_s, (k_s, v_s, lf_s), mix_s = _trunk(x_sample, p_sample.reshape(p_sample.shape[0], Bs * Ts, -1), state,
                                          attn_sample, Wt, Bs * Ts)
    return (y_p, y_s, k_p, v_p, lf_p) + mix_p + (k_s, v_s, lf_s) + mix_s
```

```python
import functools

import jax
import jax.numpy as jnp
from jax import lax
from jax.experimental import pallas as pl
from jax.experimental.pallas import tpu as pltpu

F32 = jnp.float32
BF16 = jnp.bfloat16
HI = lax.Precision.HIGHEST

NORM_EPS = 1e-6
RWKV_GN_EPS = 64e-5
M_HEADS = 8
M_HEAD_DIM = 256
M_WIDTH = M_HEADS * M_HEAD_DIM
CONV_WIDTH = 4
M_CHUNK = 128
M_STEP_HEADS = 4
R_HEAD_DIM = 64
R_LORA = 64
R_CHUNK = 64
R_STEP_TOKENS = 512
F_HEAD_DIM = 128
DEC_PAGES_PER_STEP = 8
LOG2E = 1.4426950408889634
FOX_BLOCK = 512
LANES = 128
SMALL_W = 512
VMEM_LIMIT = 48 * 1024 * 1024


def _cp(sem, vmem=VMEM_LIMIT):
    return pltpu.CompilerParams(dimension_semantics=sem, vmem_limit_bytes=vmem)


def _dot(a, b, prec=None):
    return jnp.dot(a, b, preferred_element_type=F32, precision=prec)


def _dot_nt(a, b, prec=None):
    return lax.dot_general(a, b, (((1,), (1,)), ((), ())), preferred_element_type=F32, precision=prec)


def _dot_tn(a, b, prec=None):
    return lax.dot_general(a, b, (((0,), (0,)), ((), ())), preferred_element_type=F32, precision=prec)


def _split_bf16(x):
    hi = x.astype(BF16)
    return hi, (x - hi.astype(F32)).astype(BF16)


def _lhs3(x, axis):
    hi, lo = _split_bf16(x)
    return jnp.concatenate([hi, hi, lo], axis=axis)


def _rhs3(x, axis):
    hi, lo = _split_bf16(x)
    return jnp.concatenate([hi, lo, hi], axis=axis)


def _dot3(a, b):
    return _dot(_lhs3(a, 1), _rhs3(b, 0))


def _dot3_nt(a, b):
    return _dot_nt(_lhs3(a, 1), _rhs3(b, 1))


def _dot3_tn(a, b):
    return _dot_tn(_lhs3(a, 0), _rhs3(b, 0))


def _softplus(y):
    return jnp.maximum(y, 0.0) + jnp.log1p(jnp.exp(-jnp.abs(y)))


def _log_sigmoid(x):
    return -_softplus(-x)


def _silu(x):
    return x * jax.nn.sigmoid(x)


def _eye(n):
    return (lax.broadcasted_iota(jnp.int32, (n, n), 0) == lax.broadcasted_iota(jnp.int32, (n, n), 1)).astype(F32)


def _rms_bf16(x, g):
    return (x * lax.rsqrt(jnp.mean(x * x, axis=-1, keepdims=True) + NORM_EPS) * g).astype(BF16)


def _norm_cast_kernel(x_ref, g_ref, o_ref):
    o_ref[...] = _rms_bf16(x_ref[...], g_ref[...])


def norm_cast(x, g, tm):
    M, K = x.shape
    return pl.pallas_call(
        _norm_cast_kernel,
        grid=(M // tm,),
        in_specs=[pl.BlockSpec((tm, K), lambda i: (i, 0)), pl.BlockSpec((1, K), lambda i: (0, 0))],
        out_specs=pl.BlockSpec((tm, K), lambda i: (i, 0)),
        out_shape=jax.ShapeDtypeStruct((M, K), BF16),
        compiler_params=_cp(("parallel",)),
        name="norm_cast",
    )(x, g.reshape(1, K))


def _mm_kernel(x_ref, w_ref, o_ref):
    o_ref[...] = _dot(x_ref[...], w_ref[...])


def matmul_bf16(x, w, tm, tn):
    M, K = x.shape
    N = w.shape[1]
    return pl.pallas_call(
        _mm_kernel,
        grid=(M // tm, N // tn),
        in_specs=[pl.BlockSpec((tm, K), lambda i, j: (i, 0)),
                  pl.BlockSpec((K, tn), lambda i, j: (0, j))],
        out_specs=pl.BlockSpec((tm, tn), lambda i, j: (i, j)),
        out_shape=jax.ShapeDtypeStruct((M, N), F32),
        compiler_params=_cp(("parallel", "parallel")),
        name="in_proj",
    )(x, w)


def _out_mix_kernel(a_ref, b_ref, wa_ref, wb_ref, res_ref, o_ref):
    o_ref[...] = res_ref[...] + (_dot(a_ref[...].astype(BF16), wa_ref[...]) + _dot(b_ref[...].astype(BF16), wb_ref[...]))


def out_mix(a, b, w, res, tm):
    M, K = a.shape
    N = w.shape[1]
    rows = lambda n: pl.BlockSpec((tm, n), lambda i: (i, 0))
    return pl.pallas_call(
        _out_mix_kernel,
        grid=(M // tm,),
        in_specs=[rows(K), rows(K),
                  pl.BlockSpec((K, N), lambda i: (0, 0)),
                  pl.BlockSpec((K, N), lambda i: (1, 0)),
                  rows(N)],
        out_specs=rows(N),
        out_shape=jax.ShapeDtypeStruct((M, N), F32),
        compiler_params=_cp(("parallel",)),
        name="out_mix",
    )(a, b, w, w, res)


def _out_fox_kernel(o_in_ref, z_ref, w_ref, res_ref, o_ref):
    g = (o_in_ref[...] * _silu(z_ref[...])).astype(BF16)
    o_ref[...] = res_ref[...] + _dot(g, w_ref[...])


def out_fox(o, proj, z_blk, w, res, tm):
    M, K = o.shape
    N = w.shape[1]
    return pl.pallas_call(
        _out_fox_kernel,
        grid=(M // tm,),
        in_specs=[pl.BlockSpec((tm, K), lambda i: (i, 0)),
                  pl.BlockSpec((tm, K), lambda i: (i, z_blk)),
                  pl.BlockSpec((K, N), lambda i: (0, 0)),
                  pl.BlockSpec((tm, N), lambda i: (i, 0))],
        out_specs=pl.BlockSpec((tm, N), lambda i: (i, 0)),
        out_shape=jax.ShapeDtypeStruct((M, N), F32),
        compiler_params=_cp(("parallel",)),
        name="out_fox",
    )(o, proj, w, res)


def _ple_kernel(h_ref, g_ref, wg_ref, p_ref, pw_ref, ng_ref, *outs, final):
    x = h_ref[...]
    gate = jax.nn.sigmoid(_dot(_rms_bf16(x, g_ref[...]), wg_ref[...]))
    e = _dot(p_ref[...].astype(BF16), pw_ref[...])
    out = x + gate * e
    if final:
        (y_ref,) = outs
        y_ref[...] = out * lax.rsqrt(jnp.mean(out * out, axis=-1, keepdims=True) + NORM_EPS) * ng_ref[...]
    else:
        o_ref, xn_ref = outs
        o_ref[...] = out
        xn_ref[...] = _rms_bf16(out, ng_ref[...])


def ple_add(h, g, wg, p, pw, next_g, tm, final):
    M, K = h.shape
    N = wg.shape[1]
    P = p.shape[1]
    rows = pl.BlockSpec((tm, N), lambda i: (i, 0))
    vec = pl.BlockSpec((1, K), lambda i: (0, 0))
    f32_rows = jax.ShapeDtypeStruct((M, N), F32)
    return pl.pallas_call(
        functools.partial(_ple_kernel, final=final),
        grid=(M // tm,),
        in_specs=[pl.BlockSpec((tm, K), lambda i: (i, 0)), vec,
                  pl.BlockSpec((K, N), lambda i: (0, 0)),
                  pl.BlockSpec((tm, P), lambda i: (i, 0)),
                  pl.BlockSpec((P, N), lambda i: (0, 0)), vec],
        out_specs=rows if final else [rows, rows],
        out_shape=f32_rows if final else [f32_rows, jax.ShapeDtypeStruct((M, N), BF16)],
        compiler_params=_cp(("parallel",)),
        name="ple_add",
    )(h, g.reshape(1, K), wg, p, pw, next_g.reshape(1, N))


def _mlstm_chunk_kernel(qp_ref, kp_ref, v_ref, mo_ref, mz_ref, g_ref, gb_ref, cwq_ref, cwk_ref, mn_ref,
                        out_ref, C_out, n_out, m_out,
                        C_s, n_s, m_s, qbuf, kbuf):
    c = pl.program_id(2)
    nc = pl.num_programs(2)
    L = qp_ref.shape[0]
    HH = C_s.shape[0]
    Dh = qp_ref.shape[1] // HH
    heads = range(HH)
    hs = [slice(h * Dh, (h + 1) * Dh) for h in heads]

    @pl.when(c == 0)
    def _():
        C_s[...] = jnp.zeros(C_s.shape, F32)
        n_s[...] = jnp.zeros(n_s.shape, F32)
        m_s[...] = jnp.zeros(m_s.shape, F32)
        qbuf[0:8, :] = jnp.zeros((8, HH * Dh), F32)
        kbuf[0:8, :] = jnp.zeros((8, HH * Dh), F32)

    qbuf[8:8 + L, :] = qp_ref[...]
    kbuf[8:8 + L, :] = kp_ref[...]

    def conv(buf, w_ref):
        acc = w_ref[CONV_WIDTH:CONV_WIDTH + 1, :]
        for j in range(CONV_WIDTH):
            acc = acc + buf[8 - (CONV_WIDTH - 1) + j: 8 - (CONV_WIDTH - 1) + j + L, :] * w_ref[j:j + 1, :]
        return acc

    q = _silu(conv(qbuf, cwq_ref))
    k = _silu(conv(kbuf, cwk_ref)) * (Dh ** -0.5)
    qbuf[0:8, :] = qbuf[L:L + 8, :]
    kbuf[0:8, :] = kbuf[L:L + 8, :]
    v = v_ref[...]

    g = [g_ref[h] + gb_ref[h] for h in heads]
    li_r = [g[h][0:1, :] for h in heads]
    lf_r = [_log_sigmoid(g[h][1:2, :]) for h in heads]
    row = lax.broadcasted_iota(jnp.int32, (L, L), 0)
    col = lax.broadcasted_iota(jnp.int32, (L, L), 1)
    tri = row >= col
    diag = row == col
    b_c = [jnp.sum(jnp.where(tri, lf_r[h], 0.0), axis=1, keepdims=True) for h in heads]
    b_r = [jnp.sum(jnp.where(diag, b_c[h], 0.0), axis=0, keepdims=True) for h in heads]
    li_c = [jnp.sum(jnp.where(diag, li_r[h], 0.0), axis=1, keepdims=True) for h in heads]
    m_prev = [m_s[h] for h in heads]
    dmat = [jnp.where(tri, b_c[h] - b_r[h] + li_r[h], -jnp.inf) for h in heads]
    inter = [b_c[h] + m_prev[h] for h in heads]
    m_t = [jnp.maximum(inter[h], jnp.max(dmat[h], axis=1, keepdims=True)) for h in heads]
    w_intra = [jnp.exp(dmat[h] - m_t[h]) for h in heads]
    w_inter = [jnp.exp(inter[h] - m_t[h]) for h in heads]
    qb = [q[:, hs[h]].astype(BF16) for h in heads]
    kb = [k[:, hs[h]].astype(BF16) for h in heads]
    vb = [v[:, hs[h]].astype(BF16) for h in heads]
    C = [C_s[h] for h in heads]
    n_row = [n_s[h] for h in heads]
    s = [_dot_nt(qb[h], kb[h]) * w_intra[h] for h in heads]
    qC = [_dot_nt(qb[h], C[h].astype(BF16)) for h in heads]
    num = [_dot(s[h].astype(BF16), vb[h]) + w_inter[h] * qC[h] for h in heads]
    den = [jnp.sum(s[h], axis=1, keepdims=True)
           + w_inter[h] * jnp.sum(q[:, hs[h]] * n_row[h], axis=1, keepdims=True) for h in heads]
    hval = [num[h] / jnp.maximum(jnp.abs(den[h]), jnp.exp(-m_t[h])) for h in heads]
    m_new = [m_t[h][L - 1:L, :] for h in heads]
    bL = [b_c[h][L - 1:L, :] for h in heads]
    cd = [jnp.exp(bL[h] + m_prev[h] - m_new[h]) for h in heads]
    w_c = [jnp.exp(bL[h] - b_c[h] + li_c[h] - m_new[h]) for h in heads]
    w_r = [jnp.exp(bL[h] - b_r[h] + li_r[h] - m_new[h]) for h in heads]
    C_new = [cd[h] * C[h] + _dot_tn((v[:, hs[h]] * w_c[h]).astype(BF16), kb[h]) for h in heads]
    n_new = [cd[h] * n_row[h] + _dot(w_r[h].astype(BF16), kb[h]) for h in heads]
    hn = [hval[h] * lax.rsqrt(jnp.mean(hval[h] * hval[h], axis=-1, keepdims=True) + NORM_EPS) for h in heads]
    for h in heads:
        C_s[h] = C_new[h]
        n_s[h] = n_new[h]
        m_s[h] = m_new[h]
    out_ref[...] = jnp.concatenate(hn, axis=1) * mn_ref[...] * jax.nn.sigmoid(mo_ref[...]) * _silu(mz_ref[...])

    @pl.when(c == nc - 1)
    def _():
        for h in heads:
            C_out[h] = C_new[h]
            n_out[h] = n_new[h]
            m_out[h] = m_new[h]


def mlstm_prompt(proj, B, S, gi, gf, conv_w, conv_b, b_i, b_f, m_norm):
    H, Dh, L = M_HEADS, M_HEAD_DIM, M_CHUNK
    nc = S // L
    M = B * S
    g2 = jnp.stack([gi, gf], axis=-1).reshape(B, nc, L, H, 2).transpose(0, 3, 1, 4, 2)
    gb = jnp.stack([b_i, b_f], axis=-1).reshape(H, 2, 1)
    cw = jnp.concatenate([conv_w, conv_b.reshape(1, -1)], axis=0)
    HH = M_STEP_HEADS
    Wb = HH * Dh
    pblk = lambda off: pl.BlockSpec((L, Wb), lambda b, h, c: (b * nc + c, off + h))
    hb = M_WIDTH // Wb
    outs = pl.pallas_call(
        _mlstm_chunk_kernel,
        grid=(B, H // HH, nc),
        in_specs=[pblk(0), pblk(hb), pblk(2 * hb), pblk(3 * hb), pblk(4 * hb),
                  pl.BlockSpec((None, HH, None, 2, L), lambda b, h, c: (b, h, c, 0, 0)),
                  pl.BlockSpec((HH, 2, 1), lambda b, h, c: (h, 0, 0)),
                  pl.BlockSpec((CONV_WIDTH + 1, Wb), lambda b, h, c: (0, h)),
                  pl.BlockSpec((CONV_WIDTH + 1, Wb), lambda b, h, c: (0, hb + h)),
                  pl.BlockSpec((1, Wb), lambda b, h, c: (0, h))],
        out_specs=[pl.BlockSpec((L, Wb), lambda b, h, c: (b * nc + c, h)),
                   pl.BlockSpec((None, HH, Dh, Dh), lambda b, h, c: (b, h, 0, 0)),
                   pl.BlockSpec((None, HH, 1, Dh), lambda b, h, c: (b, h, 0, 0)),
                   pl.BlockSpec((None, HH, 1, 1), lambda b, h, c: (b, h, 0, 0))],
        out_shape=[jax.ShapeDtypeStruct((M, M_WIDTH), F32),
                   jax.ShapeDtypeStruct((B, H, Dh, Dh), F32),
                   jax.ShapeDtypeStruct((B, H, 1, Dh), F32),
                   jax.ShapeDtypeStruct((B, H, 1, 1), F32)],
        scratch_shapes=[pltpu.VMEM((HH, Dh, Dh), F32), pltpu.VMEM((HH, 1, Dh), F32), pltpu.VMEM((HH, 1, 1), F32),
                        pltpu.VMEM((L + 8, Wb), F32), pltpu.VMEM((L + 8, Wb), F32)],
        compiler_params=_cp(("parallel", "parallel", "arbitrary")),
        name="mlstm_prompt",
    )(proj, proj, proj, proj, proj, g2, gb, cw, cw, m_norm.reshape(1, -1))
    return outs


def _mlstm_step_kernel(x_ref, conv0_ref, cw_ref, cb_ref, v_ref, gi_ref, gf_ref, bi_ref, bf_ref,
                       mo_ref, mz_ref, mn_ref, C_ref, n_ref, m_ref,
                       out_ref, conv_out, C_out, n_out, m_out):
    H, Dh = v_ref.shape
    x = x_ref[...]
    qk = cb_ref[...]
    for j in range(CONV_WIDTH - 1):
        qk = qk + conv0_ref[j] * cw_ref[j]
    qk = qk + x * cw_ref[CONV_WIDTH - 1]
    for j in range(CONV_WIDTH - 2):
        conv_out[j] = conv0_ref[j + 1]
    conv_out[CONV_WIDTH - 2] = x
    qk = _silu(qk)
    q = qk[0:H]
    k = qk[H:2 * H] * (Dh ** -0.5)
    v = v_ref[...]
    li = gi_ref[...] + bi_ref[...]
    lf = _log_sigmoid(gf_ref[...] + bf_ref[...])
    m0 = m_ref[...]
    inter = lf + m0
    m_t = jnp.maximum(inter, li)
    w_intra = jnp.exp(li - m_t)
    w_inter = jnp.exp(inter - m_t)
    s = jnp.sum(q * k, axis=-1, keepdims=True) * w_intra
    n0 = n_ref[...]
    hrow = lax.broadcasted_iota(jnp.int32, (H, Dh), 0)
    cq = jnp.zeros((H, Dh), F32)
    vT = _dot_nt(_eye(Dh), v * w_intra, HI)
    for h in range(H):
        Ch = C_ref[h]
        cq = jnp.where(hrow == h, _dot_nt(q, Ch, HI), cq)
        C_out[h] = w_inter[h:h + 1, :] * Ch + vT[:, h:h + 1] * k[h:h + 1, :]
    num = s * v + w_inter * cq
    den = s + w_inter * jnp.sum(n0 * q, axis=-1, keepdims=True)
    hval = num / jnp.maximum(jnp.abs(den), jnp.exp(-m_t))
    n_out[...] = w_inter * n0 + w_intra * k
    m_out[...] = m_t
    hn = hval * lax.rsqrt(jnp.mean(hval * hval, axis=-1, keepdims=True) + NORM_EPS) * mn_ref[...]
    out_ref[...] = hn * jax.nn.sigmoid(mo_ref[...]) * _silu(mz_ref[...])


def mlstm_step(x16, conv0, conv_w, conv_b, v, gi, gf, b_i, b_f, mo, mz, m_norm, C0, n0, m0):
    B = x16.shape[0]
    H, Dh = M_HEADS, M_HEAD_DIM
    W = CONV_WIDTH
    per_b = lambda *shape: pl.BlockSpec((None,) + shape, lambda b: (b,) + (0,) * len(shape))
    const = lambda *shape: pl.BlockSpec(shape, lambda b: (0,) * len(shape))
    return pl.pallas_call(
        _mlstm_step_kernel,
        grid=(B,),
        in_specs=[per_b(2 * H, Dh), per_b(W - 1, 2 * H, Dh), const(W, 2 * H, Dh), const(2 * H, Dh),
                  per_b(H, Dh), per_b(H, 1), per_b(H, 1), const(H, 1), const(H, 1),
                  per_b(H, Dh), per_b(H, Dh), const(H, Dh),
                  per_b(H, Dh, Dh), per_b(H, Dh), per_b(H, 1)],
        out_specs=[per_b(H, Dh), per_b(W - 1, 2 * H, Dh), per_b(H, Dh, Dh), per_b(H, Dh), per_b(H, 1)],
        out_shape=[jax.ShapeDtypeStruct((B, H, Dh), F32),
                   jax.ShapeDtypeStruct((B, W - 1, 2 * H, Dh), F32),
                   jax.ShapeDtypeStruct((B, H, Dh, Dh), F32),
                   jax.ShapeDtypeStruct((B, H, Dh), F32),
                   jax.ShapeDtypeStruct((B, H, 1), F32)],
        compiler_params=_cp(("parallel",)),
        name="mlstm_step",
    )(x16, conv0, conv_w, conv_b, v, gi, gf, b_i, b_f, mo, mz, m_norm, C0, n0, m0)


def _rwkv_chunk_kernel(rr_ref, rk_ref, rv_ref, rz_ref, rw_ref, ra_ref, mu4_ref, mu2_ref, par_ref,
                       w2_ref, a2_ref, out_ref, S_out, S_s, carry_s):
    c = pl.program_id(2)
    nc = pl.num_programs(2)
    TB = rr_ref.shape[0]
    W = rr_ref.shape[1]
    N = R_HEAD_DIM
    L = R_CHUNK

    @pl.when(c == 0)
    def _():
        S_s[...] = jnp.zeros(S_s.shape, F32)
        carry_s[...] = jnp.zeros(carry_s.shape, F32)

    row = lax.broadcasted_iota(jnp.int32, (TB, W), 0)
    lo = lax.broadcasted_iota(jnp.int32, (TB, W), 1) < N

    def shift(x, idx, mu):
        prev = jnp.where(row == 0, carry_s[idx:idx + 1, :], pltpu.roll(x, 1, 0))
        carry_s[idx:idx + 1, :] = x[TB - 1:TB, :]
        return x + mu * (prev - x)

    r = shift(rr_ref[...], 0, mu4_ref[0:1, :])
    k = shift(rk_ref[...], 1, mu4_ref[1:2, :])
    v = shift(rv_ref[...], 2, mu4_ref[2:3, :])
    z = shift(rz_ref[...], 3, mu4_ref[3:4, :])
    xw = shift(rw_ref[...], 4, mu2_ref[0:1, :])
    xa = shift(ra_ref[...], 5, mu2_ref[1:2, :])
    w0, a0 = par_ref[0:1, :], par_ref[1:2, :]
    k_k, k_a, r_k = par_ref[2:3, :], par_ref[3:4, :], par_ref[4:5, :]
    gn_w, gn_b = par_ref[5:6, :], par_ref[6:7, :]

    def seg_sum(x):
        s0 = jnp.sum(jnp.where(lo, x, 0.0), axis=1, keepdims=True)
        s1 = jnp.sum(jnp.where(lo, 0.0, x), axis=1, keepdims=True)
        return jnp.where(lo, s0, s1)

    w_log = -_softplus(-(w0 + _dot3(jnp.tanh(xw), w2_ref[...]))) - 0.5
    logw = -jnp.exp(w_log)
    a = jax.nn.sigmoid(a0 + _dot3(xa, a2_ref[...]))
    kk = k * k_k
    kk = kk * lax.rsqrt(jnp.maximum(seg_sum(kk * kk), 1e-24))
    k2 = k * (1.0 + (a - 1.0) * k_a)
    alpha = -kk
    beta = kk * a

    tr = lax.broadcasted_iota(jnp.int32, (TB, TB), 0)
    tc = lax.broadcasted_iota(jnp.int32, (TB, TB), 1)
    in_chunk = jnp.logical_and(tc <= tr, tc >= tr - jnp.bitwise_and(tr, L - 1))
    l_hi = logw.astype(BF16)
    l_r = logw - l_hi.astype(F32)
    l_mid = l_r.astype(BF16)
    l_lo = (l_r - l_mid.astype(F32)).astype(BF16)
    b3 = _dot(jnp.where(in_chunk, 1.0, 0.0).astype(BF16), jnp.concatenate([l_hi, l_mid, l_lo], axis=1))
    b = (b3[:, 0:W] + b3[:, W:2 * W]) + b3[:, 2 * W:3 * W]

    lo_c = lax.broadcasted_iota(jnp.int32, (L, W), 1) < N

    def stack(x):
        return jnp.concatenate([jnp.where(lo_c, x, 0.0), jnp.where(lo_c, 0.0, x)], axis=0)

    ri = lax.broadcasted_iota(jnp.int32, (2 * L, 2 * L), 0)
    ci = lax.broadcasted_iota(jnp.int32, (2 * L, 2 * L), 1)
    same = jnp.where(ri >= L, 1, 0) == jnp.where(ci >= L, 1, 0)
    strict = jnp.logical_and(same, ci < ri)
    incl = jnp.logical_and(same, ci <= ri)
    eye = jnp.where(ri == ci, 1.0, 0.0)

    chunks = range(TB // L)
    sls = [slice(ch * L, (ch + 1) * L) for ch in chunks]
    bLs = [b[sl][L - 1:L, :] for sl in sls]
    a_s = [stack(alpha[sl] * jnp.exp(b[sl] - logw[sl])) for sl in sls]
    r_s = [stack(r[sl] * jnp.exp(b[sl])) for sl in sls]
    Ys_ = [jnp.concatenate([stack(beta[sl] * jnp.exp(-b[sl])), stack(k2[sl] * jnp.exp(-b[sl]))], axis=0) for sl in sls]
    bh_s = [stack(beta[sl] * jnp.exp(bL - b[sl])) for sl, bL in zip(sls, bLs)]
    kh_s = [stack(k2[sl] * jnp.exp(bL - b[sl])) for sl, bL in zip(sls, bLs)]
    v_s = [stack(v[sl]) for sl in sls]
    Gs = [_dot3_nt(a_s[i], Ys_[i]) for i in chunks]
    Go = [_dot_nt(r_s[i].astype(BF16), Ys_[i].astype(BF16)) for i in chunks]
    A = [jnp.where(strict, Gs[i][:, 0:2 * L], 0.0) for i in chunks]
    Bm = [jnp.where(strict, Gs[i][:, 2 * L:4 * L], 0.0) for i in chunks]
    P = [jnp.where(incl, Go[i][:, 0:2 * L], 0.0).astype(BF16) for i in chunks]
    Q = [jnp.where(incl, Go[i][:, 2 * L:4 * L], 0.0).astype(BF16) for i in chunks]
    BV = [_dot3(Bm[i], v_s[i]) for i in chunks]
    QV = [_dot(Q[i], v_s[i].astype(BF16)) for i in chunks]
    T = [eye + A[i] for i in chunks]
    Ap = [_dot3(A[i], A[i]) for i in chunks]
    for _ in range(4):
        R = [_dot(_lhs3(Ap[i], 1), _rhs3(jnp.concatenate([Ap[i], T[i]], axis=1), 0)) for i in chunks]
        T = [T[i] + R[i][:, 2 * L:4 * L] for i in chunks]
        Ap = [R[i][:, 0:2 * L] for i in chunks]
    T = [T[i] + _dot3(Ap[i], T[i]) for i in chunks]
    TXB = [_dot3(T[i], jnp.concatenate([a_s[i], BV[i]], axis=1)) for i in chunks]
    PTX = [_dot(P[i], TXB[i].astype(BF16)) for i in chunks]
    MN = [_dot3_tn(TXB[i], bh_s[i]) for i in chunks]
    VK = [_dot3_tn(v_s[i], kh_s[i]) for i in chunks]
    pre = [((r_s[i] + PTX[i][:, 0:W]).astype(BF16), QV[i] + PTX[i][:, W:2 * W],
            MN[i][0:W], MN[i][W:2 * W] + VK[i], jnp.exp(bLs[i])) for i in chunks]

    S = S_s[...]
    ys = []
    for Rp, Y0, Mc, Nc, gL in pre:
        Ys = _dot_nt(Rp, S.astype(BF16)) + Y0
        ys.append(Ys[0:L] + Ys[L:2 * L])
        S = (S * gL + _dot3(S, Mc)) + Nc
    S_new = S
    S_s[...] = S_new
    y = jnp.concatenate(ys, axis=0)

    mu_y = seg_sum(y) * (1.0 / N)
    d = y - mu_y
    var = seg_sum(d * d) * (1.0 / N)
    yn = d * lax.rsqrt(var + RWKV_GN_EPS) * gn_w + gn_b
    yn = yn + seg_sum(r * k2 * r_k) * v
    out_ref[...] = yn * _silu(z)

    @pl.when(c == nc - 1)
    def _():
        S_out[...] = S_new


def rwkv_prompt(proj, B, S, mu4, mu2, par, w2p, a2p, r_off):
    L = R_STEP_TOKENS
    W = 2 * R_HEAD_DIM
    nc = S // L
    M = B * S
    npair = mu4.shape[1] // W
    blk = lambda off: pl.BlockSpec((L, W), lambda b, p, c: (b * nc + c, off + p))
    cblk = lambda off: pl.BlockSpec((L, W), lambda b, p, c: (b * nc + c, off))
    return pl.pallas_call(
        _rwkv_chunk_kernel,
        grid=(B, npair, nc),
        in_specs=[blk(r_off), blk(r_off + npair), blk(r_off + 2 * npair), blk(r_off + 3 * npair),
                  cblk(r_off + 4 * npair), cblk(r_off + 4 * npair + 1),
                  pl.BlockSpec((8, W), lambda b, p, c: (0, p)),
                  pl.BlockSpec((8, W), lambda b, p, c: (0, 0)),
                  pl.BlockSpec((8, W), lambda b, p, c: (0, p)),
                  pl.BlockSpec((W, W), lambda b, p, c: (0, p)),
                  pl.BlockSpec((W, W), lambda b, p, c: (0, p))],
        out_specs=[pl.BlockSpec((L, W), lambda b, p, c: (b * nc + c, p)),
                   pl.BlockSpec((None, None, W, W), lambda b, p, c: (b, p, 0, 0))],
        out_shape=[jax.ShapeDtypeStruct((M, npair * W), F32),
                   jax.ShapeDtypeStruct((B, npair, W, W), F32)],
        scratch_shapes=[pltpu.VMEM((W, W), F32), pltpu.VMEM((8, W), F32)],
        compiler_params=_cp(("parallel", "parallel", "arbitrary")),
        name="rwkv_prompt",
    )(proj, proj, proj, proj, proj, proj, mu4, mu2, par, w2p, a2p)


def _rwkv_prep_kernel(rr_ref, rk_ref, rv_ref, rz_ref, rw_ref, ra_ref, sh4_ref, sh2_ref, mu4_ref, mu2_ref,
                      w0_ref, a0_ref, w2_ref, a2_ref, r_o, k_o, v_o, z_o, w_o, a_o):
    Wd = rr_ref.shape[1]
    sh = lambda x, prev, mu: x + mu * (prev - x)
    r_o[...] = sh(rr_ref[...], sh4_ref[:, 0:Wd], mu4_ref[:, 0:Wd])
    k_o[...] = sh(rk_ref[...], sh4_ref[:, Wd:2 * Wd], mu4_ref[:, Wd:2 * Wd])
    v_o[...] = sh(rv_ref[...], sh4_ref[:, 2 * Wd:3 * Wd], mu4_ref[:, 2 * Wd:3 * Wd])
    z_o[...] = sh(rz_ref[...], sh4_ref[:, 3 * Wd:4 * Wd], mu4_ref[:, 3 * Wd:4 * Wd])
    xw = sh(rw_ref[...], sh2_ref[:, 0:LANES], mu2_ref[:, 0:LANES])
    xa = sh(ra_ref[...], sh2_ref[:, LANES:2 * LANES], mu2_ref[:, LANES:2 * LANES])
    w_log = -_softplus(-(w0_ref[...] + _dot(jnp.tanh(xw), w2_ref[...], HI))) - 0.5
    w_o[...] = jnp.exp(-jnp.exp(w_log))
    a_o[...] = jax.nn.sigmoid(a0_ref[...] + _dot(xa, a2_ref[...], HI))


def rwkv_prep(proj, sh4, sh2, mu4, mu2, w0, a0, w2p, a2p, r_blk, s_blk):
    B = proj.shape[0]
    Wd = w0.shape[1]
    full = lambda a: pl.BlockSpec(a.shape, lambda i: (0,) * a.ndim)
    return pl.pallas_call(
        _rwkv_prep_kernel,
        grid=(1,),
        in_specs=[pl.BlockSpec((B, Wd), lambda i: (0, r_blk)), pl.BlockSpec((B, Wd), lambda i: (0, r_blk + 1)),
                  pl.BlockSpec((B, Wd), lambda i: (0, r_blk + 2)), pl.BlockSpec((B, Wd), lambda i: (0, r_blk + 3)),
                  pl.BlockSpec((B, LANES), lambda i: (0, s_blk)), pl.BlockSpec((B, LANES), lambda i: (0, s_blk + 1)),
                  full(sh4), full(sh2), full(mu4), full(mu2), full(w0), full(a0), full(w2p), full(a2p)],
        out_specs=[pl.BlockSpec((B, Wd), lambda i: (0, 0))] * 6,
        out_shape=[jax.ShapeDtypeStruct((B, Wd), F32)] * 6,
        compiler_params=_cp(("arbitrary",)),
        name="rwkv_prep",
    )(proj, proj, proj, proj, proj, proj, sh4, sh2, mu4, mu2, w0, a0, w2p, a2p)


def _rwkv_step_kernel(r_ref, k_ref, v_ref, z_ref, w_ref, a_ref, par_ref, S_ref, out_ref, S_out):
    H, N = r_ref.shape
    r, k, v, z, w, a = (t[...] for t in (r_ref, k_ref, v_ref, z_ref, w_ref, a_ref))
    k_k, k_a, r_k, gn_w, gn_b = (par_ref[i] for i in range(5))
    kk = k * k_k
    kk = kk * lax.rsqrt(jnp.maximum(jnp.sum(kk * kk, axis=-1, keepdims=True), 1e-24))
    k2 = k * (1.0 + (a - 1.0) * k_a)
    alpha = -kk
    beta = kk * a
    vT = _dot_nt(_eye(N), v, HI)
    lane_h = lax.broadcasted_iota(jnp.int32, (N, H), 1)
    yT = jnp.zeros((N, H), F32)
    for h in range(H):
        S = S_ref[h]
        sa = jnp.sum(S * alpha[h:h + 1, :], axis=-1, keepdims=True)
        Sn = S * w[h:h + 1, :] + sa * beta[h:h + 1, :] + vT[:, h:h + 1] * k2[h:h + 1, :]
        S_out[h] = Sn
        yT = jnp.where(lane_h == h, jnp.sum(Sn * r[h:h + 1, :], axis=-1, keepdims=True), yT)
    y = _dot_nt(_eye(H), yT, HI)
    mu_y = jnp.mean(y, axis=-1, keepdims=True)
    d = y - mu_y
    var = jnp.mean(d * d, axis=-1, keepdims=True)
    yn = d * lax.rsqrt(var + RWKV_GN_EPS) * gn_w + gn_b
    yn = yn + jnp.sum(r * k2 * r_k, axis=-1, keepdims=True) * v
    out_ref[...] = yn * _silu(z)


def rwkv_step(r, k, v, z, w, a, par, S0):
    B, H, N = r.shape
    vec = pl.BlockSpec((None, H, N), lambda b: (b, 0, 0))
    mat = pl.BlockSpec((None, H, N, N), lambda b: (b, 0, 0, 0))
    return pl.pallas_call(
        _rwkv_step_kernel,
        grid=(B,),
        in_specs=[vec] * 6 + [pl.BlockSpec(par.shape, lambda b: (0, 0, 0)), mat],
        out_specs=[vec, mat],
        out_shape=[jax.ShapeDtypeStruct((B, H, N), F32), jax.ShapeDtypeStruct((B, H, N, N), F32)],
        compiler_params=_cp(("parallel",)),
        name="rwkv_step",
    )(r, k, v, z, w, a, par, S0)


def _qk_norm_kernel(q_ref, k_ref, v_ref, f_ref, gq_ref, gk_ref, bf_ref, kn_ref, v_out_ref, lf_ref, *q_refs,
                    mxu_copies):
    D = F_HEAD_DIM
    for h in range(q_ref.shape[1] // D):
        sl = slice(h * D, (h + 1) * D)
        x = q_ref[:, sl]
        qn = x * lax.rsqrt(jnp.mean(x * x, axis=-1, keepdims=True) + NORM_EPS) * gq_ref[...]
        y = k_ref[:, sl]
        kn = y * lax.rsqrt(jnp.mean(y * y, axis=-1, keepdims=True) + NORM_EPS) * gk_ref[...]
        v = v_ref[:, sl]
        kn_ref[:, sl] = kn
        v_out_ref[:, sl] = v
        if mxu_copies:
            qb_ref, kb_ref, vb_ref = q_refs
            qb_ref[:, sl] = (qn * (D ** -0.5 * LOG2E)).astype(BF16)
            kb_ref[:, sl] = kn.astype(BF16)
            vb_ref[:, sl] = v.astype(BF16)
        else:
            q_refs[0][:, sl] = qn
    lf_ref[...] = _log_sigmoid(f_ref[...] + bf_ref[...])


def qk_norm(proj, gq, gk, bf_pad, tm, Wd, f_blk, mxu_copies):
    M = proj.shape[0]
    wide = pl.BlockSpec((tm, Wd), lambda i: (i, 0))
    wide_sds = lambda dt: jax.ShapeDtypeStruct((M, Wd), dt)
    extra = [wide_sds(BF16)] * 3 if mxu_copies else [wide_sds(F32)]
    return pl.pallas_call(
        functools.partial(_qk_norm_kernel, mxu_copies=mxu_copies),
        grid=(M // tm,),
        in_specs=[pl.BlockSpec((tm, Wd), lambda i: (i, 0)), pl.BlockSpec((tm, Wd), lambda i: (i, 1)),
                  pl.BlockSpec((tm, Wd), lambda i: (i, 2)),
                  pl.BlockSpec((tm, LANES), lambda i: (i, f_blk)),
                  pl.BlockSpec((1, F_HEAD_DIM), lambda i: (0, 0)), pl.BlockSpec((1, F_HEAD_DIM), lambda i: (0, 0)),
                  pl.BlockSpec((1, LANES), lambda i: (0, 0))],
        out_specs=[wide, wide, pl.BlockSpec((tm, LANES), lambda i: (i, 0))] + [wide] * len(extra),
        out_shape=[wide_sds(F32), wide_sds(F32), jax.ShapeDtypeStruct((M, LANES), F32)] + extra,
        compiler_params=_cp(("parallel",)),
        name="qk_norm",
    )(proj, proj, proj, proj, gq.reshape(1, -1), gk.reshape(1, -1), bf_pad)


def _cumsum_kernel(x_ref, c_ref):
    H, S = x_ref.shape
    U = (lax.broadcasted_iota(jnp.int32, (LANES, LANES), 0) <= lax.broadcasted_iota(jnp.int32, (LANES, LANES), 1)).astype(F32)
    off = jnp.zeros((H, 1), F32)
    for t in range(S // LANES):
        sl = slice(t * LANES, (t + 1) * LANES)
        w = _dot(x_ref[:, sl], U, HI) + off
        c_ref[:, sl] = w
        off = w[:, LANES - 1:LANES]


def cumsum_lanes(x):
    B, H, S = x.shape
    return pl.pallas_call(
        _cumsum_kernel,
        grid=(B,),
        in_specs=[pl.BlockSpec((None, H, S), lambda b: (b, 0, 0))],
        out_specs=pl.BlockSpec((None, H, S), lambda b: (b, 0, 0)),
        out_shape=jax.ShapeDtypeStruct((B, H, S), F32),
        compiler_params=_cp(("parallel",)),
        name="forget_cumsum",
    )(x)


def _fox_flash_kernel(q_ref, k_ref, v_ref, ck_ref, o_ref, m_s, l_s, acc_s):
    i = pl.program_id(2)
    t, D = q_ref.shape
    q = q_ref[...]
    m_s[...] = jnp.full(m_s.shape, -jnp.inf, F32)
    l_s[...] = jnp.zeros(l_s.shape, F32)
    acc_s[...] = jnp.zeros(acc_s.shape, F32)

    def block(j, masked):
        start = pl.multiple_of(j * t, t)
        s = _dot_nt(q, k_ref[pl.ds(start, t), :]) - ck_ref[j] * LOG2E
        if masked:
            row = lax.broadcasted_iota(jnp.int32, (t, t), 0)
            col = lax.broadcasted_iota(jnp.int32, (t, t), 1)
            s = jnp.where(col <= row, s, -jnp.inf)
        m_prev = m_s[...]
        m_new = jnp.maximum(m_prev, jnp.max(s, axis=1, keepdims=True))
        alpha = jnp.exp2(m_prev - m_new)
        p = jnp.exp2(s - jnp.concatenate([m_new] * (t // LANES), axis=1))
        l_s[...] = alpha * l_s[...] + jnp.sum(p, axis=1, keepdims=True)
        acc_s[...] = alpha * acc_s[...] + _dot(p.astype(BF16), v_ref[pl.ds(start, t), :])
        m_s[...] = m_new

    def body(j, carry):
        block(j, False)
        return carry

    lax.fori_loop(0, i, body, 0)
    block(i, True)
    o_ref[...] = acc_s[...] / l_s[...]


def fox_prompt_attn(qb, kb, vb, ck, B, S, t):
    D = F_HEAD_DIM
    H = qb.shape[1] // D
    nb = S // t
    full = pl.BlockSpec((S, D), lambda b, h, i: (b, h))
    return pl.pallas_call(
        _fox_flash_kernel,
        grid=(B, H, nb),
        in_specs=[pl.BlockSpec((t, D), lambda b, h, i: (b * nb + i, h)), full, full,
                  pl.BlockSpec((None, None, nb, 1, t), lambda b, h, i: (b, h, 0, 0, 0))],
        out_specs=pl.BlockSpec((t, D), lambda b, h, i: (b * nb + i, h)),
        out_shape=jax.ShapeDtypeStruct((B * S, H * D), F32),
        scratch_shapes=[pltpu.VMEM((t, LANES), F32), pltpu.VMEM((t, LANES), F32), pltpu.VMEM((t, D), F32)],
        compiler_params=_cp(("parallel", "parallel", "parallel")),
        name="fox_prompt_attn",
    )(qb, kb, vb, ck)


def _fox_decode_kernel(pt_ref, q_ref, kn_ref, vn_ref, lfn_ref, *refs, npg):
    k_refs, v_refs, lf_refs = refs[0:npg], refs[npg:2 * npg], refs[2 * npg:3 * npg]
    o_ref = refs[3 * npg]
    m_s, l_s, acc_s, off_s, cp_s = refs[3 * npg + 1:]
    p = pl.program_id(1)
    H, D = q_ref.shape
    T = k_refs[0].shape[0]
    pages = range(npg)

    @pl.when(p == 0)
    def _():
        m_s[...] = jnp.full(m_s.shape, -jnp.inf, F32)
        l_s[...] = jnp.zeros(l_s.shape, F32)
        acc_s[...] = jnp.zeros(acc_s.shape, F32)
        off_s[...] = jnp.zeros(off_s.shape, F32)
        cp_s[...] = jnp.zeros(cp_s.shape, F32)

    q = q_ref[...] * (D ** -0.5 * LOG2E)
    hmask = (lax.broadcasted_iota(jnp.int32, (H, LANES), 0) == lax.broadcasted_iota(jnp.int32, (H, LANES), 1)).astype(F32)
    hmask2 = hmask * LOG2E
    trow = lax.broadcasted_iota(jnp.int32, (T, LANES), 0)
    for g in pages:
        cp_s[g, :, 0:H] = lf_refs[g][...]
    cps = [cp_s[g] for g in pages]
    sh = 1
    while sh < T:
        cps = [x + jnp.where(trow >= sh, pltpu.roll(x, sh, 0), 0.0) for x in cps]
        sh *= 2
    for g in pages:
        cp_s[g] = cps[g]
    cbm = [jnp.stack([jnp.broadcast_to(cp_s[g, t:t + 1, :], (H, LANES)) for t in range(T)], axis=0) * hmask2[None]
           for g in pages]
    s = [jnp.sum(k_refs[g][...] * q[None] - cbm[g], axis=-1, keepdims=True) for g in pages]
    tot = [jnp.sum(jnp.broadcast_to(cp_s[g, T - 1:T, :], (H, LANES)) * hmask, axis=-1, keepdims=True) for g in pages]
    mg = [jnp.max(s[g], axis=0) for g in pages]
    pr = [jnp.exp2(s[g] - mg[g][None]) for g in pages]
    lg = [jnp.sum(pr[g], axis=0) for g in pages]
    ag = [jnp.sum(pr[g] * v_refs[g][...], axis=0) for g in pages]
    off = off_s[:, 0:1]
    m_prev = m_s[:, 0:1]
    offs = []
    off_new = off
    for g in pages:
        offs.append(off_new * LOG2E)
        off_new = off_new + tot[g]
    m_new = m_prev
    for g in pages:
        m_new = jnp.maximum(m_new, mg[g] - offs[g])
    alpha = jnp.exp2(m_prev - m_new)
    l_new = alpha * l_s[:, 0:1]
    acc_new = alpha * acc_s[...]
    for g in pages:
        wg = jnp.exp2(mg[g] - offs[g] - m_new)
        l_new = l_new + wg * lg[g]
        acc_new = acc_new + wg * ag[g]
    m_s[...] = jnp.broadcast_to(m_new, m_s.shape)
    l_s[...] = jnp.broadcast_to(l_new, l_s.shape)
    acc_s[...] = acc_new
    off_s[...] = jnp.broadcast_to(off_new, off_s.shape)

    @pl.when(p == pl.num_programs(1) - 1)
    def _():
        cq = (off_new + lfn_ref[:, 0:1]) * LOG2E
        s_self = jnp.sum(q * kn_ref[...], axis=-1, keepdims=True) - cq
        m_f = jnp.maximum(m_new, s_self)
        a2 = jnp.exp2(m_new - m_f)
        ps = jnp.exp2(s_self - m_f)
        o_ref[...] = (a2 * acc_new + ps * vn_ref[...]) / (a2 * l_new + ps)


def fox_decode_attn(q, k_new, v_new, lf_new, cache_k, cache_v, cache_lf, page_table):
    B, H, D = q.shape
    n_pages = page_table.shape[1]
    T = cache_k.shape[1]
    npg = DEC_PAGES_PER_STEP if n_pages % DEC_PAGES_PER_STEP == 0 else 1
    vec = pl.BlockSpec((None, H, D), lambda b, p, pt: (b, 0, 0))
    kv = lambda g: pl.BlockSpec((None, T, H, D), lambda b, p, pt: (pt[b, p * npg + g], 0, 0, 0))
    lfs = lambda g: pl.BlockSpec((None, T, H), lambda b, p, pt: (pt[b, p * npg + g], 0, 0))
    grid_spec = pltpu.PrefetchScalarGridSpec(
        num_scalar_prefetch=1,
        grid=(B, n_pages // npg),
        in_specs=[vec, vec, vec, vec] + [kv(g) for g in range(npg)] * 2 + [lfs(g) for g in range(npg)],
        out_specs=pl.BlockSpec((None, H, D), lambda b, p, pt: (b, 0, 0)),
        scratch_shapes=[pltpu.VMEM((H, D), F32)] * 4 + [pltpu.VMEM((npg, T, LANES), F32)],
    )
    return pl.pallas_call(
        functools.partial(_fox_decode_kernel, npg=npg),
        grid_spec=grid_spec,
        out_shape=jax.ShapeDtypeStruct((B, H, D), F32),
        compiler_params=_cp(("parallel", "arbitrary")),
        name="fox_decode_attn",
    )(page_table, q, k_new, v_new, lf_new, *([cache_k] * npg), *([cache_v] * npg), *([cache_lf] * npg))


def _pad_cols(a, width):
    return jnp.pad(a, [(0, 0)] * (a.ndim - 1) + [(0, width - a.shape[-1])])


def _split_mix(a):
    W, H, D = M_WIDTH, M_HEADS, a.shape[-1]
    o = 0
    parts = {}
    for name, n in (("mqk", 2 * W), ("mv", W), ("mi", H), ("mf", H), ("mo", W), ("mz", W)):
        parts[name] = a[..., o:o + n]
        o += n
    Wr = (D - o - 2 * R_LORA) // 4
    for name, n in (("rr", Wr), ("rk", Wr), ("rv", Wr), ("rw", R_LORA), ("ra", R_LORA), ("rz", Wr)):
        parts[name] = a[..., o:o + n]
        o += n
    return parts


def _small_mix(p):
    z = jnp.zeros(p["rw"].shape[:-1] + (LANES - R_LORA,), p["rw"].dtype)
    return _pad_cols(jnp.concatenate([p["rw"], z, p["ra"], z, p["mi"], p["mf"]], axis=-1), SMALL_W)


def _arrange_mix(a):
    p = _split_mix(a)
    return jnp.concatenate([p["mqk"], p["mv"], p["mo"], p["mz"], p["rr"], p["rk"], p["rv"], p["rz"], _small_mix(p)],
                           axis=-1)


def _trunk(x, p_in, state, attn, Wt, tm):
    B, T, D = x.shape
    M = B * T
    x2 = x.reshape(M, D)
    tn = 512
    tm_in = 4 * tm if M % (4 * tm) == 0 else tm
    tm_w = min(tm, 256)
    Wr = D
    proj = matmul_bf16(norm_cast(x2, Wt["norm_w"][0], tm_w), Wt["mix_w"], tm_in, tn)
    W = M_WIDTH
    small0 = 3 * W + 2 * W + 4 * Wr
    gi = proj[:, small0 + 2 * LANES: small0 + 2 * LANES + M_HEADS]
    gf = proj[:, small0 + 2 * LANES + M_HEADS: small0 + 2 * LANES + 2 * M_HEADS]
    r_col0 = 5 * W

    if state is None:
        bm, C, n, m = mlstm_prompt(proj, B, T, gi.reshape(B, T, M_HEADS), gf.reshape(B, T, M_HEADS),
                                   Wt["m_conv_w"], Wt["m_conv_b"], Wt["m_b_i"], Wt["m_b_f"], Wt["m_norm"])
        conv_new = proj.reshape(B, T, -1)[:, T - (CONV_WIDTH - 1):, 0:2 * W]
        C = C
        n = n.reshape(B, M_HEADS, M_HEAD_DIM)
        m = m.reshape(B, M_HEADS)
        br, Sbd = rwkv_prompt(proj, B, T, Wt["mu4"], Wt["mu2"], Wt["r_par"], Wt["w2p"], Wt["a2p"],
                              r_col0 // LANES)
        N = R_HEAD_DIM
        S_new = jnp.stack([Sbd[:, :, 0:N, 0:N], Sbd[:, :, N:2 * N, N:2 * N]], axis=2).reshape(B, Wr // N, N, N)
    else:
        conv0, C0, n0, m0, S0, sh0 = state
        H, Dh = M_HEADS, M_HEAD_DIM
        bm, conv_new, C, n, m = mlstm_step(
            proj[:, 0:2 * W].reshape(B, 2 * H, Dh), conv0.reshape(B, CONV_WIDTH - 1, 2 * H, Dh),
            Wt["m_conv_w"].reshape(CONV_WIDTH, 2 * H, Dh), Wt["m_conv_b"].reshape(2 * H, Dh),
            proj[:, 2 * W:3 * W].reshape(B, H, Dh), gi.reshape(B, H, 1), gf.reshape(B, H, 1),
            Wt["m_b_i"].reshape(H, 1), Wt["m_b_f"].reshape(H, 1),
            proj[:, 3 * W:4 * W].reshape(B, H, Dh), proj[:, 4 * W:5 * W].reshape(B, H, Dh),
            Wt["m_norm"].reshape(H, Dh), C0, n0, m0.reshape(B, H, 1))
        bm = bm.reshape(M, W)
        conv_new = conv_new.reshape(B, CONV_WIDTH - 1, 2 * W)
        m = m.reshape(B, H)
        shp = _split_mix(jnp.concatenate([jnp.zeros((B, 5 * W + 2 * M_HEADS), F32), sh0], axis=-1))
        sh4 = jnp.concatenate([shp["rr"], shp["rk"], shp["rv"], shp["rz"]], axis=-1)
        sh2 = _small_mix(shp)[:, 0:2 * LANES]
        N = R_HEAD_DIM
        Hr = Wr // N
        r_, k_, v_, z_, w_, a_ = rwkv_prep(proj, sh4, sh2, Wt["mu4_flat"],
                                           Wt["mu2_flat"], Wt["r_par"][0:1], Wt["r_par"][1:2], Wt["w2p"], Wt["a2p"],
                                           r_col0 // Wr, small0 // LANES)
        hs = lambda t: t.reshape(B, Hr, N)
        par5 = Wt["r_par"][2:7].reshape(5, Hr, N)
        br, S_new = rwkv_step(hs(r_), hs(k_), hs(v_), hs(z_), hs(w_), hs(a_), par5, S0)
        br = br.reshape(M, Wr)
    last = proj.reshape(B, T, -1)[:, T - 1]
    sh_new = jnp.concatenate([last[:, r_col0:r_col0 + 3 * Wr], last[:, small0:small0 + R_LORA],
                              last[:, small0 + LANES:small0 + LANES + R_LORA],
                              last[:, r_col0 + 3 * Wr:r_col0 + 4 * Wr]], axis=-1)

    h1 = out_mix(bm, br, Wt["out_mix_w"], x2, min(tm, 128))
    h1, xn1 = ple_add(h1, Wt["ple_norm"][0], Wt["ple_gate_w"][0], p_in[0].reshape(M, -1), Wt["ple_proj"][0],
                      Wt["norm_w"][1], tm_w, final=False)

    proj1 = matmul_bf16(xn1, Wt["fox_w"], tm_in, tn)
    kn, vv, lf, *qs = qk_norm(proj1, Wt["f_q_norm"], Wt["f_k_norm"], Wt["f_b_f_pad"], min(tm, 256), D,
                              4 * D // LANES, state is None)
    Hf = D // F_HEAD_DIM
    lf = lf[:, 0:Hf]
    o = attn(qs, kn, vv, lf, B, T)
    h2 = out_fox(o, proj1, 3, Wt["out_fox_w"], h1, tm_w)
    y = ple_add(h2, Wt["ple_norm"][1], Wt["ple_gate_w"][1], p_in[1].reshape(M, -1), Wt["ple_proj"][1],
                Wt["final_norm"], tm_w, final=True).reshape(B, T, D)

    k_rows = kn.reshape(1, B, T, Hf, F_HEAD_DIM)
    v_rows = vv.reshape(1, B, T, Hf, F_HEAD_DIM)
    lf_rows = lf.reshape(1, B, T, Hf)
    mix_state = tuple(t[None] for t in (conv_new, C, n, m, S_new, sh_new))
    return y, (k_rows, v_rows, lf_rows), mix_state


def kernel(x_prompt, x_sample, cache_k, cache_v, cache_lf, state_mlstm_conv, state_mlstm_C, state_mlstm_n,
           state_mlstm_m, state_rwkv_S, state_rwkv_shift, page_table, p_prompt, p_sample,
           norm_w, final_norm, w_in_mix, w_out_mix, m_conv_w, m_conv_b, m_b_i, m_b_f, m_norm,
           r_mu, r_w0, r_w2, r_a0, r_a2, r_k_k, r_k_a, r_r_k, r_gn_w, r_gn_b,
           w_in_fox, w_out_fox, f_b_f, f_q_norm, f_k_norm, ple_proj, ple_gate_w, ple_norm):
    D = x_prompt.shape[-1]
    Wr = r_w0.shape[-1]
    Hf = D // F_HEAD_DIM
    mix_w = _arrange_mix(w_in_mix[0]).astype(BF16)
    fw = w_in_fox[0]
    fox_w = jnp.concatenate([fw[:, 0:3 * D], fw[:, 3 * D + Hf:4 * D + Hf], _pad_cols(fw[:, 3 * D:3 * D + Hf], SMALL_W)],
                            axis=-1).astype(BF16)
    mu_full = jnp.concatenate([jnp.zeros((5 * M_WIDTH + 2 * M_HEADS,), F32), r_mu[0]])
    mup = _split_mix(mu_full)
    mu4_flat = jnp.concatenate([mup["rr"], mup["rk"], mup["rv"], mup["rz"]])[None]
    mu2_flat = _small_mix(mup)[None, 0:2 * LANES]
    mu4 = _pad_cols(jnp.stack([mup["rr"], mup["rk"], mup["rv"], mup["rz"]]).T, 8).T
    mu2 = _pad_cols(mu2_flat.reshape(2, LANES).T, 8).T
    r_par = jnp.stack([r_w0[0], r_a0[0], r_k_k[0], r_k_a[0], r_r_k[0], r_gn_w[0], r_gn_b[0], jnp.zeros((Wr,), F32)])
    pad_rows = lambda a: jnp.pad(a, ((0, LANES - a.shape[0]), (0, 0)))
    Wt = dict(mix_w=mix_w, fox_w=fox_w, norm_w=norm_w, final_norm=final_norm,
              out_mix_w=w_out_mix[0].astype(BF16), out_fox_w=w_out_fox[0].astype(BF16),
              m_conv_w=m_conv_w[0], m_conv_b=m_conv_b[0], m_b_i=m_b_i[0], m_b_f=m_b_f[0], m_norm=m_norm[0],
              mu4=mu4, mu2=mu2, mu4_flat=mu4_flat, mu2_flat=mu2_flat, r_par=r_par,
              w2p=pad_rows(r_w2[0]), a2p=pad_rows(r_a2[0]),
              f_q_norm=f_q_norm[0], f_k_norm=f_k_norm[0], f_b_f_pad=_pad_cols(f_b_f[0][None], LANES),
              ple_proj=ple_proj.astype(BF16), ple_gate_w=ple_gate_w.astype(BF16), ple_norm=ple_norm)

    def attn_prompt(qs, kn, vv, lf, B, T):
        lfT = lf.reshape(B, T, Hf).transpose(0, 2, 1)
        c = cumsum_lanes(lfT)
        t = FOX_BLOCK if T % FOX_BLOCK == 0 else T
        return fox_prompt_attn(*qs, c.reshape(B, Hf, T // t, 1, t), B, T, t)

    def attn_sample(qs, kn, vv, lf, B, T):
        r3 = lambda t: t.reshape(B, Hf, F_HEAD_DIM)
        lfb = jnp.broadcast_to(lf.reshape(B, Hf, 1), (B, Hf, F_HEAD_DIM))
        o = fox_decode_attn(r3(qs[0]), r3(kn), r3(vv), lfb, cache_k[0], cache_v[0], cache_lf[0], page_table)
        return o.reshape(B, D)

    Bp, Tp, _ = x_prompt.shape
    y_p, (k_p, v_p, lf_p), mix_p = _trunk(x_prompt, p_prompt.reshape(p_prompt.shape[0], Bp * Tp, -1), None,
                                          attn_prompt, Wt, 512)
    Bs, Ts, _ = x_sample.shape
    state = (state_mlstm_conv[0], state_mlstm_C[0], state_mlstm_n[0], state_mlstm_m[0], state_rwkv_S[0],
             state_rwkv_shift[0])
    y_s, (k_s, v_s, lf_s), mix_s = _trunk(x_sample, p_sample.reshape(p_sample.shape[0], Bs * Ts, -1), state,
                                          attn_sample, Wt, Bs * Ts)
    return (y_p, y_s, k_p, v_p, lf_p) + mix_p + (k_s, v_s, lf_s) + mix_s
```

```python
import functools

import jax
import jax.numpy as jnp
from jax import lax
from jax.experimental import pallas as pl
from jax.experimental.pallas import tpu as pltpu

F32 = jnp.float32
BF16 = jnp.bfloat16
HI = lax.Precision.HIGHEST

NORM_EPS = 1e-6
RWKV_GN_EPS = 64e-5
M_HEADS = 8
M_HEAD_DIM = 256
M_WIDTH = M_HEADS * M_HEAD_DIM
CONV_WIDTH = 4
M_CHUNK = 128
M_STEP_HEADS = 4
R_HEAD_DIM = 64
R_LORA = 64
R_CHUNK = 64
R_STEP_TOKENS = 512
F_HEAD_DIM = 128
DEC_PAGES_PER_STEP = 8
LOG2E = 1.4426950408889634
FOX_BLOCK = 512
LANES = 128
SMALL_W = 512
VMEM_LIMIT = 48 * 1024 * 1024
W_BLOCK_BUDGET = 40 * 1024 * 1024


def _cp(sem, vmem=VMEM_LIMIT):
    return pltpu.CompilerParams(dimension_semantics=sem, vmem_limit_bytes=vmem)


def _dot(a, b, prec=None):
    return jnp.dot(a, b, preferred_element_type=F32, precision=prec)


def _dot_nt(a, b, prec=None):
    return lax.dot_general(a, b, (((1,), (1,)), ((), ())), preferred_element_type=F32, precision=prec)


def _dot_tn(a, b, prec=None):
    return lax.dot_general(a, b, (((0,), (0,)), ((), ())), preferred_element_type=F32, precision=prec)


def _split_bf16(x):
    hi = x.astype(BF16)
    return hi, (x - hi.astype(F32)).astype(BF16)


def _lhs3(x, axis):
    hi, lo = _split_bf16(x)
    return jnp.concatenate([hi, hi, lo], axis=axis)


def _rhs3(x, axis):
    hi, lo = _split_bf16(x)
    return jnp.concatenate([hi, lo, hi], axis=axis)


def _dot3(a, b):
    return _dot(_lhs3(a, 1), _rhs3(b, 0))


def _dot3_nt(a, b):
    return _dot_nt(_lhs3(a, 1), _rhs3(b, 1))


def _dot3_tn(a, b):
    return _dot_tn(_lhs3(a, 0), _rhs3(b, 0))


def _softplus(y):
    return jnp.maximum(y, 0.0) + jnp.log1p(jnp.exp(-jnp.abs(y)))


def _log_sigmoid(x):
    return -_softplus(-x)


def _silu(x):
    return x * jax.nn.sigmoid(x)


def _eye(n):
    return (lax.broadcasted_iota(jnp.int32, (n, n), 0) == lax.broadcasted_iota(jnp.int32, (n, n), 1)).astype(F32)


def _rms_bf16(x, g):
    return (x * lax.rsqrt(jnp.mean(x * x, axis=-1, keepdims=True) + NORM_EPS) * g).astype(BF16)


def _norm_cast_kernel(x_ref, g_ref, o_ref):
    o_ref[...] = _rms_bf16(x_ref[...], g_ref[...])


def norm_cast(x, g, tm):
    M, K = x.shape
    return pl.pallas_call(
        _norm_cast_kernel,
        grid=(M // tm,),
        in_specs=[pl.BlockSpec((tm, K), lambda i: (i, 0)), pl.BlockSpec((1, K), lambda i: (0, 0))],
        out_specs=pl.BlockSpec((tm, K), lambda i: (i, 0)),
        out_shape=jax.ShapeDtypeStruct((M, K), BF16),
        compiler_params=_cp(("parallel",)),
        name="norm_cast",
    )(x, g.reshape(1, K))


def _col_tile(N, K, M):
    if M >= 512:
        return 512
    fits = [d * LANES for d in range(1, N // LANES + 1)
            if (N // LANES) % d == 0 and 2 * 2 * K * d * LANES <= W_BLOCK_BUDGET]
    return max(fits + [512])


def _mm_kernel(x_ref, w_ref, o_ref):
    o_ref[...] = _dot(x_ref[...], w_ref[...])


def matmul_bf16(x, w, tm, tn):
    M, K = x.shape
    N = w.shape[1]
    return pl.pallas_call(
        _mm_kernel,
        grid=(M // tm, N // tn),
        in_specs=[pl.BlockSpec((tm, K), lambda i, j: (i, 0)),
                  pl.BlockSpec((K, tn), lambda i, j: (0, j))],
        out_specs=pl.BlockSpec((tm, tn), lambda i, j: (i, j)),
        out_shape=jax.ShapeDtypeStruct((M, N), F32),
        compiler_params=_cp(("parallel", "parallel")),
        name="in_proj",
    )(x, w)


def _out_mix_kernel(a_ref, b_ref, wa_ref, wb_ref, res_ref, o_ref):
    o_ref[...] = res_ref[...] + (_dot(a_ref[...].astype(BF16), wa_ref[...]) + _dot(b_ref[...].astype(BF16), wb_ref[...]))


def out_mix(a, b, w, res, tm):
    M, K = a.shape
    N = w.shape[1]
    rows = lambda n: pl.BlockSpec((tm, n), lambda i: (i, 0))
    return pl.pallas_call(
        _out_mix_kernel,
        grid=(M // tm,),
        in_specs=[rows(K), rows(K),
                  pl.BlockSpec((K, N), lambda i: (0, 0)),
                  pl.BlockSpec((K, N), lambda i: (1, 0)),
                  rows(N)],
        out_specs=rows(N),
        out_shape=jax.ShapeDtypeStruct((M, N), F32),
        compiler_params=_cp(("parallel",)),
        name="out_mix",
    )(a, b, w, w, res)


def _out_fox_kernel(o_in_ref, z_ref, w_ref, res_ref, o_ref):
    g = (o_in_ref[...] * _silu(z_ref[...])).astype(BF16)
    o_ref[...] = res_ref[...] + _dot(g, w_ref[...])


def out_fox(o, proj, z_blk, w, res, tm):
    M, K = o.shape
    N = w.shape[1]
    return pl.pallas_call(
        _out_fox_kernel,
        grid=(M // tm,),
        in_specs=[pl.BlockSpec((tm, K), lambda i: (i, 0)),
                  pl.BlockSpec((tm, K), lambda i: (i, z_blk)),
                  pl.BlockSpec((K, N), lambda i: (0, 0)),
                  pl.BlockSpec((tm, N), lambda i: (i, 0))],
        out_specs=pl.BlockSpec((tm, N), lambda i: (i, 0)),
        out_shape=jax.ShapeDtypeStruct((M, N), F32),
        compiler_params=_cp(("parallel",)),
        name="out_fox",
    )(o, proj, w, res)


def _ple_kernel(h_ref, g_ref, wg_ref, p_ref, pw_ref, ng_ref, *outs, final):
    x = h_ref[...]
    gate = jax.nn.sigmoid(_dot(_rms_bf16(x, g_ref[...]), wg_ref[...]))
    e = _dot(p_ref[...].astype(BF16), pw_ref[...])
    out = x + gate * e
    if final:
        (y_ref,) = outs
        y_ref[...] = out * lax.rsqrt(jnp.mean(out * out, axis=-1, keepdims=True) + NORM_EPS) * ng_ref[...]
    else:
        o_ref, xn_ref = outs
        o_ref[...] = out
        xn_ref[...] = _rms_bf16(out, ng_ref[...])


def ple_add(h, g, wg, p, pw, next_g, tm, final):
    M, K = h.shape
    N = wg.shape[1]
    P = p.shape[1]
    rows = pl.BlockSpec((tm, N), lambda i: (i, 0))
    vec = pl.BlockSpec((1, K), lambda i: (0, 0))
    f32_rows = jax.ShapeDtypeStruct((M, N), F32)
    return pl.pallas_call(
        functools.partial(_ple_kernel, final=final),
        grid=(M // tm,),
        in_specs=[pl.BlockSpec((tm, K), lambda i: (i, 0)), vec,
                  pl.BlockSpec((K, N), lambda i: (0, 0)),
                  pl.BlockSpec((tm, P), lambda i: (i, 0)),
                  pl.BlockSpec((P, N), lambda i: (0, 0)), vec],
        out_specs=rows if final else [rows, rows],
        out_shape=f32_rows if final else [f32_rows, jax.ShapeDtypeStruct((M, N), BF16)],
        compiler_params=_cp(("parallel",)),
        name="ple_add",
    )(h, g.reshape(1, K), wg, p, pw, next_g.reshape(1, N))


def _mlstm_chunk_kernel(qp_ref, kp_ref, v_ref, mo_ref, mz_ref, g_ref, gb_ref, cwq_ref, cwk_ref, mn_ref,
                        out_ref, C_out, n_out, m_out,
                        C_s, n_s, m_s, qbuf, kbuf):
    c = pl.program_id(2)
    nc = pl.num_programs(2)
    L = qp_ref.shape[0]
    HH = C_s.shape[0]
    Dh = qp_ref.shape[1] // HH
    heads = range(HH)
    hs = [slice(h * Dh, (h + 1) * Dh) for h in heads]

    @pl.when(c == 0)
    def _():
        C_s[...] = jnp.zeros(C_s.shape, F32)
        n_s[...] = jnp.zeros(n_s.shape, F32)
        m_s[...] = jnp.zeros(m_s.shape, F32)
        qbuf[0:8, :] = jnp.zeros((8, HH * Dh), F32)
        kbuf[0:8, :] = jnp.zeros((8, HH * Dh), F32)

    qbuf[8:8 + L, :] = qp_ref[...]
    kbuf[8:8 + L, :] = kp_ref[...]

    def conv(buf, w_ref):
        acc = w_ref[CONV_WIDTH:CONV_WIDTH + 1, :]
        for j in range(CONV_WIDTH):
            acc = acc + buf[8 - (CONV_WIDTH - 1) + j: 8 - (CONV_WIDTH - 1) + j + L, :] * w_ref[j:j + 1, :]
        return acc

    q = _silu(conv(qbuf, cwq_ref))
    k = _silu(conv(kbuf, cwk_ref)) * (Dh ** -0.5)
    qbuf[0:8, :] = qbuf[L:L + 8, :]
    kbuf[0:8, :] = kbuf[L:L + 8, :]
    v = v_ref[...]

    g = [g_ref[h] + gb_ref[h] for h in heads]
    li_r = [g[h][0:1, :] for h in heads]
    lf_r = [_log_sigmoid(g[h][1:2, :]) for h in heads]
    row = lax.broadcasted_iota(jnp.int32, (L, L), 0)
    col = lax.broadcasted_iota(jnp.int32, (L, L), 1)
    tri = row >= col
    diag = row == col
    b_c = [jnp.sum(jnp.where(tri, lf_r[h], 0.0), axis=1, keepdims=True) for h in heads]
    b_r = [jnp.sum(jnp.where(diag, b_c[h], 0.0), axis=0, keepdims=True) for h in heads]
    li_c = [jnp.sum(jnp.where(diag, li_r[h], 0.0), axis=1, keepdims=True) for h in heads]
    m_prev = [m_s[h] for h in heads]
    dmat = [jnp.where(tri, b_c[h] - b_r[h] + li_r[h], -jnp.inf) for h in heads]
    inter = [b_c[h] + m_prev[h] for h in heads]
    m_t = [jnp.maximum(inter[h], jnp.max(dmat[h], axis=1, keepdims=True)) for h in heads]
    w_intra = [jnp.exp(dmat[h] - m_t[h]) for h in heads]
    w_inter = [jnp.exp(inter[h] - m_t[h]) for h in heads]
    qb = [q[:, hs[h]].astype(BF16) for h in heads]
    kb = [k[:, hs[h]].astype(BF16) for h in heads]
    vb = [v[:, hs[h]].astype(BF16) for h in heads]
    C = [C_s[h] for h in heads]
    n_row = [n_s[h] for h in heads]
    s = [_dot_nt(qb[h], kb[h]) * w_intra[h] for h in heads]
    qC = [_dot_nt(qb[h], C[h].astype(BF16)) for h in heads]
    num = [_dot(s[h].astype(BF16), vb[h]) + w_inter[h] * qC[h] for h in heads]
    den = [jnp.sum(s[h], axis=1, keepdims=True)
           + w_inter[h] * jnp.sum(q[:, hs[h]] * n_row[h], axis=1, keepdims=True) for h in heads]
    hval = [num[h] / jnp.maximum(jnp.abs(den[h]), jnp.exp(-m_t[h])) for h in heads]
    m_new = [m_t[h][L - 1:L, :] for h in heads]
    bL = [b_c[h][L - 1:L, :] for h in heads]
    cd = [jnp.exp(bL[h] + m_prev[h] - m_new[h]) for h in heads]
    w_c = [jnp.exp(bL[h] - b_c[h] + li_c[h] - m_new[h]) for h in heads]
    w_r = [jnp.exp(bL[h] - b_r[h] + li_r[h] - m_new[h]) for h in heads]
    C_new = [cd[h] * C[h] + _dot_tn((v[:, hs[h]] * w_c[h]).astype(BF16), kb[h]) for h in heads]
    n_new = [cd[h] * n_row[h] + _dot(w_r[h].astype(BF16), kb[h]) for h in heads]
    hn = [hval[h] * lax.rsqrt(jnp.mean(hval[h] * hval[h], axis=-1, keepdims=True) + NORM_EPS) for h in heads]
    for h in heads:
        C_s[h] = C_new[h]
        n_s[h] = n_new[h]
        m_s[h] = m_new[h]
    out_ref[...] = jnp.concatenate(hn, axis=1) * mn_ref[...] * jax.nn.sigmoid(mo_ref[...]) * _silu(mz_ref[...])

    @pl.when(c == nc - 1)
    def _():
        for h in heads:
            C_out[h] = C_new[h]
            n_out[h] = n_new[h]
            m_out[h] = m_new[h]


def mlstm_prompt(proj, B, S, gi, gf, conv_w, conv_b, b_i, b_f, m_norm):
    H, Dh, L = M_HEADS, M_HEAD_DIM, M_CHUNK
    nc = S // L
    M = B * S
    g2 = jnp.stack([gi, gf], axis=-1).reshape(B, nc, L, H, 2).transpose(0, 3, 1, 4, 2)
    gb = jnp.stack([b_i, b_f], axis=-1).reshape(H, 2, 1)
    cw = jnp.concatenate([conv_w, conv_b.reshape(1, -1)], axis=0)
    HH = M_STEP_HEADS
    Wb = HH * Dh
    pblk = lambda off: pl.BlockSpec((L, Wb), lambda b, h, c: (b * nc + c, off + h))
    hb = M_WIDTH // Wb
    outs = pl.pallas_call(
        _mlstm_chunk_kernel,
        grid=(B, H // HH, nc),
        in_specs=[pblk(0), pblk(hb), pblk(2 * hb), pblk(3 * hb), pblk(4 * hb),
                  pl.BlockSpec((None, HH, None, 2, L), lambda b, h, c: (b, h, c, 0, 0)),
                  pl.BlockSpec((HH, 2, 1), lambda b, h, c: (h, 0, 0)),
                  pl.BlockSpec((CONV_WIDTH + 1, Wb), lambda b, h, c: (0, h)),
                  pl.BlockSpec((CONV_WIDTH + 1, Wb), lambda b, h, c: (0, hb + h)),
                  pl.BlockSpec((1, Wb), lambda b, h, c: (0, h))],
        out_specs=[pl.BlockSpec((L, Wb), lambda b, h, c: (b * nc + c, h)),
                   pl.BlockSpec((None, HH, Dh, Dh), lambda b, h, c: (b, h, 0, 0)),
                   pl.BlockSpec((None, HH, 1, Dh), lambda b, h, c: (b, h, 0, 0)),
                   pl.BlockSpec((None, HH, 1, 1), lambda b, h, c: (b, h, 0, 0))],
        out_shape=[jax.ShapeDtypeStruct((M, M_WIDTH), F32),
                   jax.ShapeDtypeStruct((B, H, Dh, Dh), F32),
                   jax.ShapeDtypeStruct((B, H, 1, Dh), F32),
                   jax.ShapeDtypeStruct((B, H, 1, 1), F32)],
        scratch_shapes=[pltpu.VMEM((HH, Dh, Dh), F32), pltpu.VMEM((HH, 1, Dh), F32), pltpu.VMEM((HH, 1, 1), F32),
                        pltpu.VMEM((L + 8, Wb), F32), pltpu.VMEM((L + 8, Wb), F32)],
        compiler_params=_cp(("parallel", "parallel", "arbitrary")),
        name="mlstm_prompt",
    )(proj, proj, proj, proj, proj, g2, gb, cw, cw, m_norm.reshape(1, -1))
    return outs


def _mlstm_step_kernel(x_ref, conv0_ref, cw_ref, cb_ref, v_ref, gi_ref, gf_ref, bi_ref, bf_ref,
                       mo_ref, mz_ref, mn_ref, C_ref, n_ref, m_ref,
                       out_ref, conv_out, C_out, n_out, m_out):
    H, Dh = v_ref.shape
    x = x_ref[...]
    qk = cb_ref[...]
    for j in range(CONV_WIDTH - 1):
        qk = qk + conv0_ref[j] * cw_ref[j]
    qk = qk + x * cw_ref[CONV_WIDTH - 1]
    for j in range(CONV_WIDTH - 2):
        conv_out[j] = conv0_ref[j + 1]
    conv_out[CONV_WIDTH - 2] = x
    qk = _silu(qk)
    q = qk[0:H]
    k = qk[H:2 * H] * (Dh ** -0.5)
    v = v_ref[...]
    li = gi_ref[...] + bi_ref[...]
    lf = _log_sigmoid(gf_ref[...] + bf_ref[...])
    m0 = m_ref[...]
    inter = lf + m0
    m_t = jnp.maximum(inter, li)
    w_intra = jnp.exp(li - m_t)
    w_inter = jnp.exp(inter - m_t)
    s = jnp.sum(q * k, axis=-1, keepdims=True) * w_intra
    n0 = n_ref[...]
    hrow = lax.broadcasted_iota(jnp.int32, (H, Dh), 0)
    cq = jnp.zeros((H, Dh), F32)
    vT = _dot_nt(_eye(Dh), v * w_intra, HI)
    Cs = [C_ref[h] for h in range(H)]
    cqs = [_dot_nt(q, Cs[h], HI) for h in range(H)]
    for h in range(H):
        cq = jnp.where(hrow == h, cqs[h], cq)
        C_out[h] = w_inter[h:h + 1, :] * Cs[h] + vT[:, h:h + 1] * k[h:h + 1, :]
    num = s * v + w_inter * cq
    den = s + w_inter * jnp.sum(n0 * q, axis=-1, keepdims=True)
    hval = num / jnp.maximum(jnp.abs(den), jnp.exp(-m_t))
    n_out[...] = w_inter * n0 + w_intra * k
    m_out[...] = m_t
    hn = hval * lax.rsqrt(jnp.mean(hval * hval, axis=-1, keepdims=True) + NORM_EPS) * mn_ref[...]
    out_ref[...] = hn * jax.nn.sigmoid(mo_ref[...]) * _silu(mz_ref[...])


def mlstm_step(x16, conv0, conv_w, conv_b, v, gi, gf, b_i, b_f, mo, mz, m_norm, C0, n0, m0):
    B = x16.shape[0]
    H, Dh = M_HEADS, M_HEAD_DIM
    W = CONV_WIDTH
    per_b = lambda *shape: pl.BlockSpec((None,) + shape, lambda b: (b,) + (0,) * len(shape))
    const = lambda *shape: pl.BlockSpec(shape, lambda b: (0,) * len(shape))
    return pl.pallas_call(
        _mlstm_step_kernel,
        grid=(B,),
        in_specs=[per_b(2 * H, Dh), per_b(W - 1, 2 * H, Dh), const(W, 2 * H, Dh), const(2 * H, Dh),
                  per_b(H, Dh), per_b(H, 1), per_b(H, 1), const(H, 1), const(H, 1),
                  per_b(H, Dh), per_b(H, Dh), const(H, Dh),
                  per_b(H, Dh, Dh), per_b(H, Dh), per_b(H, 1)],
        out_specs=[per_b(H, Dh), per_b(W - 1, 2 * H, Dh), per_b(H, Dh, Dh), per_b(H, Dh), per_b(H, 1)],
        out_shape=[jax.ShapeDtypeStruct((B, H, Dh), F32),
                   jax.ShapeDtypeStruct((B, W - 1, 2 * H, Dh), F32),
                   jax.ShapeDtypeStruct((B, H, Dh, Dh), F32),
                   jax.ShapeDtypeStruct((B, H, Dh), F32),
                   jax.ShapeDtypeStruct((B, H, 1), F32)],
        compiler_params=_cp(("parallel",)),
        name="mlstm_step",
    )(x16, conv0, conv_w, conv_b, v, gi, gf, b_i, b_f, mo, mz, m_norm, C0, n0, m0)


def _rwkv_chunk_kernel(rr_ref, rk_ref, rv_ref, rz_ref, rw_ref, ra_ref, mu4_ref, mu2_ref, par_ref,
                       w2_ref, a2_ref, out_ref, S_out, S_s, carry_s):
    c = pl.program_id(2)
    nc = pl.num_programs(2)
    TB = rr_ref.shape[0]
    W = rr_ref.shape[1]
    N = R_HEAD_DIM
    L = R_CHUNK

    @pl.when(c == 0)
    def _():
        S_s[...] = jnp.zeros(S_s.shape, F32)
        carry_s[...] = jnp.zeros(carry_s.shape, F32)

    row = lax.broadcasted_iota(jnp.int32, (TB, W), 0)
    lo = lax.broadcasted_iota(jnp.int32, (TB, W), 1) < N

    def shift(x, idx, mu):
        prev = jnp.where(row == 0, carry_s[idx:idx + 1, :], pltpu.roll(x, 1, 0))
        carry_s[idx:idx + 1, :] = x[TB - 1:TB, :]
        return x + mu * (prev - x)

    r = shift(rr_ref[...], 0, mu4_ref[0:1, :])
    k = shift(rk_ref[...], 1, mu4_ref[1:2, :])
    v = shift(rv_ref[...], 2, mu4_ref[2:3, :])
    z = shift(rz_ref[...], 3, mu4_ref[3:4, :])
    xw = shift(rw_ref[...], 4, mu2_ref[0:1, :])
    xa = shift(ra_ref[...], 5, mu2_ref[1:2, :])
    w0, a0 = par_ref[0:1, :], par_ref[1:2, :]
    k_k, k_a, r_k = par_ref[2:3, :], par_ref[3:4, :], par_ref[4:5, :]
    gn_w, gn_b = par_ref[5:6, :], par_ref[6:7, :]

    def seg_sum(x):
        s0 = jnp.sum(jnp.where(lo, x, 0.0), axis=1, keepdims=True)
        s1 = jnp.sum(jnp.where(lo, 0.0, x), axis=1, keepdims=True)
        return jnp.where(lo, s0, s1)

    w_log = -_softplus(-(w0 + _dot3(jnp.tanh(xw), w2_ref[...]))) - 0.5
    logw = -jnp.exp(w_log)
    a = jax.nn.sigmoid(a0 + _dot3(xa, a2_ref[...]))
    kk = k * k_k
    kk = kk * lax.rsqrt(jnp.maximum(seg_sum(kk * kk), 1e-24))
    k2 = k * (1.0 + (a - 1.0) * k_a)
    alpha = -kk
    beta = kk * a

    tr = lax.broadcasted_iota(jnp.int32, (TB, TB), 0)
    tc = lax.broadcasted_iota(jnp.int32, (TB, TB), 1)
    in_chunk = jnp.logical_and(tc <= tr, tc >= tr - jnp.bitwise_and(tr, L - 1))
    l_hi = logw.astype(BF16)
    l_r = logw - l_hi.astype(F32)
    l_mid = l_r.astype(BF16)
    l_lo = (l_r - l_mid.astype(F32)).astype(BF16)
    b3 = _dot(jnp.where(in_chunk, 1.0, 0.0).astype(BF16), jnp.concatenate([l_hi, l_mid, l_lo], axis=1))
    b = (b3[:, 0:W] + b3[:, W:2 * W]) + b3[:, 2 * W:3 * W]

    lo_c = lax.broadcasted_iota(jnp.int32, (L, W), 1) < N

    def stack(x):
        return jnp.concatenate([jnp.where(lo_c, x, 0.0), jnp.where(lo_c, 0.0, x)], axis=0)

    ri = lax.broadcasted_iota(jnp.int32, (2 * L, 2 * L), 0)
    ci = lax.broadcasted_iota(jnp.int32, (2 * L, 2 * L), 1)
    same = jnp.where(ri >= L, 1, 0) == jnp.where(ci >= L, 1, 0)
    strict = jnp.logical_and(same, ci < ri)
    incl = jnp.logical_and(same, ci <= ri)
    eye = jnp.where(ri == ci, 1.0, 0.0)

    chunks = range(TB // L)
    sls = [slice(ch * L, (ch + 1) * L) for ch in chunks]
    bLs = [b[sl][L - 1:L, :] for sl in sls]
    a_s = [stack(alpha[sl] * jnp.exp(b[sl] - logw[sl])) for sl in sls]
    r_s = [stack(r[sl] * jnp.exp(b[sl])) for sl in sls]
    Ys_ = [jnp.concatenate([stack(beta[sl] * jnp.exp(-b[sl])), stack(k2[sl] * jnp.exp(-b[sl]))], axis=0) for sl in sls]
    bh_s = [stack(beta[sl] * jnp.exp(bL - b[sl])) for sl, bL in zip(sls, bLs)]
    kh_s = [stack(k2[sl] * jnp.exp(bL - b[sl])) for sl, bL in zip(sls, bLs)]
    v_s = [stack(v[sl]) for sl in sls]
    Gs = [_dot3_nt(a_s[i], Ys_[i]) for i in chunks]
    Go = [_dot_nt(r_s[i].astype(BF16), Ys_[i].astype(BF16)) for i in chunks]
    A = [jnp.where(strict, Gs[i][:, 0:2 * L], 0.0) for i in chunks]
    Bm = [jnp.where(strict, Gs[i][:, 2 * L:4 * L], 0.0) for i in chunks]
    P = [jnp.where(incl, Go[i][:, 0:2 * L], 0.0).astype(BF16) for i in chunks]
    Q = [jnp.where(incl, Go[i][:, 2 * L:4 * L], 0.0).astype(BF16) for i in chunks]
    BV = [_dot3(Bm[i], v_s[i]) for i in chunks]
    QV = [_dot(Q[i], v_s[i].astype(BF16)) for i in chunks]
    T = [eye + A[i] for i in chunks]
    Ap = [_dot3(A[i], A[i]) for i in chunks]
    for _ in range(4):
        R = [_dot(_lhs3(Ap[i], 1), _rhs3(jnp.concatenate([Ap[i], T[i]], axis=1), 0)) for i in chunks]
        T = [T[i] + R[i][:, 2 * L:4 * L] for i in chunks]
        Ap = [R[i][:, 0:2 * L] for i in chunks]
    T = [T[i] + _dot3(Ap[i], T[i]) for i in chunks]
    TXB = [_dot3(T[i], jnp.concatenate([a_s[i], BV[i]], axis=1)) for i in chunks]
    PTX = [_dot(P[i], TXB[i].astype(BF16)) for i in chunks]
    MN = [_dot3_tn(TXB[i], bh_s[i]) for i in chunks]
    VK = [_dot3_tn(v_s[i], kh_s[i]) for i in chunks]
    pre = [((r_s[i] + PTX[i][:, 0:W]).astype(BF16), QV[i] + PTX[i][:, W:2 * W],
            MN[i][0:W], MN[i][W:2 * W] + VK[i], jnp.exp(bLs[i])) for i in chunks]

    S = S_s[...]
    ys = []
    for Rp, Y0, Mc, Nc, gL in pre:
        Ys = _dot_nt(Rp, S.astype(BF16)) + Y0
        ys.append(Ys[0:L] + Ys[L:2 * L])
        S = (S * gL + _dot3(S, Mc)) + Nc
    S_new = S
    S_s[...] = S_new
    y = jnp.concatenate(ys, axis=0)

    mu_y = seg_sum(y) * (1.0 / N)
    d = y - mu_y
    var = seg_sum(d * d) * (1.0 / N)
    yn = d * lax.rsqrt(var + RWKV_GN_EPS) * gn_w + gn_b
    yn = yn + seg_sum(r * k2 * r_k) * v
    out_ref[...] = yn * _silu(z)

    @pl.when(c == nc - 1)
    def _():
        S_out[...] = S_new


def rwkv_prompt(proj, B, S, mu4, mu2, par, w2p, a2p, r_off):
    L = R_STEP_TOKENS
    W = 2 * R_HEAD_DIM
    nc = S // L
    M = B * S
    npair = mu4.shape[1] // W
    blk = lambda off: pl.BlockSpec((L, W), lambda b, p, c: (b * nc + c, off + p))
    cblk = lambda off: pl.BlockSpec((L, W), lambda b, p, c: (b * nc + c, off))
    return pl.pallas_call(
        _rwkv_chunk_kernel,
        grid=(B, npair, nc),
        in_specs=[blk(r_off), blk(r_off + npair), blk(r_off + 2 * npair), blk(r_off + 3 * npair),
                  cblk(r_off + 4 * npair), cblk(r_off + 4 * npair + 1),
                  pl.BlockSpec((8, W), lambda b, p, c: (0, p)),
                  pl.BlockSpec((8, W), lambda b, p, c: (0, 0)),
                  pl.BlockSpec((8, W), lambda b, p, c: (0, p)),
                  pl.BlockSpec((W, W), lambda b, p, c: (0, p)),
                  pl.BlockSpec((W, W), lambda b, p, c: (0, p))],
        out_specs=[pl.BlockSpec((L, W), lambda b, p, c: (b * nc + c, p)),
                   pl.BlockSpec((None, None, W, W), lambda b, p, c: (b, p, 0, 0))],
        out_shape=[jax.ShapeDtypeStruct((M, npair * W), F32),
                   jax.ShapeDtypeStruct((B, npair, W, W), F32)],
        scratch_shapes=[pltpu.VMEM((W, W), F32), pltpu.VMEM((8, W), F32)],
        compiler_params=_cp(("parallel", "parallel", "arbitrary")),
        name="rwkv_prompt",
    )(proj, proj, proj, proj, proj, proj, mu4, mu2, par, w2p, a2p)


def _rwkv_prep_kernel(rr_ref, rk_ref, rv_ref, rz_ref, rw_ref, ra_ref, sh4_ref, sh2_ref, mu4_ref, mu2_ref,
                      w0_ref, a0_ref, w2_ref, a2_ref, r_o, k_o, v_o, z_o, w_o, a_o):
    Wd = rr_ref.shape[1]
    sh = lambda x, prev, mu: x + mu * (prev - x)
    r_o[...] = sh(rr_ref[...], sh4_ref[:, 0:Wd], mu4_ref[:, 0:Wd])
    k_o[...] = sh(rk_ref[...], sh4_ref[:, Wd:2 * Wd], mu4_ref[:, Wd:2 * Wd])
    v_o[...] = sh(rv_ref[...], sh4_ref[:, 2 * Wd:3 * Wd], mu4_ref[:, 2 * Wd:3 * Wd])
    z_o[...] = sh(rz_ref[...], sh4_ref[:, 3 * Wd:4 * Wd], mu4_ref[:, 3 * Wd:4 * Wd])
    xw = sh(rw_ref[...], sh2_ref[:, 0:LANES], mu2_ref[:, 0:LANES])
    xa = sh(ra_ref[...], sh2_ref[:, LANES:2 * LANES], mu2_ref[:, LANES:2 * LANES])
    w_log = -_softplus(-(w0_ref[...] + _dot(jnp.tanh(xw), w2_ref[...], HI))) - 0.5
    w_o[...] = jnp.exp(-jnp.exp(w_log))
    a_o[...] = jax.nn.sigmoid(a0_ref[...] + _dot(xa, a2_ref[...], HI))


def rwkv_prep(proj, sh4, sh2, mu4, mu2, w0, a0, w2p, a2p, r_blk, s_blk):
    B = proj.shape[0]
    Wd = w0.shape[1]
    full = lambda a: pl.BlockSpec(a.shape, lambda i: (0,) * a.ndim)
    return pl.pallas_call(
        _rwkv_prep_kernel,
        grid=(1,),
        in_specs=[pl.BlockSpec((B, Wd), lambda i: (0, r_blk)), pl.BlockSpec((B, Wd), lambda i: (0, r_blk + 1)),
                  pl.BlockSpec((B, Wd), lambda i: (0, r_blk + 2)), pl.BlockSpec((B, Wd), lambda i: (0, r_blk + 3)),
                  pl.BlockSpec((B, LANES), lambda i: (0, s_blk)), pl.BlockSpec((B, LANES), lambda i: (0, s_blk + 1)),
                  full(sh4), full(sh2), full(mu4), full(mu2), full(w0), full(a0), full(w2p), full(a2p)],
        out_specs=[pl.BlockSpec((B, Wd), lambda i: (0, 0))] * 6,
        out_shape=[jax.ShapeDtypeStruct((B, Wd), F32)] * 6,
        compiler_params=_cp(("arbitrary",)),
        name="rwkv_prep",
    )(proj, proj, proj, proj, proj, proj, sh4, sh2, mu4, mu2, w0, a0, w2p, a2p)


def _rwkv_step_kernel(r_ref, k_ref, v_ref, z_ref, w_ref, a_ref, par_ref, S_ref, out_ref, S_out):
    H, N = r_ref.shape
    r, k, v, z, w, a = (t[...] for t in (r_ref, k_ref, v_ref, z_ref, w_ref, a_ref))
    k_k, k_a, r_k, gn_w, gn_b = (par_ref[i] for i in range(5))
    kk = k * k_k
    kk = kk * lax.rsqrt(jnp.maximum(jnp.sum(kk * kk, axis=-1, keepdims=True), 1e-24))
    k2 = k * (1.0 + (a - 1.0) * k_a)
    alpha = -kk
    beta = kk * a
    vT = _dot_nt(_eye(N), v, HI)
    lane_h = lax.broadcasted_iota(jnp.int32, (N, H), 1)
    heads = range(H)
    S = [S_ref[h] for h in heads]
    sa = [jnp.sum(S[h] * alpha[h:h + 1, :], axis=-1, keepdims=True) for h in heads]
    Sn = [S[h] * w[h:h + 1, :] + sa[h] * beta[h:h + 1, :] + vT[:, h:h + 1] * k2[h:h + 1, :] for h in heads]
    ys = [jnp.sum(Sn[h] * r[h:h + 1, :], axis=-1, keepdims=True) for h in heads]
    yT = jnp.zeros((N, H), F32)
    for h in heads:
        S_out[h] = Sn[h]
        yT = jnp.where(lane_h == h, ys[h], yT)
    y = _dot_nt(_eye(H), yT, HI)
    mu_y = jnp.mean(y, axis=-1, keepdims=True)
    d = y - mu_y
    var = jnp.mean(d * d, axis=-1, keepdims=True)
    yn = d * lax.rsqrt(var + RWKV_GN_EPS) * gn_w + gn_b
    yn = yn + jnp.sum(r * k2 * r_k, axis=-1, keepdims=True) * v
    out_ref[...] = yn * _silu(z)


def rwkv_step(r, k, v, z, w, a, par, S0):
    B, H, N = r.shape
    vec = pl.BlockSpec((None, H, N), lambda b: (b, 0, 0))
    mat = pl.BlockSpec((None, H, N, N), lambda b: (b, 0, 0, 0))
    return pl.pallas_call(
        _rwkv_step_kernel,
        grid=(B,),
        in_specs=[vec] * 6 + [pl.BlockSpec(par.shape, lambda b: (0, 0, 0)), mat],
        out_specs=[vec, mat],
        out_shape=[jax.ShapeDtypeStruct((B, H, N), F32), jax.ShapeDtypeStruct((B, H, N, N), F32)],
        compiler_params=_cp(("parallel",)),
        name="rwkv_step",
    )(r, k, v, z, w, a, par, S0)


def _qk_norm_kernel(q_ref, k_ref, v_ref, f_ref, gq_ref, gk_ref, bf_ref, kn_ref, v_out_ref, lf_ref, *q_refs,
                    mxu_copies):
    D = F_HEAD_DIM
    for h in range(q_ref.shape[1] // D):
        sl = slice(h * D, (h + 1) * D)
        x = q_ref[:, sl]
        qn = x * lax.rsqrt(jnp.mean(x * x, axis=-1, keepdims=True) + NORM_EPS) * gq_ref[...]
        y = k_ref[:, sl]
        kn = y * lax.rsqrt(jnp.mean(y * y, axis=-1, keepdims=True) + NORM_EPS) * gk_ref[...]
        v = v_ref[:, sl]
        kn_ref[:, sl] = kn
        v_out_ref[:, sl] = v
        if mxu_copies:
            qb_ref, kb_ref, vb_ref = q_refs
            qb_ref[:, sl] = (qn * (D ** -0.5 * LOG2E)).astype(BF16)
            kb_ref[:, sl] = kn.astype(BF16)
            vb_ref[:, sl] = v.astype(BF16)
        else:
            q_refs[0][:, sl] = qn
    lf_ref[...] = _log_sigmoid(f_ref[...] + bf_ref[...])


def qk_norm(proj, gq, gk, bf_pad, tm, Wd, f_blk, mxu_copies):
    M = proj.shape[0]
    wide = pl.BlockSpec((tm, Wd), lambda i: (i, 0))
    wide_sds = lambda dt: jax.ShapeDtypeStruct((M, Wd), dt)
    extra = [wide_sds(BF16)] * 3 if mxu_copies else [wide_sds(F32)]
    return pl.pallas_call(
        functools.partial(_qk_norm_kernel, mxu_copies=mxu_copies),
        grid=(M // tm,),
        in_specs=[pl.BlockSpec((tm, Wd), lambda i: (i, 0)), pl.BlockSpec((tm, Wd), lambda i: (i, 1)),
                  pl.BlockSpec((tm, Wd), lambda i: (i, 2)),
                  pl.BlockSpec((tm, LANES), lambda i: (i, f_blk)),
                  pl.BlockSpec((1, F_HEAD_DIM), lambda i: (0, 0)), pl.BlockSpec((1, F_HEAD_DIM), lambda i: (0, 0)),
                  pl.BlockSpec((1, LANES), lambda i: (0, 0))],
        out_specs=[wide, wide, pl.BlockSpec((tm, LANES), lambda i: (i, 0))] + [wide] * len(extra),
        out_shape=[wide_sds(F32), wide_sds(F32), jax.ShapeDtypeStruct((M, LANES), F32)] + extra,
        compiler_params=_cp(("parallel",)),
        name="qk_norm",
    )(proj, proj, proj, proj, gq.reshape(1, -1), gk.reshape(1, -1), bf_pad)


def _cumsum_kernel(x_ref, c_ref):
    H, S = x_ref.shape
    U = (lax.broadcasted_iota(jnp.int32, (LANES, LANES), 0) <= lax.broadcasted_iota(jnp.int32, (LANES, LANES), 1)).astype(F32)
    off = jnp.zeros((H, 1), F32)
    for t in range(S // LANES):
        sl = slice(t * LANES, (t + 1) * LANES)
        w = _dot(x_ref[:, sl], U, HI) + off
        c_ref[:, sl] = w
        off = w[:, LANES - 1:LANES]


def cumsum_lanes(x):
    B, H, S = x.shape
    return pl.pallas_call(
        _cumsum_kernel,
        grid=(B,),
        in_specs=[pl.BlockSpec((None, H, S), lambda b: (b, 0, 0))],
        out_specs=pl.BlockSpec((None, H, S), lambda b: (b, 0, 0)),
        out_shape=jax.ShapeDtypeStruct((B, H, S), F32),
        compiler_params=_cp(("parallel",)),
        name="forget_cumsum",
    )(x)


def _fox_flash_kernel(q_ref, k_ref, v_ref, ck_ref, o_ref, m_s, l_s, acc_s):
    i = pl.program_id(2)
    t, D = q_ref.shape
    q = q_ref[...]
    m_s[...] = jnp.full(m_s.shape, -jnp.inf, F32)
    l_s[...] = jnp.zeros(l_s.shape, F32)
    acc_s[...] = jnp.zeros(acc_s.shape, F32)

    def block(j, masked):
        start = pl.multiple_of(j * t, t)
        s = _dot_nt(q, k_ref[pl.ds(start, t), :]) - ck_ref[j] * LOG2E
        if masked:
            row = lax.broadcasted_iota(jnp.int32, (t, t), 0)
            col = lax.broadcasted_iota(jnp.int32, (t, t), 1)
            s = jnp.where(col <= row, s, -jnp.inf)
        m_prev = m_s[...]
        m_new = jnp.maximum(m_prev, jnp.max(s, axis=1, keepdims=True))
        alpha = jnp.exp2(m_prev - m_new)
        p = jnp.exp2(s - jnp.concatenate([m_new] * (t // LANES), axis=1))
        l_s[...] = alpha * l_s[...] + jnp.sum(p, axis=1, keepdims=True)
        acc_s[...] = alpha * acc_s[...] + _dot(p.astype(BF16), v_ref[pl.ds(start, t), :])
        m_s[...] = m_new

    def body(j, carry):
        block(j, False)
        return carry

    lax.fori_loop(0, i, body, 0)
    block(i, True)
    o_ref[...] = acc_s[...] / l_s[...]


def fox_prompt_attn(qb, kb, vb, ck, B, S, t):
    D = F_HEAD_DIM
    H = qb.shape[1] // D
    nb = S // t
    full = pl.BlockSpec((S, D), lambda b, h, i: (b, h))
    return pl.pallas_call(
        _fox_flash_kernel,
        grid=(B, H, nb),
        in_specs=[pl.BlockSpec((t, D), lambda b, h, i: (b * nb + i, h)), full, full,
                  pl.BlockSpec((None, None, nb, 1, t), lambda b, h, i: (b, h, 0, 0, 0))],
        out_specs=pl.BlockSpec((t, D), lambda b, h, i: (b * nb + i, h)),
        out_shape=jax.ShapeDtypeStruct((B * S, H * D), F32),
        scratch_shapes=[pltpu.VMEM((t, LANES), F32), pltpu.VMEM((t, LANES), F32), pltpu.VMEM((t, D), F32)],
        compiler_params=_cp(("parallel", "parallel", "parallel")),
        name="fox_prompt_attn",
    )(qb, kb, vb, ck)


def _fox_decode_kernel(pt_ref, q_ref, kn_ref, vn_ref, lfn_ref, *refs, npg):
    k_refs, v_refs, lf_refs = refs[0:npg], refs[npg:2 * npg], refs[2 * npg:3 * npg]
    o_ref = refs[3 * npg]
    m_s, l_s, acc_s, off_s, cp_s = refs[3 * npg + 1:]
    p = pl.program_id(1)
    H, D = q_ref.shape
    T = k_refs[0].shape[0]
    pages = range(npg)

    @pl.when(p == 0)
    def _():
        m_s[...] = jnp.full(m_s.shape, -jnp.inf, F32)
        l_s[...] = jnp.zeros(l_s.shape, F32)
        acc_s[...] = jnp.zeros(acc_s.shape, F32)
        off_s[...] = jnp.zeros(off_s.shape, F32)
        cp_s[...] = jnp.zeros(cp_s.shape, F32)

    q = q_ref[...] * (D ** -0.5 * LOG2E)
    hmask = (lax.broadcasted_iota(jnp.int32, (H, LANES), 0) == lax.broadcasted_iota(jnp.int32, (H, LANES), 1)).astype(F32)
    hmask2 = hmask * LOG2E
    trow = lax.broadcasted_iota(jnp.int32, (T, LANES), 0)
    for g in pages:
        cp_s[g, :, 0:H] = lf_refs[g][...]
    cps = [cp_s[g] for g in pages]
    sh = 1
    while sh < T:
        cps = [x + jnp.where(trow >= sh, pltpu.roll(x, sh, 0), 0.0) for x in cps]
        sh *= 2
    for g in pages:
        cp_s[g] = cps[g]
    cbm = [jnp.stack([jnp.broadcast_to(cp_s[g, t:t + 1, :], (H, LANES)) for t in range(T)], axis=0) * hmask2[None]
           for g in pages]
    s = [jnp.sum(k_refs[g][...] * q[None] - cbm[g], axis=-1, keepdims=True) for g in pages]
    tot = [jnp.sum(jnp.broadcast_to(cp_s[g, T - 1:T, :], (H, LANES)) * hmask, axis=-1, keepdims=True) for g in pages]
    mg = [jnp.max(s[g], axis=0) for g in pages]
    pr = [jnp.exp2(s[g] - mg[g][None]) for g in pages]
    lg = [jnp.sum(pr[g], axis=0) for g in pages]
    ag = [jnp.sum(pr[g] * v_refs[g][...], axis=0) for g in pages]
    off = off_s[:, 0:1]
    m_prev = m_s[:, 0:1]
    offs = []
    off_new = off
    for g in pages:
        offs.append(off_new * LOG2E)
        off_new = off_new + tot[g]
    m_new = m_prev
    for g in pages:
        m_new = jnp.maximum(m_new, mg[g] - offs[g])
    alpha = jnp.exp2(m_prev - m_new)
    l_new = alpha * l_s[:, 0:1]
    acc_new = alpha * acc_s[...]
    for g in pages:
        wg = jnp.exp2(mg[g] - offs[g] - m_new)
        l_new = l_new + wg * lg[g]
        acc_new = acc_new + wg * ag[g]
    m_s[...] = jnp.broadcast_to(m_new, m_s.shape)
    l_s[...] = jnp.broadcast_to(l_new, l_s.shape)
    acc_s[...] = acc_new
    off_s[...] = jnp.broadcast_to(off_new, off_s.shape)

    @pl.when(p == pl.num_programs(1) - 1)
    def _():
        cq = (off_new + lfn_ref[:, 0:1]) * LOG2E
        s_self = jnp.sum(q * kn_ref[...], axis=-1, keepdims=True) - cq
        m_f = jnp.maximum(m_new, s_self)
        a2 = jnp.exp2(m_new - m_f)
        ps = jnp.exp2(s_self - m_f)
        o_ref[...] = (a2 * acc_new + ps * vn_ref[...]) / (a2 * l_new + ps)


def fox_decode_attn(q, k_new, v_new, lf_new, cache_k, cache_v, cache_lf, page_table):
    B, H, D = q.shape
    n_pages = page_table.shape[1]
    T = cache_k.shape[1]
    npg = DEC_PAGES_PER_STEP if n_pages % DEC_PAGES_PER_STEP == 0 else 1
    vec = pl.BlockSpec((None, H, D), lambda b, p, pt: (b, 0, 0))
    kv = lambda g: pl.BlockSpec((None, T, H, D), lambda b, p, pt: (pt[b, p * npg + g], 0, 0, 0))
    lfs = lambda g: pl.BlockSpec((None, T, H), lambda b, p, pt: (pt[b, p * npg + g], 0, 0))
    grid_spec = pltpu.PrefetchScalarGridSpec(
        num_scalar_prefetch=1,
        grid=(B, n_pages // npg),
        in_specs=[vec, vec, vec, vec] + [kv(g) for g in range(npg)] * 2 + [lfs(g) for g in range(npg)],
        out_specs=pl.BlockSpec((None, H, D), lambda b, p, pt: (b, 0, 0)),
        scratch_shapes=[pltpu.VMEM((H, D), F32)] * 4 + [pltpu.VMEM((npg, T, LANES), F32)],
    )
    return pl.pallas_call(
        functools.partial(_fox_decode_kernel, npg=npg),
        grid_spec=grid_spec,
        out_shape=jax.ShapeDtypeStruct((B, H, D), F32),
        compiler_params=_cp(("parallel", "arbitrary")),
        name="fox_decode_attn",
    )(page_table, q, k_new, v_new, lf_new, *([cache_k] * npg), *([cache_v] * npg), *([cache_lf] * npg))


def _pad_cols(a, width):
    return jnp.pad(a, [(0, 0)] * (a.ndim - 1) + [(0, width - a.shape[-1])])


def _split_mix(a):
    W, H, D = M_WIDTH, M_HEADS, a.shape[-1]
    o = 0
    parts = {}
    for name, n in (("mqk", 2 * W), ("mv", W), ("mi", H), ("mf", H), ("mo", W), ("mz", W)):
        parts[name] = a[..., o:o + n]
        o += n
    Wr = (D - o - 2 * R_LORA) // 4
    for name, n in (("rr", Wr), ("rk", Wr), ("rv", Wr), ("rw", R_LORA), ("ra", R_LORA), ("rz", Wr)):
        parts[name] = a[..., o:o + n]
        o += n
    return parts


def _small_mix(p):
    z = jnp.zeros(p["rw"].shape[:-1] + (LANES - R_LORA,), p["rw"].dtype)
    return _pad_cols(jnp.concatenate([p["rw"], z, p["ra"], z, p["mi"], p["mf"]], axis=-1), SMALL_W)


def _arrange_mix(a):
    p = _split_mix(a)
    return jnp.concatenate([p["mqk"], p["mv"], p["mo"], p["mz"], p["rr"], p["rk"], p["rv"], p["rz"], _small_mix(p)],
                           axis=-1)


def _trunk(x, p_in, state, attn, Wt, tm):
    B, T, D = x.shape
    M = B * T
    x2 = x.reshape(M, D)
    tm_in = 4 * tm if M % (4 * tm) == 0 else tm
    tm_w = min(tm, 256)
    Wr = D
    proj = matmul_bf16(norm_cast(x2, Wt["norm_w"][0], tm_w), Wt["mix_w"], tm_in,
                       _col_tile(Wt["mix_w"].shape[1], D, M))
    W = M_WIDTH
    small0 = 3 * W + 2 * W + 4 * Wr
    gi = proj[:, small0 + 2 * LANES: small0 + 2 * LANES + M_HEADS]
    gf = proj[:, small0 + 2 * LANES + M_HEADS: small0 + 2 * LANES + 2 * M_HEADS]
    r_col0 = 5 * W

    if state is None:
        bm, C, n, m = mlstm_prompt(proj, B, T, gi.reshape(B, T, M_HEADS), gf.reshape(B, T, M_HEADS),
                                   Wt["m_conv_w"], Wt["m_conv_b"], Wt["m_b_i"], Wt["m_b_f"], Wt["m_norm"])
        conv_new = proj.reshape(B, T, -1)[:, T - (CONV_WIDTH - 1):, 0:2 * W]
        C = C
        n = n.reshape(B, M_HEADS, M_HEAD_DIM)
        m = m.reshape(B, M_HEADS)
        br, Sbd = rwkv_prompt(proj, B, T, Wt["mu4"], Wt["mu2"], Wt["r_par"], Wt["w2p"], Wt["a2p"],
                              r_col0 // LANES)
        N = R_HEAD_DIM
        S_new = jnp.stack([Sbd[:, :, 0:N, 0:N], Sbd[:, :, N:2 * N, N:2 * N]], axis=2).reshape(B, Wr // N, N, N)
    else:
        conv0, C0, n0, m0, S0, sh0 = state
        H, Dh = M_HEADS, M_HEAD_DIM
        bm, conv_new, C, n, m = mlstm_step(
            proj[:, 0:2 * W].reshape(B, 2 * H, Dh), conv0.reshape(B, CONV_WIDTH - 1, 2 * H, Dh),
            Wt["m_conv_w"].reshape(CONV_WIDTH, 2 * H, Dh), Wt["m_conv_b"].reshape(2 * H, Dh),
            proj[:, 2 * W:3 * W].reshape(B, H, Dh), gi.reshape(B, H, 1), gf.reshape(B, H, 1),
            Wt["m_b_i"].reshape(H, 1), Wt["m_b_f"].reshape(H, 1),
            proj[:, 3 * W:4 * W].reshape(B, H, Dh), proj[:, 4 * W:5 * W].reshape(B, H, Dh),
            Wt["m_norm"].reshape(H, Dh), C0, n0, m0.reshape(B, H, 1))
        bm = bm.reshape(M, W)
        conv_new = conv_new.reshape(B, CONV_WIDTH - 1, 2 * W)
        m = m.reshape(B, H)
        shp = _split_mix(jnp.concatenate([jnp.zeros((B, 5 * W + 2 * M_HEADS), F32), sh0], axis=-1))
        sh4 = jnp.concatenate([shp["rr"], shp["rk"], shp["rv"], shp["rz"]], axis=-1)
        sh2 = _small_mix(shp)[:, 0:2 * LANES]
        N = R_HEAD_DIM
        Hr = Wr // N
        r_, k_, v_, z_, w_, a_ = rwkv_prep(proj, sh4, sh2, Wt["mu4_flat"],
                                           Wt["mu2_flat"], Wt["r_par"][0:1], Wt["r_par"][1:2], Wt["w2p"], Wt["a2p"],
                                           r_col0 // Wr, small0 // LANES)
        hs = lambda t: t.reshape(B, Hr, N)
        par5 = Wt["r_par"][2:7].reshape(5, Hr, N)
        br, S_new = rwkv_step(hs(r_), hs(k_), hs(v_), hs(z_), hs(w_), hs(a_), par5, S0)
        br = br.reshape(M, Wr)
    last = proj.reshape(B, T, -1)[:, T - 1]
    sh_new = jnp.concatenate([last[:, r_col0:r_col0 + 3 * Wr], last[:, small0:small0 + R_LORA],
                              last[:, small0 + LANES:small0 + LANES + R_LORA],
                              last[:, r_col0 + 3 * Wr:r_col0 + 4 * Wr]], axis=-1)

    h1 = out_mix(bm, br, Wt["out_mix_w"], x2, min(tm, 128))
    h1, xn1 = ple_add(h1, Wt["ple_norm"][0], Wt["ple_gate_w"][0], p_in[0].reshape(M, -1), Wt["ple_proj"][0],
                      Wt["norm_w"][1], tm_w, final=False)

    proj1 = matmul_bf16(xn1, Wt["fox_w"], tm_in, _col_tile(Wt["fox_w"].shape[1], D, M))
    kn, vv, lf, *qs = qk_norm(proj1, Wt["f_q_norm"], Wt["f_k_norm"], Wt["f_b_f_pad"], min(tm, 256), D,
                              4 * D // LANES, state is None)
    Hf = D // F_HEAD_DIM
    lf = lf[:, 0:Hf]
    o = attn(qs, kn, vv, lf, B, T)
    h2 = out_fox(o, proj1, 3, Wt["out_fox_w"], h1, tm_w)
    y = ple_add(h2, Wt["ple_norm"][1], Wt["ple_gate_w"][1], p_in[1].reshape(M, -1), Wt["ple_proj"][1],
                Wt["final_norm"], tm_w, final=True).reshape(B, T, D)

    k_rows = kn.reshape(1, B, T, Hf, F_HEAD_DIM)
    v_rows = vv.reshape(1, B, T, Hf, F_HEAD_DIM)
    lf_rows = lf.reshape(1, B, T, Hf)
    mix_state = tuple(t[None] for t in (conv_new, C, n, m, S_new, sh_new))
    return y, (k_rows, v_rows, lf_rows), mix_state


def kernel(x_prompt, x_sample, cache_k, cache_v, cache_lf, state_mlstm_conv, state_mlstm_C, state_mlstm_n,
           state_mlstm_m, state_rwkv_S, state_rwkv_shift, page_table, p_prompt, p_sample,
           norm_w, final_norm, w_in_mix, w_out_mix, m_conv_w, m_conv_b, m_b_i, m_b_f, m_norm,
           r_mu, r_w0, r_w2, r_a0, r_a2, r_k_k, r_k_a, r_r_k, r_gn_w, r_gn_b,
           w_in_fox, w_out_fox, f_b_f, f_q_norm, f_k_norm, ple_proj, ple_gate_w, ple_norm):
    D = x_prompt.shape[-1]
    Wr = r_w0.shape[-1]
    Hf = D // F_HEAD_DIM
    mix_w = _arrange_mix(w_in_mix[0]).astype(BF16)
    fw = w_in_fox[0]
    fox_w = jnp.concatenate([fw[:, 0:3 * D], fw[:, 3 * D + Hf:4 * D + Hf], _pad_cols(fw[:, 3 * D:3 * D + Hf], SMALL_W)],
                            axis=-1).astype(BF16)
    mu_full = jnp.concatenate([jnp.zeros((5 * M_WIDTH + 2 * M_HEADS,), F32), r_mu[0]])
    mup = _split_mix(mu_full)
    mu4_flat = jnp.concatenate([mup["rr"], mup["rk"], mup["rv"], mup["rz"]])[None]
    mu2_flat = _small_mix(mup)[None, 0:2 * LANES]
    mu4 = _pad_cols(jnp.stack([mup["rr"], mup["rk"], mup["rv"], mup["rz"]]).T, 8).T
    mu2 = _pad_cols(mu2_flat.reshape(2, LANES).T, 8).T
    r_par = jnp.stack([r_w0[0], r_a0[0], r_k_k[0], r_k_a[0], r_r_k[0], r_gn_w[0], r_gn_b[0], jnp.zeros((Wr,), F32)])
    pad_rows = lambda a: jnp.pad(a, ((0, LANES - a.shape[0]), (0, 0)))
    Wt = dict(mix_w=mix_w, fox_w=fox_w, norm_w=norm_w, final_norm=final_norm,
              out_mix_w=w_out_mix[0].astype(BF16), out_fox_w=w_out_fox[0].astype(BF16),
              m_conv_w=m_conv_w[0], m_conv_b=m_conv_b[0], m_b_i=m_b_i[0], m_b_f=m_b_f[0], m_norm=m_norm[0],
              mu4=mu4, mu2=mu2, mu4_flat=mu4_flat, mu2_flat=mu2_flat, r_par=r_par,
              w2p=pad_rows(r_w2[0]), a2p=pad_rows(r_a2[0]),
              f_q_norm=f_q_norm[0], f_k_norm=f_k_norm[0], f_b_f_pad=_pad_cols(f_b_f[0][None], LANES),
              ple_proj=ple_proj.astype(BF16), ple_gate_w=ple_gate_w.astype(BF16), ple_norm=ple_norm)

    def attn_prompt(qs, kn, vv, lf, B, T):
        lfT = lf.reshape(B, T, Hf).transpose(0, 2, 1)
        c = cumsum_lanes(lfT)
        t = FOX_BLOCK if T % FOX_BLOCK == 0 else T
        return fox_prompt_attn(*qs, c.reshape(B, Hf, T // t, 1, t), B, T, t)

    def attn_sample(qs, kn, vv, lf, B, T):
        r3 = lambda t: t.reshape(B, Hf, F_HEAD_DIM)
        lfb = jnp.broadcast_to(lf.reshape(B, Hf, 1), (B, Hf, F_HEAD_DIM))
        o = fox_decode_attn(r3(qs[0]), r3(kn), r3(vv), lfb, cache_k[0], cache_v[0], cache_lf[0], page_table)
        return o.reshape(B, D)

    Bp, Tp, _ = x_prompt.shape
    y_p, (k_p, v_p, lf_p), mix_p = _trunk(x_prompt, p_prompt.reshape(p_prompt.shape[0], Bp * Tp, -1), None,
                                          attn_prompt, Wt, 512)
    Bs, Ts, _ = x_sample.shape
    state = (state_mlstm_conv[0], state_mlstm_C[0], state_mlstm_n[0], state_mlstm_m[0], state_rwkv_S[0],
             state_rwkv_shift[0])
    y_s, (k_s, v_s, lf_s), mix_s = _trunk(x_sample, p_sample.reshape(p_sample.shape[0], Bs * Ts, -1), state,
                                          attn_sample, Wt, Bs * Ts)
    return (y_p, y_s, k_p, v_p, lf_p) + mix_p + (k_s, v_s, lf_s) + mix_s
```

```python
import functools

import jax
import jax.numpy as jnp
from jax import lax
from jax.experimental import pallas as pl
from jax.experimental.pallas import tpu as pltpu

F32 = jnp.float32
BF16 = jnp.bfloat16
HI = lax.Precision.HIGHEST

NORM_EPS = 1e-6
RWKV_GN_EPS = 64e-5
M_HEADS = 8
M_HEAD_DIM = 256
M_WIDTH = M_HEADS * M_HEAD_DIM
CONV_WIDTH = 4
M_CHUNK = 128
M_STEP_HEADS = 4
R_HEAD_DIM = 64
R_LORA = 64
R_CHUNK = 64
R_STEP_TOKENS = 512
F_HEAD_DIM = 128
DEC_PAGES_PER_STEP = 8
LOG2E = 1.4426950408889634
FOX_BLOCK = 1024
LANES = 128
SMALL_W = 512
VMEM_LIMIT = 48 * 1024 * 1024
W_BLOCK_BUDGET = 40 * 1024 * 1024


def _cp(sem, vmem=VMEM_LIMIT):
    return pltpu.CompilerParams(dimension_semantics=sem, vmem_limit_bytes=vmem)


def _dot(a, b, prec=None):
    return jnp.dot(a, b, preferred_element_type=F32, precision=prec)


def _dot_nt(a, b, prec=None):
    return lax.dot_general(a, b, (((1,), (1,)), ((), ())), preferred_element_type=F32, precision=prec)


def _dot_tn(a, b, prec=None):
    return lax.dot_general(a, b, (((0,), (0,)), ((), ())), preferred_element_type=F32, precision=prec)


def _split_bf16(x):
    hi = x.astype(BF16)
    return hi, (x - hi.astype(F32)).astype(BF16)


def _lhs3(x, axis):
    hi, lo = _split_bf16(x)
    return jnp.concatenate([hi, hi, lo], axis=axis)


def _rhs3(x, axis):
    hi, lo = _split_bf16(x)
    return jnp.concatenate([hi, lo, hi], axis=axis)


def _dot3(a, b):
    return _dot(_lhs3(a, 1), _rhs3(b, 0))


def _dot3_nt(a, b):
    return _dot_nt(_lhs3(a, 1), _rhs3(b, 1))


def _dot3_tn(a, b):
    return _dot_tn(_lhs3(a, 0), _rhs3(b, 0))


def _softplus(y):
    return jnp.maximum(y, 0.0) + jnp.log1p(jnp.exp(-jnp.abs(y)))


def _log_sigmoid(x):
    return -_softplus(-x)


def _silu(x):
    return x * jax.nn.sigmoid(x)


def _eye(n):
    return (lax.broadcasted_iota(jnp.int32, (n, n), 0) == lax.broadcasted_iota(jnp.int32, (n, n), 1)).astype(F32)


def _rms_bf16(x, g):
    return (x * lax.rsqrt(jnp.mean(x * x, axis=-1, keepdims=True) + NORM_EPS) * g).astype(BF16)


def _norm_cast_kernel(x_ref, g_ref, o_ref):
    o_ref[...] = _rms_bf16(x_ref[...], g_ref[...])


def norm_cast(x, g, tm):
    M, K = x.shape
    return pl.pallas_call(
        _norm_cast_kernel,
        grid=(M // tm,),
        in_specs=[pl.BlockSpec((tm, K), lambda i: (i, 0)), pl.BlockSpec((1, K), lambda i: (0, 0))],
        out_specs=pl.BlockSpec((tm, K), lambda i: (i, 0)),
        out_shape=jax.ShapeDtypeStruct((M, K), BF16),
        compiler_params=_cp(("parallel",)),
        name="norm_cast",
    )(x, g.reshape(1, K))


def _col_tile(N, K, M):
    if M >= 512:
        return 512
    fits = [d * LANES for d in range(1, N // LANES + 1)
            if (N // LANES) % d == 0 and 2 * 2 * K * d * LANES <= W_BLOCK_BUDGET]
    return max(fits + [512])


def _mm_kernel(x_ref, w_ref, o_ref):
    o_ref[...] = _dot(x_ref[...], w_ref[...])


def matmul_bf16(x, w, tm, tn):
    M, K = x.shape
    N = w.shape[1]
    return pl.pallas_call(
        _mm_kernel,
        grid=(M // tm, N // tn),
        in_specs=[pl.BlockSpec((tm, K), lambda i, j: (i, 0)),
                  pl.BlockSpec((K, tn), lambda i, j: (0, j))],
        out_specs=pl.BlockSpec((tm, tn), lambda i, j: (i, j)),
        out_shape=jax.ShapeDtypeStruct((M, N), F32),
        compiler_params=_cp(("parallel", "parallel")),
        name="in_proj",
    )(x, w)


def _out_mix_kernel(a_ref, b_ref, wa_ref, wb_ref, res_ref, o_ref):
    o_ref[...] = res_ref[...] + (_dot(a_ref[...].astype(BF16), wa_ref[...]) + _dot(b_ref[...].astype(BF16), wb_ref[...]))


def out_mix(a, b, w, res, tm):
    M, K = a.shape
    N = w.shape[1]
    rows = lambda n: pl.BlockSpec((tm, n), lambda i: (i, 0))
    return pl.pallas_call(
        _out_mix_kernel,
        grid=(M // tm,),
        in_specs=[rows(K), rows(K),
                  pl.BlockSpec((K, N), lambda i: (0, 0)),
                  pl.BlockSpec((K, N), lambda i: (1, 0)),
                  rows(N)],
        out_specs=rows(N),
        out_shape=jax.ShapeDtypeStruct((M, N), F32),
        compiler_params=_cp(("parallel",)),
        name="out_mix",
    )(a, b, w, w, res)


def _out_fox_kernel(o_in_ref, z_ref, w_ref, res_ref, o_ref):
    g = (o_in_ref[...] * _silu(z_ref[...])).astype(BF16)
    o_ref[...] = res_ref[...] + _dot(g, w_ref[...])


def out_fox(o, proj, z_blk, w, res, tm):
    M, K = o.shape
    N = w.shape[1]
    return pl.pallas_call(
        _out_fox_kernel,
        grid=(M // tm,),
        in_specs=[pl.BlockSpec((tm, K), lambda i: (i, 0)),
                  pl.BlockSpec((tm, K), lambda i: (i, z_blk)),
                  pl.BlockSpec((K, N), lambda i: (0, 0)),
                  pl.BlockSpec((tm, N), lambda i: (i, 0))],
        out_specs=pl.BlockSpec((tm, N), lambda i: (i, 0)),
        out_shape=jax.ShapeDtypeStruct((M, N), F32),
        compiler_params=_cp(("parallel",)),
        name="out_fox",
    )(o, proj, w, res)


def _ple_kernel(h_ref, g_ref, wg_ref, p_ref, pw_ref, ng_ref, *outs, final):
    x = h_ref[...]
    gate = jax.nn.sigmoid(_dot(_rms_bf16(x, g_ref[...]), wg_ref[...]))
    e = _dot(p_ref[...].astype(BF16), pw_ref[...])
    out = x + gate * e
    if final:
        (y_ref,) = outs
        y_ref[...] = out * lax.rsqrt(jnp.mean(out * out, axis=-1, keepdims=True) + NORM_EPS) * ng_ref[...]
    else:
        o_ref, xn_ref = outs
        o_ref[...] = out
        xn_ref[...] = _rms_bf16(out, ng_ref[...])


def ple_add(h, g, wg, p, pw, next_g, tm, final):
    M, K = h.shape
    N = wg.shape[1]
    P = p.shape[1]
    rows = pl.BlockSpec((tm, N), lambda i: (i, 0))
    vec = pl.BlockSpec((1, K), lambda i: (0, 0))
    f32_rows = jax.ShapeDtypeStruct((M, N), F32)
    return pl.pallas_call(
        functools.partial(_ple_kernel, final=final),
        grid=(M // tm,),
        in_specs=[pl.BlockSpec((tm, K), lambda i: (i, 0)), vec,
                  pl.BlockSpec((K, N), lambda i: (0, 0)),
                  pl.BlockSpec((tm, P), lambda i: (i, 0)),
                  pl.BlockSpec((P, N), lambda i: (0, 0)), vec],
        out_specs=rows if final else [rows, rows],
        out_shape=f32_rows if final else [f32_rows, jax.ShapeDtypeStruct((M, N), BF16)],
        compiler_params=_cp(("parallel",)),
        name="ple_add",
    )(h, g.reshape(1, K), wg, p, pw, next_g.reshape(1, N))


def _mlstm_chunk_kernel(qp_ref, kp_ref, v_ref, mo_ref, mz_ref, g_ref, gb_ref, cwq_ref, cwk_ref, mn_ref,
                        out_ref, C_out, n_out, m_out,
                        C_s, n_s, m_s, qbuf, kbuf):
    c = pl.program_id(2)
    nc = pl.num_programs(2)
    L = qp_ref.shape[0]
    HH = C_s.shape[0]
    Dh = qp_ref.shape[1] // HH
    heads = range(HH)
    hs = [slice(h * Dh, (h + 1) * Dh) for h in heads]

    @pl.when(c == 0)
    def _():
        C_s[...] = jnp.zeros(C_s.shape, F32)
        n_s[...] = jnp.zeros(n_s.shape, F32)
        m_s[...] = jnp.zeros(m_s.shape, F32)
        qbuf[0:8, :] = jnp.zeros((8, HH * Dh), F32)
        kbuf[0:8, :] = jnp.zeros((8, HH * Dh), F32)

    qbuf[8:8 + L, :] = qp_ref[...]
    kbuf[8:8 + L, :] = kp_ref[...]

    def conv(buf, w_ref):
        acc = w_ref[CONV_WIDTH:CONV_WIDTH + 1, :]
        for j in range(CONV_WIDTH):
            acc = acc + buf[8 - (CONV_WIDTH - 1) + j: 8 - (CONV_WIDTH - 1) + j + L, :] * w_ref[j:j + 1, :]
        return acc

    q = _silu(conv(qbuf, cwq_ref))
    k = _silu(conv(kbuf, cwk_ref)) * (Dh ** -0.5)
    qbuf[0:8, :] = qbuf[L:L + 8, :]
    kbuf[0:8, :] = kbuf[L:L + 8, :]
    v = v_ref[...]

    g = [g_ref[h] + gb_ref[h] for h in heads]
    li_r = [g[h][0:1, :] for h in heads]
    lf_r = [_log_sigmoid(g[h][1:2, :]) for h in heads]
    row = lax.broadcasted_iota(jnp.int32, (L, L), 0)
    col = lax.broadcasted_iota(jnp.int32, (L, L), 1)
    tri = row >= col
    diag = row == col
    b_c = [jnp.sum(jnp.where(tri, lf_r[h], 0.0), axis=1, keepdims=True) for h in heads]
    b_r = [jnp.sum(jnp.where(diag, b_c[h], 0.0), axis=0, keepdims=True) for h in heads]
    li_c = [jnp.sum(jnp.where(diag, li_r[h], 0.0), axis=1, keepdims=True) for h in heads]
    m_prev = [m_s[h] for h in heads]
    dmat = [jnp.where(tri, b_c[h] - b_r[h] + li_r[h], -jnp.inf) for h in heads]
    inter = [b_c[h] + m_prev[h] for h in heads]
    m_t = [jnp.maximum(inter[h], jnp.max(dmat[h], axis=1, keepdims=True)) for h in heads]
    w_intra = [jnp.exp(dmat[h] - m_t[h]) for h in heads]
    w_inter = [jnp.exp(inter[h] - m_t[h]) for h in heads]
    qb = [q[:, hs[h]].astype(BF16) for h in heads]
    kb = [k[:, hs[h]].astype(BF16) for h in heads]
    vb = [v[:, hs[h]].astype(BF16) for h in heads]
    C = [C_s[h] for h in heads]
    n_row = [n_s[h] for h in heads]
    s = [_dot_nt(qb[h], kb[h]) * w_intra[h] for h in heads]
    qC = [_dot_nt(qb[h], C[h].astype(BF16)) for h in heads]
    num = [_dot(s[h].astype(BF16), vb[h]) + w_inter[h] * qC[h] for h in heads]
    den = [jnp.sum(s[h], axis=1, keepdims=True)
           + w_inter[h] * jnp.sum(q[:, hs[h]] * n_row[h], axis=1, keepdims=True) for h in heads]
    hval = [num[h] / jnp.maximum(jnp.abs(den[h]), jnp.exp(-m_t[h])) for h in heads]
    m_new = [m_t[h][L - 1:L, :] for h in heads]
    bL = [b_c[h][L - 1:L, :] for h in heads]
    cd = [jnp.exp(bL[h] + m_prev[h] - m_new[h]) for h in heads]
    w_c = [jnp.exp(bL[h] - b_c[h] + li_c[h] - m_new[h]) for h in heads]
    w_r = [jnp.exp(bL[h] - b_r[h] + li_r[h] - m_new[h]) for h in heads]
    C_new = [cd[h] * C[h] + _dot_tn((v[:, hs[h]] * w_c[h]).astype(BF16), kb[h]) for h in heads]
    n_new = [cd[h] * n_row[h] + _dot(w_r[h].astype(BF16), kb[h]) for h in heads]
    hn = [hval[h] * lax.rsqrt(jnp.mean(hval[h] * hval[h], axis=-1, keepdims=True) + NORM_EPS) for h in heads]
    for h in heads:
        C_s[h] = C_new[h]
        n_s[h] = n_new[h]
        m_s[h] = m_new[h]
    out_ref[...] = jnp.concatenate(hn, axis=1) * mn_ref[...] * jax.nn.sigmoid(mo_ref[...]) * _silu(mz_ref[...])

    @pl.when(c == nc - 1)
    def _():
        for h in heads:
            C_out[h] = C_new[h]
            n_out[h] = n_new[h]
            m_out[h] = m_new[h]


def mlstm_prompt(proj, B, S, gi, gf, conv_w, conv_b, b_i, b_f, m_norm):
    H, Dh, L = M_HEADS, M_HEAD_DIM, M_CHUNK
    nc = S // L
    M = B * S
    g2 = jnp.stack([gi, gf], axis=-1).reshape(B, nc, L, H, 2).transpose(0, 3, 1, 4, 2)
    gb = jnp.stack([b_i, b_f], axis=-1).reshape(H, 2, 1)
    cw = jnp.concatenate([conv_w, conv_b.reshape(1, -1)], axis=0)
    HH = M_STEP_HEADS
    Wb = HH * Dh
    pblk = lambda off: pl.BlockSpec((L, Wb), lambda b, h, c: (b * nc + c, off + h))
    hb = M_WIDTH // Wb
    outs = pl.pallas_call(
        _mlstm_chunk_kernel,
        grid=(B, H // HH, nc),
        in_specs=[pblk(0), pblk(hb), pblk(2 * hb), pblk(3 * hb), pblk(4 * hb),
                  pl.BlockSpec((None, HH, None, 2, L), lambda b, h, c: (b, h, c, 0, 0)),
                  pl.BlockSpec((HH, 2, 1), lambda b, h, c: (h, 0, 0)),
                  pl.BlockSpec((CONV_WIDTH + 1, Wb), lambda b, h, c: (0, h)),
                  pl.BlockSpec((CONV_WIDTH + 1, Wb), lambda b, h, c: (0, hb + h)),
                  pl.BlockSpec((1, Wb), lambda b, h, c: (0, h))],
        out_specs=[pl.BlockSpec((L, Wb), lambda b, h, c: (b * nc + c, h)),
                   pl.BlockSpec((None, HH, Dh, Dh), lambda b, h, c: (b, h, 0, 0)),
                   pl.BlockSpec((None, HH, 1, Dh), lambda b, h, c: (b, h, 0, 0)),
                   pl.BlockSpec((None, HH, 1, 1), lambda b, h, c: (b, h, 0, 0))],
        out_shape=[jax.ShapeDtypeStruct((M, M_WIDTH), F32),
                   jax.ShapeDtypeStruct((B, H, Dh, Dh), F32),
                   jax.ShapeDtypeStruct((B, H, 1, Dh), F32),
                   jax.ShapeDtypeStruct((B, H, 1, 1), F32)],
        scratch_shapes=[pltpu.VMEM((HH, Dh, Dh), F32), pltpu.VMEM((HH, 1, Dh), F32), pltpu.VMEM((HH, 1, 1), F32),
                        pltpu.VMEM((L + 8, Wb), F32), pltpu.VMEM((L + 8, Wb), F32)],
        compiler_params=_cp(("parallel", "parallel", "arbitrary")),
        name="mlstm_prompt",
    )(proj, proj, proj, proj, proj, g2, gb, cw, cw, m_norm.reshape(1, -1))
    return outs


def _mlstm_step_kernel(x_ref, conv0_ref, cw_ref, cb_ref, v_ref, gi_ref, gf_ref, bi_ref, bf_ref,
                       mo_ref, mz_ref, mn_ref, C_ref, n_ref, m_ref,
                       out_ref, conv_out, C_out, n_out, m_out):
    H, Dh = v_ref.shape
    x = x_ref[...]
    qk = cb_ref[...]
    for j in range(CONV_WIDTH - 1):
        qk = qk + conv0_ref[j] * cw_ref[j]
    qk = qk + x * cw_ref[CONV_WIDTH - 1]
    for j in range(CONV_WIDTH - 2):
        conv_out[j] = conv0_ref[j + 1]
    conv_out[CONV_WIDTH - 2] = x
    qk = _silu(qk)
    q = qk[0:H]
    k = qk[H:2 * H] * (Dh ** -0.5)
    v = v_ref[...]
    li = gi_ref[...] + bi_ref[...]
    lf = _log_sigmoid(gf_ref[...] + bf_ref[...])
    m0 = m_ref[...]
    inter = lf + m0
    m_t = jnp.maximum(inter, li)
    w_intra = jnp.exp(li - m_t)
    w_inter = jnp.exp(inter - m_t)
    s = jnp.sum(q * k, axis=-1, keepdims=True) * w_intra
    n0 = n_ref[...]
    hrow = lax.broadcasted_iota(jnp.int32, (H, Dh), 0)
    cq = jnp.zeros((H, Dh), F32)
    vT = _dot_nt(_eye(Dh), v * w_intra, HI)
    Cs = [C_ref[h] for h in range(H)]
    cqs = [_dot_nt(q, Cs[h], HI) for h in range(H)]
    for h in range(H):
        cq = jnp.where(hrow == h, cqs[h], cq)
        C_out[h] = w_inter[h:h + 1, :] * Cs[h] + vT[:, h:h + 1] * k[h:h + 1, :]
    num = s * v + w_inter * cq
    den = s + w_inter * jnp.sum(n0 * q, axis=-1, keepdims=True)
    hval = num / jnp.maximum(jnp.abs(den), jnp.exp(-m_t))
    n_out[...] = w_inter * n0 + w_intra * k
    m_out[...] = m_t
    hn = hval * lax.rsqrt(jnp.mean(hval * hval, axis=-1, keepdims=True) + NORM_EPS) * mn_ref[...]
    out_ref[...] = hn * jax.nn.sigmoid(mo_ref[...]) * _silu(mz_ref[...])


def mlstm_step(x16, conv0, conv_w, conv_b, v, gi, gf, b_i, b_f, mo, mz, m_norm, C0, n0, m0):
    B = x16.shape[0]
    H, Dh = M_HEADS, M_HEAD_DIM
    W = CONV_WIDTH
    per_b = lambda *shape: pl.BlockSpec((None,) + shape, lambda b: (b,) + (0,) * len(shape))
    const = lambda *shape: pl.BlockSpec(shape, lambda b: (0,) * len(shape))
    return pl.pallas_call(
        _mlstm_step_kernel,
        grid=(B,),
        in_specs=[per_b(2 * H, Dh), per_b(W - 1, 2 * H, Dh), const(W, 2 * H, Dh), const(2 * H, Dh),
                  per_b(H, Dh), per_b(H, 1), per_b(H, 1), const(H, 1), const(H, 1),
                  per_b(H, Dh), per_b(H, Dh), const(H, Dh),
                  per_b(H, Dh, Dh), per_b(H, Dh), per_b(H, 1)],
        out_specs=[per_b(H, Dh), per_b(W - 1, 2 * H, Dh), per_b(H, Dh, Dh), per_b(H, Dh), per_b(H, 1)],
        out_shape=[jax.ShapeDtypeStruct((B, H, Dh), F32),
                   jax.ShapeDtypeStruct((B, W - 1, 2 * H, Dh), F32),
                   jax.ShapeDtypeStruct((B, H, Dh, Dh), F32),
                   jax.ShapeDtypeStruct((B, H, Dh), F32),
                   jax.ShapeDtypeStruct((B, H, 1), F32)],
        compiler_params=_cp(("parallel",)),
        name="mlstm_step",
    )(x16, conv0, conv_w, conv_b, v, gi, gf, b_i, b_f, mo, mz, m_norm, C0, n0, m0)


def _rwkv_chunk_kernel(rr_ref, rk_ref, rv_ref, rz_ref, rw_ref, ra_ref, mu4_ref, mu2_ref, par_ref,
                       w2_ref, a2_ref, out_ref, S_out, S_s, carry_s):
    c = pl.program_id(2)
    nc = pl.num_programs(2)
    TB = rr_ref.shape[0]
    W = rr_ref.shape[1]
    N = R_HEAD_DIM
    L = R_CHUNK

    @pl.when(c == 0)
    def _():
        S_s[...] = jnp.zeros(S_s.shape, F32)
        carry_s[...] = jnp.zeros(carry_s.shape, F32)

    row = lax.broadcasted_iota(jnp.int32, (TB, W), 0)
    lo = lax.broadcasted_iota(jnp.int32, (TB, W), 1) < N

    def shift(x, idx, mu):
        prev = jnp.where(row == 0, carry_s[idx:idx + 1, :], pltpu.roll(x, 1, 0))
        carry_s[idx:idx + 1, :] = x[TB - 1:TB, :]
        return x + mu * (prev - x)

    r = shift(rr_ref[...], 0, mu4_ref[0:1, :])
    k = shift(rk_ref[...], 1, mu4_ref[1:2, :])
    v = shift(rv_ref[...], 2, mu4_ref[2:3, :])
    z = shift(rz_ref[...], 3, mu4_ref[3:4, :])
    xw = shift(rw_ref[...], 4, mu2_ref[0:1, :])
    xa = shift(ra_ref[...], 5, mu2_ref[1:2, :])
    w0, a0 = par_ref[0:1, :], par_ref[1:2, :]
    k_k, k_a, r_k = par_ref[2:3, :], par_ref[3:4, :], par_ref[4:5, :]
    gn_w, gn_b = par_ref[5:6, :], par_ref[6:7, :]

    def seg_sum(x):
        s0 = jnp.sum(jnp.where(lo, x, 0.0), axis=1, keepdims=True)
        s1 = jnp.sum(jnp.where(lo, 0.0, x), axis=1, keepdims=True)
        return jnp.where(lo, s0, s1)

    w_log = -_softplus(-(w0 + _dot3(jnp.tanh(xw), w2_ref[...]))) - 0.5
    logw = -jnp.exp(w_log)
    a = jax.nn.sigmoid(a0 + _dot3(xa, a2_ref[...]))
    kk = k * k_k
    kk = kk * lax.rsqrt(jnp.maximum(seg_sum(kk * kk), 1e-24))
    k2 = k * (1.0 + (a - 1.0) * k_a)
    alpha = -kk
    beta = kk * a

    tr = lax.broadcasted_iota(jnp.int32, (TB, TB), 0)
    tc = lax.broadcasted_iota(jnp.int32, (TB, TB), 1)
    in_chunk = jnp.logical_and(tc <= tr, tc >= tr - jnp.bitwise_and(tr, L - 1))
    l_hi = logw.astype(BF16)
    l_r = logw - l_hi.astype(F32)
    l_mid = l_r.astype(BF16)
    l_lo = (l_r - l_mid.astype(F32)).astype(BF16)
    b3 = _dot(jnp.where(in_chunk, 1.0, 0.0).astype(BF16), jnp.concatenate([l_hi, l_mid, l_lo], axis=1))
    b = (b3[:, 0:W] + b3[:, W:2 * W]) + b3[:, 2 * W:3 * W]

    lo_c = lax.broadcasted_iota(jnp.int32, (L, W), 1) < N

    def stack(x):
        return jnp.concatenate([jnp.where(lo_c, x, 0.0), jnp.where(lo_c, 0.0, x)], axis=0)

    ri = lax.broadcasted_iota(jnp.int32, (2 * L, 2 * L), 0)
    ci = lax.broadcasted_iota(jnp.int32, (2 * L, 2 * L), 1)
    same = jnp.where(ri >= L, 1, 0) == jnp.where(ci >= L, 1, 0)
    strict = jnp.logical_and(same, ci < ri)
    incl = jnp.logical_and(same, ci <= ri)
    eye = jnp.where(ri == ci, 1.0, 0.0)

    chunks = range(TB // L)
    sls = [slice(ch * L, (ch + 1) * L) for ch in chunks]
    bLs = [b[sl][L - 1:L, :] for sl in sls]
    a_s = [stack(alpha[sl] * jnp.exp(b[sl] - logw[sl])) for sl in sls]
    r_s = [stack(r[sl] * jnp.exp(b[sl])) for sl in sls]
    Ys_ = [jnp.concatenate([stack(beta[sl] * jnp.exp(-b[sl])), stack(k2[sl] * jnp.exp(-b[sl]))], axis=0) for sl in sls]
    bh_s = [stack(beta[sl] * jnp.exp(bL - b[sl])) for sl, bL in zip(sls, bLs)]
    kh_s = [stack(k2[sl] * jnp.exp(bL - b[sl])) for sl, bL in zip(sls, bLs)]
    v_s = [stack(v[sl]) for sl in sls]
    Gs = [_dot3_nt(a_s[i], Ys_[i]) for i in chunks]
    Go = [_dot_nt(r_s[i].astype(BF16), Ys_[i].astype(BF16)) for i in chunks]
    A = [jnp.where(strict, Gs[i][:, 0:2 * L], 0.0) for i in chunks]
    Bm = [jnp.where(strict, Gs[i][:, 2 * L:4 * L], 0.0) for i in chunks]
    P = [jnp.where(incl, Go[i][:, 0:2 * L], 0.0).astype(BF16) for i in chunks]
    Q = [jnp.where(incl, Go[i][:, 2 * L:4 * L], 0.0).astype(BF16) for i in chunks]
    BV = [_dot3(Bm[i], v_s[i]) for i in chunks]
    QV = [_dot(Q[i], v_s[i].astype(BF16)) for i in chunks]
    T = [eye + A[i] for i in chunks]
    Ap = [_dot3(A[i], A[i]) for i in chunks]
    for _ in range(4):
        R = [_dot(_lhs3(Ap[i], 1), _rhs3(jnp.concatenate([Ap[i], T[i]], axis=1), 0)) for i in chunks]
        T = [T[i] + R[i][:, 2 * L:4 * L] for i in chunks]
        Ap = [R[i][:, 0:2 * L] for i in chunks]
    T = [T[i] + _dot3(Ap[i], T[i]) for i in chunks]
    TXB = [_dot3(T[i], jnp.concatenate([a_s[i], BV[i]], axis=1)) for i in chunks]
    PTX = [_dot(P[i], TXB[i].astype(BF16)) for i in chunks]
    MN = [_dot3_tn(TXB[i], bh_s[i]) for i in chunks]
    VK = [_dot3_tn(v_s[i], kh_s[i]) for i in chunks]
    pre = [((r_s[i] + PTX[i][:, 0:W]).astype(BF16), QV[i] + PTX[i][:, W:2 * W],
            MN[i][0:W], MN[i][W:2 * W] + VK[i], jnp.exp(bLs[i])) for i in chunks]

    S = S_s[...]
    ys = []
    for Rp, Y0, Mc, Nc, gL in pre:
        Ys = _dot_nt(Rp, S.astype(BF16)) + Y0
        ys.append(Ys[0:L] + Ys[L:2 * L])
        S = (S * gL + _dot3(S, Mc)) + Nc
    S_new = S
    S_s[...] = S_new
    y = jnp.concatenate(ys, axis=0)

    mu_y = seg_sum(y) * (1.0 / N)
    d = y - mu_y
    var = seg_sum(d * d) * (1.0 / N)
    yn = d * lax.rsqrt(var + RWKV_GN_EPS) * gn_w + gn_b
    yn = yn + seg_sum(r * k2 * r_k) * v
    out_ref[...] = yn * _silu(z)

    @pl.when(c == nc - 1)
    def _():
        S_out[...] = S_new


def rwkv_prompt(proj, B, S, mu4, mu2, par, w2p, a2p, r_off):
    L = R_STEP_TOKENS
    W = 2 * R_HEAD_DIM
    nc = S // L
    M = B * S
    npair = mu4.shape[1] // W
    blk = lambda off: pl.BlockSpec((L, W), lambda b, p, c: (b * nc + c, off + p))
    cblk = lambda off: pl.BlockSpec((L, W), lambda b, p, c: (b * nc + c, off))
    return pl.pallas_call(
        _rwkv_chunk_kernel,
        grid=(B, npair, nc),
        in_specs=[blk(r_off), blk(r_off + npair), blk(r_off + 2 * npair), blk(r_off + 3 * npair),
                  cblk(r_off + 4 * npair), cblk(r_off + 4 * npair + 1),
                  pl.BlockSpec((8, W), lambda b, p, c: (0, p)),
                  pl.BlockSpec((8, W), lambda b, p, c: (0, 0)),
                  pl.BlockSpec((8, W), lambda b, p, c: (0, p)),
                  pl.BlockSpec((W, W), lambda b, p, c: (0, p)),
                  pl.BlockSpec((W, W), lambda b, p, c: (0, p))],
        out_specs=[pl.BlockSpec((L, W), lambda b, p, c: (b * nc + c, p)),
                   pl.BlockSpec((None, None, W, W), lambda b, p, c: (b, p, 0, 0))],
        out_shape=[jax.ShapeDtypeStruct((M, npair * W), F32),
                   jax.ShapeDtypeStruct((B, npair, W, W), F32)],
        scratch_shapes=[pltpu.VMEM((W, W), F32), pltpu.VMEM((8, W), F32)],
        compiler_params=_cp(("parallel", "parallel", "arbitrary")),
        name="rwkv_prompt",
    )(proj, proj, proj, proj, proj, proj, mu4, mu2, par, w2p, a2p)


def _rwkv_prep_kernel(rr_ref, rk_ref, rv_ref, rz_ref, rw_ref, ra_ref, sh4_ref, sh2_ref, mu4_ref, mu2_ref,
                      w0_ref, a0_ref, w2_ref, a2_ref, r_o, k_o, v_o, z_o, w_o, a_o):
    Wd = rr_ref.shape[1]
    sh = lambda x, prev, mu: x + mu * (prev - x)
    r_o[...] = sh(rr_ref[...], sh4_ref[:, 0:Wd], mu4_ref[:, 0:Wd])
    k_o[...] = sh(rk_ref[...], sh4_ref[:, Wd:2 * Wd], mu4_ref[:, Wd:2 * Wd])
    v_o[...] = sh(rv_ref[...], sh4_ref[:, 2 * Wd:3 * Wd], mu4_ref[:, 2 * Wd:3 * Wd])
    z_o[...] = sh(rz_ref[...], sh4_ref[:, 3 * Wd:4 * Wd], mu4_ref[:, 3 * Wd:4 * Wd])
    xw = sh(rw_ref[...], sh2_ref[:, 0:LANES], mu2_ref[:, 0:LANES])
    xa = sh(ra_ref[...], sh2_ref[:, LANES:2 * LANES], mu2_ref[:, LANES:2 * LANES])
    w_log = -_softplus(-(w0_ref[...] + _dot(jnp.tanh(xw), w2_ref[...], HI))) - 0.5
    w_o[...] = jnp.exp(-jnp.exp(w_log))
    a_o[...] = jax.nn.sigmoid(a0_ref[...] + _dot(xa, a2_ref[...], HI))


def rwkv_prep(proj, sh4, sh2, mu4, mu2, w0, a0, w2p, a2p, r_blk, s_blk):
    B = proj.shape[0]
    Wd = w0.shape[1]
    full = lambda a: pl.BlockSpec(a.shape, lambda i: (0,) * a.ndim)
    return pl.pallas_call(
        _rwkv_prep_kernel,
        grid=(1,),
        in_specs=[pl.BlockSpec((B, Wd), lambda i: (0, r_blk)), pl.BlockSpec((B, Wd), lambda i: (0, r_blk + 1)),
                  pl.BlockSpec((B, Wd), lambda i: (0, r_blk + 2)), pl.BlockSpec((B, Wd), lambda i: (0, r_blk + 3)),
                  pl.BlockSpec((B, LANES), lambda i: (0, s_blk)), pl.BlockSpec((B, LANES), lambda i: (0, s_blk + 1)),
                  full(sh4), full(sh2), full(mu4), full(mu2), full(w0), full(a0), full(w2p), full(a2p)],
        out_specs=[pl.BlockSpec((B, Wd), lambda i: (0, 0))] * 6,
        out_shape=[jax.ShapeDtypeStruct((B, Wd), F32)] * 6,
        compiler_params=_cp(("arbitrary",)),
        name="rwkv_prep",
    )(proj, proj, proj, proj, proj, proj, sh4, sh2, mu4, mu2, w0, a0, w2p, a2p)


def _rwkv_step_kernel(r_ref, k_ref, v_ref, z_ref, w_ref, a_ref, par_ref, S_ref, out_ref, S_out):
    H, N = r_ref.shape
    r, k, v, z, w, a = (t[...] for t in (r_ref, k_ref, v_ref, z_ref, w_ref, a_ref))
    k_k, k_a, r_k, gn_w, gn_b = (par_ref[i] for i in range(5))
    kk = k * k_k
    kk = kk * lax.rsqrt(jnp.maximum(jnp.sum(kk * kk, axis=-1, keepdims=True), 1e-24))
    k2 = k * (1.0 + (a - 1.0) * k_a)
    alpha = -kk
    beta = kk * a
    vT = _dot_nt(_eye(N), v, HI)
    lane_h = lax.broadcasted_iota(jnp.int32, (N, H), 1)
    heads = range(H)
    S = [S_ref[h] for h in heads]
    sa = [jnp.sum(S[h] * alpha[h:h + 1, :], axis=-1, keepdims=True) for h in heads]
    Sn = [S[h] * w[h:h + 1, :] + sa[h] * beta[h:h + 1, :] + vT[:, h:h + 1] * k2[h:h + 1, :] for h in heads]
    ys = [jnp.sum(Sn[h] * r[h:h + 1, :], axis=-1, keepdims=True) for h in heads]
    yT = jnp.zeros((N, H), F32)
    for h in heads:
        S_out[h] = Sn[h]
        yT = jnp.where(lane_h == h, ys[h], yT)
    y = _dot_nt(_eye(H), yT, HI)
    mu_y = jnp.mean(y, axis=-1, keepdims=True)
    d = y - mu_y
    var = jnp.mean(d * d, axis=-1, keepdims=True)
    yn = d * lax.rsqrt(var + RWKV_GN_EPS) * gn_w + gn_b
    yn = yn + jnp.sum(r * k2 * r_k, axis=-1, keepdims=True) * v
    out_ref[...] = yn * _silu(z)


def rwkv_step(r, k, v, z, w, a, par, S0):
    B, H, N = r.shape
    vec = pl.BlockSpec((None, H, N), lambda b: (b, 0, 0))
    mat = pl.BlockSpec((None, H, N, N), lambda b: (b, 0, 0, 0))
    return pl.pallas_call(
        _rwkv_step_kernel,
        grid=(B,),
        in_specs=[vec] * 6 + [pl.BlockSpec(par.shape, lambda b: (0, 0, 0)), mat],
        out_specs=[vec, mat],
        out_shape=[jax.ShapeDtypeStruct((B, H, N), F32), jax.ShapeDtypeStruct((B, H, N, N), F32)],
        compiler_params=_cp(("parallel",)),
        name="rwkv_step",
    )(r, k, v, z, w, a, par, S0)


def _qk_norm_kernel(q_ref, k_ref, v_ref, f_ref, gq_ref, gk_ref, bf_ref, kn_ref, v_out_ref, lf_ref, *q_refs,
                    mxu_copies):
    D = F_HEAD_DIM
    for h in range(q_ref.shape[1] // D):
        sl = slice(h * D, (h + 1) * D)
        x = q_ref[:, sl]
        qn = x * lax.rsqrt(jnp.mean(x * x, axis=-1, keepdims=True) + NORM_EPS) * gq_ref[...]
        y = k_ref[:, sl]
        kn = y * lax.rsqrt(jnp.mean(y * y, axis=-1, keepdims=True) + NORM_EPS) * gk_ref[...]
        v = v_ref[:, sl]
        kn_ref[:, sl] = kn
        v_out_ref[:, sl] = v
        if mxu_copies:
            qb_ref, kb_ref, vb_ref = q_refs
            qb_ref[:, sl] = (qn * (D ** -0.5 * LOG2E)).astype(BF16)
            kb_ref[:, sl] = kn.astype(BF16)
            vb_ref[:, sl] = v.astype(BF16)
        else:
            q_refs[0][:, sl] = qn
    lf_ref[...] = _log_sigmoid(f_ref[...] + bf_ref[...])


def qk_norm(proj, gq, gk, bf_pad, tm, Wd, f_blk, mxu_copies):
    M = proj.shape[0]
    wide = pl.BlockSpec((tm, Wd), lambda i: (i, 0))
    wide_sds = lambda dt: jax.ShapeDtypeStruct((M, Wd), dt)
    extra = [wide_sds(BF16)] * 3 if mxu_copies else [wide_sds(F32)]
    return pl.pallas_call(
        functools.partial(_qk_norm_kernel, mxu_copies=mxu_copies),
        grid=(M // tm,),
        in_specs=[pl.BlockSpec((tm, Wd), lambda i: (i, 0)), pl.BlockSpec((tm, Wd), lambda i: (i, 1)),
                  pl.BlockSpec((tm, Wd), lambda i: (i, 2)),
                  pl.BlockSpec((tm, LANES), lambda i: (i, f_blk)),
                  pl.BlockSpec((1, F_HEAD_DIM), lambda i: (0, 0)), pl.BlockSpec((1, F_HEAD_DIM), lambda i: (0, 0)),
                  pl.BlockSpec((1, LANES), lambda i: (0, 0))],
        out_specs=[wide, wide, pl.BlockSpec((tm, LANES), lambda i: (i, 0))] + [wide] * len(extra),
        out_shape=[wide_sds(F32), wide_sds(F32), jax.ShapeDtypeStruct((M, LANES), F32)] + extra,
        compiler_params=_cp(("parallel",)),
        name="qk_norm",
    )(proj, proj, proj, proj, gq.reshape(1, -1), gk.reshape(1, -1), bf_pad)


def _cumsum_kernel(x_ref, c_ref):
    H, S = x_ref.shape
    U = (lax.broadcasted_iota(jnp.int32, (LANES, LANES), 0) <= lax.broadcasted_iota(jnp.int32, (LANES, LANES), 1)).astype(F32)
    off = jnp.zeros((H, 1), F32)
    for t in range(S // LANES):
        sl = slice(t * LANES, (t + 1) * LANES)
        w = _dot(x_ref[:, sl], U, HI) + off
        c_ref[:, sl] = w
        off = w[:, LANES - 1:LANES]


def cumsum_lanes(x):
    B, H, S = x.shape
    return pl.pallas_call(
        _cumsum_kernel,
        grid=(B,),
        in_specs=[pl.BlockSpec((None, H, S), lambda b: (b, 0, 0))],
        out_specs=pl.BlockSpec((None, H, S), lambda b: (b, 0, 0)),
        out_shape=jax.ShapeDtypeStruct((B, H, S), F32),
        compiler_params=_cp(("parallel",)),
        name="forget_cumsum",
    )(x)


def _fox_flash_kernel(q_ref, k_ref, v_ref, ck_ref, o_ref, m_s, l_s, acc_s):
    i = pl.program_id(2)
    t, D = q_ref.shape
    q = q_ref[...]
    m_s[...] = jnp.full(m_s.shape, -jnp.inf, F32)
    l_s[...] = jnp.zeros(l_s.shape, F32)
    acc_s[...] = jnp.zeros(acc_s.shape, F32)

    def block(j, masked):
        start = pl.multiple_of(j * t, t)
        s = _dot_nt(q, k_ref[pl.ds(start, t), :]) - ck_ref[j] * LOG2E
        if masked:
            row = lax.broadcasted_iota(jnp.int32, (t, t), 0)
            col = lax.broadcasted_iota(jnp.int32, (t, t), 1)
            s = jnp.where(col <= row, s, -jnp.inf)
        m_prev = m_s[...]
        m_new = jnp.maximum(m_prev, jnp.max(s, axis=1, keepdims=True))
        alpha = jnp.exp2(m_prev - m_new)
        p = jnp.exp2(s - jnp.concatenate([m_new] * (t // LANES), axis=1))
        l_s[...] = alpha * l_s[...] + jnp.sum(p, axis=1, keepdims=True)
        acc_s[...] = alpha * acc_s[...] + _dot(p.astype(BF16), v_ref[pl.ds(start, t), :])
        m_s[...] = m_new

    def body(j, carry):
        block(j, False)
        return carry

    lax.fori_loop(0, i, body, 0)
    block(i, True)
    o_ref[...] = acc_s[...] / l_s[...]


def fox_prompt_attn(qb, kb, vb, ck, B, S, t):
    D = F_HEAD_DIM
    H = qb.shape[1] // D
    nb = S // t
    full = pl.BlockSpec((S, D), lambda b, h, i: (b, h))
    return pl.pallas_call(
        _fox_flash_kernel,
        grid=(B, H, nb),
        in_specs=[pl.BlockSpec((t, D), lambda b, h, i: (b * nb + i, h)), full, full,
                  pl.BlockSpec((None, None, nb, 1, t), lambda b, h, i: (b, h, 0, 0, 0))],
        out_specs=pl.BlockSpec((t, D), lambda b, h, i: (b * nb + i, h)),
        out_shape=jax.ShapeDtypeStruct((B * S, H * D), F32),
        scratch_shapes=[pltpu.VMEM((t, LANES), F32), pltpu.VMEM((t, LANES), F32), pltpu.VMEM((t, D), F32)],
        compiler_params=_cp(("parallel", "parallel", "parallel")),
        name="fox_prompt_attn",
    )(qb, kb, vb, ck)


def _fox_decode_kernel(pt_ref, q_ref, kn_ref, vn_ref, lfn_ref, *refs, npg):
    k_refs, v_refs, lf_refs = refs[0:npg], refs[npg:2 * npg], refs[2 * npg:3 * npg]
    o_ref = refs[3 * npg]
    m_s, l_s, acc_s, off_s, cp_s = refs[3 * npg + 1:]
    p = pl.program_id(1)
    H, D = q_ref.shape
    T = k_refs[0].shape[0]
    pages = range(npg)

    @pl.when(p == 0)
    def _():
        m_s[...] = jnp.full(m_s.shape, -jnp.inf, F32)
        l_s[...] = jnp.zeros(l_s.shape, F32)
        acc_s[...] = jnp.zeros(acc_s.shape, F32)
        off_s[...] = jnp.zeros(off_s.shape, F32)
        cp_s[...] = jnp.zeros(cp_s.shape, F32)

    q = q_ref[...] * (D ** -0.5 * LOG2E)
    hmask = (lax.broadcasted_iota(jnp.int32, (H, LANES), 0) == lax.broadcasted_iota(jnp.int32, (H, LANES), 1)).astype(F32)
    hmask2 = hmask * LOG2E
    trow = lax.broadcasted_iota(jnp.int32, (T, LANES), 0)
    for g in pages:
        cp_s[g, :, 0:H] = lf_refs[g][...]
    cps = [cp_s[g] for g in pages]
    sh = 1
    while sh < T:
        cps = [x + jnp.where(trow >= sh, pltpu.roll(x, sh, 0), 0.0) for x in cps]
        sh *= 2
    for g in pages:
        cp_s[g] = cps[g]
    cbm = [jnp.stack([jnp.broadcast_to(cp_s[g, t:t + 1, :], (H, LANES)) for t in range(T)], axis=0) * hmask2[None]
           for g in pages]
    s = [jnp.sum(k_refs[g][...] * q[None] - cbm[g], axis=-1, keepdims=True) for g in pages]
    tot = [jnp.sum(jnp.broadcast_to(cp_s[g, T - 1:T, :], (H, LANES)) * hmask, axis=-1, keepdims=True) for g in pages]
    mg = [jnp.max(s[g], axis=0) for g in pages]
    pr = [jnp.exp2(s[g] - mg[g][None]) for g in pages]
    lg = [jnp.sum(pr[g], axis=0) for g in pages]
    ag = [jnp.sum(pr[g] * v_refs[g][...], axis=0) for g in pages]
    off = off_s[:, 0:1]
    m_prev = m_s[:, 0:1]
    offs = []
    off_new = off
    for g in pages:
        offs.append(off_new * LOG2E)
        off_new = off_new + tot[g]
    m_new = m_prev
    for g in pages:
        m_new = jnp.maximum(m_new, mg[g] - offs[g])
    alpha = jnp.exp2(m_prev - m_new)
    l_new = alpha * l_s[:, 0:1]
    acc_new = alpha * acc_s[...]
    for g in pages:
        wg = jnp.exp2(mg[g] - offs[g] - m_new)
        l_new = l_new + wg * lg[g]
        acc_new = acc_new + wg * ag[g]
    m_s[...] = jnp.broadcast_to(m_new, m_s.shape)
    l_s[...] = jnp.broadcast_to(l_new, l_s.shape)
    acc_s[...] = acc_new
    off_s[...] = jnp.broadcast_to(off_new, off_s.shape)

    @pl.when(p == pl.num_programs(1) - 1)
    def _():
        cq = (off_new + lfn_ref[:, 0:1]) * LOG2E
        s_self = jnp.sum(q * kn_ref[...], axis=-1, keepdims=True) - cq
        m_f = jnp.maximum(m_new, s_self)
        a2 = jnp.exp2(m_new - m_f)
        ps = jnp.exp2(s_self - m_f)
        o_ref[...] = (a2 * acc_new + ps * vn_ref[...]) / (a2 * l_new + ps)


def fox_decode_attn(q, k_new, v_new, lf_new, cache_k, cache_v, cache_lf, page_table):
    B, H, D = q.shape
    n_pages = page_table.shape[1]
    T = cache_k.shape[1]
    npg = DEC_PAGES_PER_STEP if n_pages % DEC_PAGES_PER_STEP == 0 else 1
    vec = pl.BlockSpec((None, H, D), lambda b, p, pt: (b, 0, 0))
    kv = lambda g: pl.BlockSpec((None, T, H, D), lambda b, p, pt: (pt[b, p * npg + g], 0, 0, 0))
    lfs = lambda g: pl.BlockSpec((None, T, H), lambda b, p, pt: (pt[b, p * npg + g], 0, 0))
    grid_spec = pltpu.PrefetchScalarGridSpec(
        num_scalar_prefetch=1,
        grid=(B, n_pages // npg),
        in_specs=[vec, vec, vec, vec] + [kv(g) for g in range(npg)] * 2 + [lfs(g) for g in range(npg)],
        out_specs=pl.BlockSpec((None, H, D), lambda b, p, pt: (b, 0, 0)),
        scratch_shapes=[pltpu.VMEM((H, D), F32)] * 4 + [pltpu.VMEM((npg, T, LANES), F32)],
    )
    return pl.pallas_call(
        functools.partial(_fox_decode_kernel, npg=npg),
        grid_spec=grid_spec,
        out_shape=jax.ShapeDtypeStruct((B, H, D), F32),
        compiler_params=_cp(("parallel", "arbitrary")),
        name="fox_decode_attn",
    )(page_table, q, k_new, v_new, lf_new, *([cache_k] * npg), *([cache_v] * npg), *([cache_lf] * npg))


def _pad_cols(a, width):
    return jnp.pad(a, [(0, 0)] * (a.ndim - 1) + [(0, width - a.shape[-1])])


def _split_mix(a):
    W, H, D = M_WIDTH, M_HEADS, a.shape[-1]
    o = 0
    parts = {}
    for name, n in (("mqk", 2 * W), ("mv", W), ("mi", H), ("mf", H), ("mo", W), ("mz", W)):
        parts[name] = a[..., o:o + n]
        o += n
    Wr = (D - o - 2 * R_LORA) // 4
    for name, n in (("rr", Wr), ("rk", Wr), ("rv", Wr), ("rw", R_LORA), ("ra", R_LORA), ("rz", Wr)):
        parts[name] = a[..., o:o + n]
        o += n
    return parts


def _small_mix(p):
    z = jnp.zeros(p["rw"].shape[:-1] + (LANES - R_LORA,), p["rw"].dtype)
    return _pad_cols(jnp.concatenate([p["rw"], z, p["ra"], z, p["mi"], p["mf"]], axis=-1), SMALL_W)


def _arrange_mix(a):
    p = _split_mix(a)
    return jnp.concatenate([p["mqk"], p["mv"], p["mo"], p["mz"], p["rr"], p["rk"], p["rv"], p["rz"], _small_mix(p)],
                           axis=-1)


def _trunk(x, p_in, state, attn, Wt, tm):
    B, T, D = x.shape
    M = B * T
    x2 = x.reshape(M, D)
    tm_in = 4 * tm if M % (4 * tm) == 0 else tm
    tm_w = min(tm, 256)
    Wr = D
    proj = matmul_bf16(norm_cast(x2, Wt["norm_w"][0], tm_w), Wt["mix_w"], tm_in,
                       _col_tile(Wt["mix_w"].shape[1], D, M))
    W = M_WIDTH
    small0 = 3 * W + 2 * W + 4 * Wr
    gi = proj[:, small0 + 2 * LANES: small0 + 2 * LANES + M_HEADS]
    gf = proj[:, small0 + 2 * LANES + M_HEADS: small0 + 2 * LANES + 2 * M_HEADS]
    r_col0 = 5 * W

    if state is None:
        bm, C, n, m = mlstm_prompt(proj, B, T, gi.reshape(B, T, M_HEADS), gf.reshape(B, T, M_HEADS),
                                   Wt["m_conv_w"], Wt["m_conv_b"], Wt["m_b_i"], Wt["m_b_f"], Wt["m_norm"])
        conv_new = proj.reshape(B, T, -1)[:, T - (CONV_WIDTH - 1):, 0:2 * W]
        C = C
        n = n.reshape(B, M_HEADS, M_HEAD_DIM)
        m = m.reshape(B, M_HEADS)
        br, Sbd = rwkv_prompt(proj, B, T, Wt["mu4"], Wt["mu2"], Wt["r_par"], Wt["w2p"], Wt["a2p"],
                              r_col0 // LANES)
        N = R_HEAD_DIM
        S_new = jnp.stack([Sbd[:, :, 0:N, 0:N], Sbd[:, :, N:2 * N, N:2 * N]], axis=2).reshape(B, Wr // N, N, N)
    else:
        conv0, C0, n0, m0, S0, sh0 = state
        H, Dh = M_HEADS, M_HEAD_DIM
        bm, conv_new, C, n, m = mlstm_step(
            proj[:, 0:2 * W].reshape(B, 2 * H, Dh), conv0.reshape(B, CONV_WIDTH - 1, 2 * H, Dh),
            Wt["m_conv_w"].reshape(CONV_WIDTH, 2 * H, Dh), Wt["m_conv_b"].reshape(2 * H, Dh),
            proj[:, 2 * W:3 * W].reshape(B, H, Dh), gi.reshape(B, H, 1), gf.reshape(B, H, 1),
            Wt["m_b_i"].reshape(H, 1), Wt["m_b_f"].reshape(H, 1),
            proj[:, 3 * W:4 * W].reshape(B, H, Dh), proj[:, 4 * W:5 * W].reshape(B, H, Dh),
            Wt["m_norm"].reshape(H, Dh), C0, n0, m0.reshape(B, H, 1))
        bm = bm.reshape(M, W)
        conv_new = conv_new.reshape(B, CONV_WIDTH - 1, 2 * W)
        m = m.reshape(B, H)
        shp = _split_mix(jnp.concatenate([jnp.zeros((B, 5 * W + 2 * M_HEADS), F32), sh0], axis=-1))
        sh4 = jnp.concatenate([shp["rr"], shp["rk"], shp["rv"], shp["rz"]], axis=-1)
        sh2 = _small_mix(shp)[:, 0:2 * LANES]
        N = R_HEAD_DIM
        Hr = Wr // N
        r_, k_, v_, z_, w_, a_ = rwkv_prep(proj, sh4, sh2, Wt["mu4_flat"],
                                           Wt["mu2_flat"], Wt["r_par"][0:1], Wt["r_par"][1:2], Wt["w2p"], Wt["a2p"],
                                           r_col0 // Wr, small0 // LANES)
        hs = lambda t: t.reshape(B, Hr, N)
        par5 = Wt["r_par"][2:7].reshape(5, Hr, N)
        br, S_new = rwkv_step(hs(r_), hs(k_), hs(v_), hs(z_), hs(w_), hs(a_), par5, S0)
        br = br.reshape(M, Wr)
    last = proj.reshape(B, T, -1)[:, T - 1]
    sh_new = jnp.concatenate([last[:, r_col0:r_col0 + 3 * Wr], last[:, small0:small0 + R_LORA],
                              last[:, small0 + LANES:small0 + LANES + R_LORA],
                              last[:, r_col0 + 3 * Wr:r_col0 + 4 * Wr]], axis=-1)

    h1 = out_mix(bm, br, Wt["out_mix_w"], x2, min(tm, 128))
    h1, xn1 = ple_add(h1, Wt["ple_norm"][0], Wt["ple_gate_w"][0], p_in[0].reshape(M, -1), Wt["ple_proj"][0],
                      Wt["norm_w"][1], tm_w, final=False)

    proj1 = matmul_bf16(xn1, Wt["fox_w"], tm_in, _col_tile(Wt["fox_w"].shape[1], D, M))
    kn, vv, lf, *qs = qk_norm(proj1, Wt["f_q_norm"], Wt["f_k_norm"], Wt["f_b_f_pad"], min(tm, 256), D,
                              4 * D // LANES, state is None)
    Hf = D // F_HEAD_DIM
    lf = lf[:, 0:Hf]
    o = attn(qs, kn, vv, lf, B, T)
    h2 = out_fox(o, proj1, 3, Wt["out_fox_w"], h1, tm_w)
    y = ple_add(h2, Wt["ple_norm"][1], Wt["ple_gate_w"][1], p_in[1].reshape(M, -1), Wt["ple_proj"][1],
                Wt["final_norm"], tm_w, final=True).reshape(B, T, D)

    k_rows = kn.reshape(1, B, T, Hf, F_HEAD_DIM)
    v_rows = vv.reshape(1, B, T, Hf, F_HEAD_DIM)
    lf_rows = lf.reshape(1, B, T, Hf)
    mix_state = tuple(t[None] for t in (conv_new, C, n, m, S_new, sh_new))
    return y, (k_rows, v_rows, lf_rows), mix_state


def kernel(x_prompt, x_sample, cache_k, cache_v, cache_lf, state_mlstm_conv, state_mlstm_C, state_mlstm_n,
           state_mlstm_m, state_rwkv_S, state_rwkv_shift, page_table, p_prompt, p_sample,
           norm_w, final_norm, w_in_mix, w_out_mix, m_conv_w, m_conv_b, m_b_i, m_b_f, m_norm,
           r_mu, r_w0, r_w2, r_a0, r_a2, r_k_k, r_k_a, r_r_k, r_gn_w, r_gn_b,
           w_in_fox, w_out_fox, f_b_f, f_q_norm, f_k_norm, ple_proj, ple_gate_w, ple_norm):
    D = x_prompt.shape[-1]
    Wr = r_w0.shape[-1]
    Hf = D // F_HEAD_DIM
    mix_w = _arrange_mix(w_in_mix[0]).astype(BF16)
    fw = w_in_fox[0]
    fox_w = jnp.concatenate([fw[:, 0:3 * D], fw[:, 3 * D + Hf:4 * D + Hf], _pad_cols(fw[:, 3 * D:3 * D + Hf], SMALL_W)],
                            axis=-1).astype(BF16)
    mu_full = jnp.concatenate([jnp.zeros((5 * M_WIDTH + 2 * M_HEADS,), F32), r_mu[0]])
    mup = _split_mix(mu_full)
    mu4_flat = jnp.concatenate([mup["rr"], mup["rk"], mup["rv"], mup["rz"]])[None]
    mu2_flat = _small_mix(mup)[None, 0:2 * LANES]
    mu4 = _pad_cols(jnp.stack([mup["rr"], mup["rk"], mup["rv"], mup["rz"]]).T, 8).T
    mu2 = _pad_cols(mu2_flat.reshape(2, LANES).T, 8).T
    r_par = jnp.stack([r_w0[0], r_a0[0], r_k_k[0], r_k_a[0], r_r_k[0], r_gn_w[0], r_gn_b[0], jnp.zeros((Wr,), F32)])
    pad_rows = lambda a: jnp.pad(a, ((0, LANES - a.shape[0]), (0, 0)))
    Wt = dict(mix_w=mix_w, fox_w=fox_w, norm_w=norm_w, final_norm=final_norm,
              out_mix_w=w_out_mix[0].astype(BF16), out_fox_w=w_out_fox[0].astype(BF16),
              m_conv_w=m_conv_w[0], m_conv_b=m_conv_b[0], m_b_i=m_b_i[0], m_b_f=m_b_f[0], m_norm=m_norm[0],
              mu4=mu4, mu2=mu2, mu4_flat=mu4_flat, mu2_flat=mu2_flat, r_par=r_par,
              w2p=pad_rows(r_w2[0]), a2p=pad_rows(r_a2[0]),
              f_q_norm=f_q_norm[0], f_k_norm=f_k_norm[0], f_b_f_pad=_pad_cols(f_b_f[0][None], LANES),
              ple_proj=ple_proj.astype(BF16), ple_gate_w=ple_gate_w.astype(BF16), ple_norm=ple_norm)

    def attn_prompt(qs, kn, vv, lf, B, T):
        lfT = lf.reshape(B, T, Hf).transpose(0, 2, 1)
        c = cumsum_lanes(lfT)
        t = FOX_BLOCK if T % FOX_BLOCK == 0 else T
        return fox_prompt_attn(*qs, c.reshape(B, Hf, T // t, 1, t), B, T, t)

    def attn_sample(qs, kn, vv, lf, B, T):
        r3 = lambda t: t.reshape(B, Hf, F_HEAD_DIM)
        lfb = jnp.broadcast_to(lf.reshape(B, Hf, 1), (B, Hf, F_HEAD_DIM))
        o = fox_decode_attn(r3(qs[0]), r3(kn), r3(vv), lfb, cache_k[0], cache_v[0], cache_lf[0], page_table)
        return o.reshape(B, D)

    Bp, Tp, _ = x_prompt.shape
    y_p, (k_p, v_p, lf_p), mix_p = _trunk(x_prompt, p_prompt.reshape(p_prompt.shape[0], Bp * Tp, -1), None,
                                          attn_prompt, Wt, 512)
    Bs, Ts, _ = x_sample.shape
    state = (state_mlstm_conv[0], state_mlstm_C[0], state_mlstm_n[0], state_mlstm_m[0], state_rwkv_S[0],
             state_rwkv_shift[0])
    y_s, (k_s, v_s, lf_s), mix_s = _trunk(x_sample, p_sample.reshape(p_sample.shape[0], Bs * Ts, -1), state,
                                          attn_sample, Wt, Bs * Ts)
    return (y_p, y_s, k_p, v_p, lf_p) + mix_p + (k_s, v_s, lf_s) + mix_s
```

```python
import functools

import jax
import jax.numpy as jnp
from jax import lax
from jax.experimental import pallas as pl
from jax.experimental.pallas import tpu as pltpu

F32 = jnp.float32
BF16 = jnp.bfloat16
HI = lax.Precision.HIGHEST

NORM_EPS = 1e-6
RWKV_GN_EPS = 64e-5
M_HEADS = 8
M_HEAD_DIM = 256
M_WIDTH = M_HEADS * M_HEAD_DIM
CONV_WIDTH = 4
M_CHUNK = 128
M_STEP_HEADS = 8
R_HEAD_DIM = 64
R_LORA = 64
R_CHUNK = 64
R_STEP_TOKENS = 512
F_HEAD_DIM = 128
DEC_PAGES_PER_STEP = 8
LOG2E = 1.4426950408889634
FOX_BLOCK = 1024
LANES = 128
SMALL_W = 512
VMEM_LIMIT = 48 * 1024 * 1024
W_BLOCK_BUDGET = 40 * 1024 * 1024


def _cp(sem, vmem=VMEM_LIMIT):
    return pltpu.CompilerParams(dimension_semantics=sem, vmem_limit_bytes=vmem)


def _dot(a, b, prec=None):
    return jnp.dot(a, b, preferred_element_type=F32, precision=prec)


def _dot_nt(a, b, prec=None):
    return lax.dot_general(a, b, (((1,), (1,)), ((), ())), preferred_element_type=F32, precision=prec)


def _dot_tn(a, b, prec=None):
    return lax.dot_general(a, b, (((0,), (0,)), ((), ())), preferred_element_type=F32, precision=prec)


def _split_bf16(x):
    hi = x.astype(BF16)
    return hi, (x - hi.astype(F32)).astype(BF16)


def _lhs3(x, axis):
    hi, lo = _split_bf16(x)
    return jnp.concatenate([hi, hi, lo], axis=axis)


def _rhs3(x, axis):
    hi, lo = _split_bf16(x)
    return jnp.concatenate([hi, lo, hi], axis=axis)


def _dot3(a, b):
    return _dot(_lhs3(a, 1), _rhs3(b, 0))


def _dot3_nt(a, b):
    return _dot_nt(_lhs3(a, 1), _rhs3(b, 1))


def _dot3_tn(a, b):
    return _dot_tn(_lhs3(a, 0), _rhs3(b, 0))


def _softplus(y):
    return jnp.maximum(y, 0.0) + jnp.log1p(jnp.exp(-jnp.abs(y)))


def _log_sigmoid(x):
    return -_softplus(-x)


def _silu(x):
    return x * jax.nn.sigmoid(x)


def _eye(n):
    return (lax.broadcasted_iota(jnp.int32, (n, n), 0) == lax.broadcasted_iota(jnp.int32, (n, n), 1)).astype(F32)


def _rms_bf16(x, g):
    return (x * lax.rsqrt(jnp.mean(x * x, axis=-1, keepdims=True) + NORM_EPS) * g).astype(BF16)


def _norm_cast_kernel(x_ref, g_ref, o_ref):
    o_ref[...] = _rms_bf16(x_ref[...], g_ref[...])


def norm_cast(x, g, tm):
    M, K = x.shape
    return pl.pallas_call(
        _norm_cast_kernel,
        grid=(M // tm,),
        in_specs=[pl.BlockSpec((tm, K), lambda i: (i, 0)), pl.BlockSpec((1, K), lambda i: (0, 0))],
        out_specs=pl.BlockSpec((tm, K), lambda i: (i, 0)),
        out_shape=jax.ShapeDtypeStruct((M, K), BF16),
        compiler_params=_cp(("parallel",)),
        name="norm_cast",
    )(x, g.reshape(1, K))


def _col_tile(N, K, M):
    if M >= 512:
        return 512
    fits = [d * LANES for d in range(1, N // LANES + 1)
            if (N // LANES) % d == 0 and 2 * 2 * K * d * LANES <= W_BLOCK_BUDGET]
    return max(fits + [512])


def _mm_kernel(x_ref, w_ref, o_ref):
    o_ref[...] = _dot(x_ref[...], w_ref[...])


def matmul_bf16(x, w, tm, tn):
    M, K = x.shape
    N = w.shape[1]
    return pl.pallas_call(
        _mm_kernel,
        grid=(M // tm, N // tn),
        in_specs=[pl.BlockSpec((tm, K), lambda i, j: (i, 0)),
                  pl.BlockSpec((K, tn), lambda i, j: (0, j))],
        out_specs=pl.BlockSpec((tm, tn), lambda i, j: (i, j)),
        out_shape=jax.ShapeDtypeStruct((M, N), F32),
        compiler_params=_cp(("parallel", "parallel")),
        name="in_proj",
    )(x, w)


def _out_mix_kernel(a_ref, b_ref, wa_ref, wb_ref, res_ref, o_ref):
    o_ref[...] = res_ref[...] + (_dot(a_ref[...].astype(BF16), wa_ref[...]) + _dot(b_ref[...].astype(BF16), wb_ref[...]))


def out_mix(a, b, w, res, tm):
    M, K = a.shape
    N = w.shape[1]
    rows = lambda n: pl.BlockSpec((tm, n), lambda i: (i, 0))
    return pl.pallas_call(
        _out_mix_kernel,
        grid=(M // tm,),
        in_specs=[rows(K), rows(K),
                  pl.BlockSpec((K, N), lambda i: (0, 0)),
                  pl.BlockSpec((K, N), lambda i: (1, 0)),
                  rows(N)],
        out_specs=rows(N),
        out_shape=jax.ShapeDtypeStruct((M, N), F32),
        compiler_params=_cp(("parallel",)),
        name="out_mix",
    )(a, b, w, w, res)


def _out_fox_kernel(o_in_ref, z_ref, w_ref, res_ref, o_ref):
    g = (o_in_ref[...] * _silu(z_ref[...])).astype(BF16)
    o_ref[...] = res_ref[...] + _dot(g, w_ref[...])


def out_fox(o, proj, z_blk, w, res, tm):
    M, K = o.shape
    N = w.shape[1]
    return pl.pallas_call(
        _out_fox_kernel,
        grid=(M // tm,),
        in_specs=[pl.BlockSpec((tm, K), lambda i: (i, 0)),
                  pl.BlockSpec((tm, K), lambda i: (i, z_blk)),
                  pl.BlockSpec((K, N), lambda i: (0, 0)),
                  pl.BlockSpec((tm, N), lambda i: (i, 0))],
        out_specs=pl.BlockSpec((tm, N), lambda i: (i, 0)),
        out_shape=jax.ShapeDtypeStruct((M, N), F32),
        compiler_params=_cp(("parallel",)),
        name="out_fox",
    )(o, proj, w, res)


def _ple_kernel(h_ref, g_ref, wg_ref, p_ref, pw_ref, ng_ref, *outs, final):
    x = h_ref[...]
    gate = jax.nn.sigmoid(_dot(_rms_bf16(x, g_ref[...]), wg_ref[...]))
    e = _dot(p_ref[...].astype(BF16), pw_ref[...])
    out = x + gate * e
    if final:
        (y_ref,) = outs
        y_ref[...] = out * lax.rsqrt(jnp.mean(out * out, axis=-1, keepdims=True) + NORM_EPS) * ng_ref[...]
    else:
        o_ref, xn_ref = outs
        o_ref[...] = out
        xn_ref[...] = _rms_bf16(out, ng_ref[...])


def ple_add(h, g, wg, p, pw, next_g, tm, final):
    M, K = h.shape
    N = wg.shape[1]
    P = p.shape[1]
    rows = pl.BlockSpec((tm, N), lambda i: (i, 0))
    vec = pl.BlockSpec((1, K), lambda i: (0, 0))
    f32_rows = jax.ShapeDtypeStruct((M, N), F32)
    return pl.pallas_call(
        functools.partial(_ple_kernel, final=final),
        grid=(M // tm,),
        in_specs=[pl.BlockSpec((tm, K), lambda i: (i, 0)), vec,
                  pl.BlockSpec((K, N), lambda i: (0, 0)),
                  pl.BlockSpec((tm, P), lambda i: (i, 0)),
                  pl.BlockSpec((P, N), lambda i: (0, 0)), vec],
        out_specs=rows if final else [rows, rows],
        out_shape=f32_rows if final else [f32_rows, jax.ShapeDtypeStruct((M, N), BF16)],
        compiler_params=_cp(("parallel",)),
        name="ple_add",
    )(h, g.reshape(1, K), wg, p, pw, next_g.reshape(1, N))


def _mlstm_chunk_kernel(qp_ref, kp_ref, v_ref, mo_ref, mz_ref, g_ref, gb_ref, cwq_ref, cwk_ref, mn_ref,
                        out_ref, C_out, n_out, m_out,
                        C_s, n_s, m_s, qbuf, kbuf):
    c = pl.program_id(2)
    nc = pl.num_programs(2)
    L = qp_ref.shape[0]
    HH = C_s.shape[0]
    Dh = qp_ref.shape[1] // HH
    heads = range(HH)
    hs = [slice(h * Dh, (h + 1) * Dh) for h in heads]

    @pl.when(c == 0)
    def _():
        C_s[...] = jnp.zeros(C_s.shape, F32)
        n_s[...] = jnp.zeros(n_s.shape, F32)
        m_s[...] = jnp.zeros(m_s.shape, F32)
        qbuf[0:8, :] = jnp.zeros((8, HH * Dh), F32)
        kbuf[0:8, :] = jnp.zeros((8, HH * Dh), F32)

    qbuf[8:8 + L, :] = qp_ref[...]
    kbuf[8:8 + L, :] = kp_ref[...]

    def conv(buf, w_ref):
        acc = w_ref[CONV_WIDTH:CONV_WIDTH + 1, :]
        for j in range(CONV_WIDTH):
            acc = acc + buf[8 - (CONV_WIDTH - 1) + j: 8 - (CONV_WIDTH - 1) + j + L, :] * w_ref[j:j + 1, :]
        return acc

    q = _silu(conv(qbuf, cwq_ref))
    k = _silu(conv(kbuf, cwk_ref)) * (Dh ** -0.5)
    qbuf[0:8, :] = qbuf[L:L + 8, :]
    kbuf[0:8, :] = kbuf[L:L + 8, :]
    v = v_ref[...]

    g = [g_ref[h] + gb_ref[h] for h in heads]
    li_r = [g[h][0:1, :] for h in heads]
    lf_r = [_log_sigmoid(g[h][1:2, :]) for h in heads]
    row = lax.broadcasted_iota(jnp.int32, (L, L), 0)
    col = lax.broadcasted_iota(jnp.int32, (L, L), 1)
    tri = row >= col
    diag = row == col
    b_c = [jnp.sum(jnp.where(tri, lf_r[h], 0.0), axis=1, keepdims=True) for h in heads]
    b_r = [jnp.sum(jnp.where(diag, b_c[h], 0.0), axis=0, keepdims=True) for h in heads]
    li_c = [jnp.sum(jnp.where(diag, li_r[h], 0.0), axis=1, keepdims=True) for h in heads]
    m_prev = [m_s[h] for h in heads]
    dmat = [jnp.where(tri, b_c[h] - b_r[h] + li_r[h], -jnp.inf) for h in heads]
    inter = [b_c[h] + m_prev[h] for h in heads]
    m_t = [jnp.maximum(inter[h], jnp.max(dmat[h], axis=1, keepdims=True)) for h in heads]
    w_intra = [jnp.exp(dmat[h] - m_t[h]) for h in heads]
    w_inter = [jnp.exp(inter[h] - m_t[h]) for h in heads]
    qb = [q[:, hs[h]].astype(BF16) for h in heads]
    kb = [k[:, hs[h]].astype(BF16) for h in heads]
    vb = [v[:, hs[h]].astype(BF16) for h in heads]
    C = [C_s[h] for h in heads]
    n_row = [n_s[h] for h in heads]
    s = [_dot_nt(qb[h], kb[h]) * w_intra[h] for h in heads]
    qC = [_dot_nt(qb[h], C[h].astype(BF16)) for h in heads]
    num = [_dot(s[h].astype(BF16), vb[h]) + w_inter[h] * qC[h] for h in heads]
    den = [jnp.sum(s[h], axis=1, keepdims=True)
           + w_inter[h] * jnp.sum(q[:, hs[h]] * n_row[h], axis=1, keepdims=True) for h in heads]
    hval = [num[h] / jnp.maximum(jnp.abs(den[h]), jnp.exp(-m_t[h])) for h in heads]
    m_new = [m_t[h][L - 1:L, :] for h in heads]
    bL = [b_c[h][L - 1:L, :] for h in heads]
    cd = [jnp.exp(bL[h] + m_prev[h] - m_new[h]) for h in heads]
    w_c = [jnp.exp(bL[h] - b_c[h] + li_c[h] - m_new[h]) for h in heads]
    w_r = [jnp.exp(bL[h] - b_r[h] + li_r[h] - m_new[h]) for h in heads]
    C_new = [cd[h] * C[h] + _dot_tn((v[:, hs[h]] * w_c[h]).astype(BF16), kb[h]) for h in heads]
    n_new = [cd[h] * n_row[h] + _dot(w_r[h].astype(BF16), kb[h]) for h in heads]
    hn = [hval[h] * lax.rsqrt(jnp.mean(hval[h] * hval[h], axis=-1, keepdims=True) + NORM_EPS) for h in heads]
    for h in heads:
        C_s[h] = C_new[h]
        n_s[h] = n_new[h]
        m_s[h] = m_new[h]
    out_ref[...] = jnp.concatenate(hn, axis=1) * mn_ref[...] * jax.nn.sigmoid(mo_ref[...]) * _silu(mz_ref[...])

    @pl.when(c == nc - 1)
    def _():
        for h in heads:
            C_out[h] = C_new[h]
            n_out[h] = n_new[h]
            m_out[h] = m_new[h]


def mlstm_prompt(proj, B, S, gi, gf, conv_w, conv_b, b_i, b_f, m_norm):
    H, Dh, L = M_HEADS, M_HEAD_DIM, M_CHUNK
    nc = S // L
    M = B * S
    g2 = jnp.stack([gi, gf], axis=-1).reshape(B, nc, L, H, 2).transpose(0, 3, 1, 4, 2)
    gb = jnp.stack([b_i, b_f], axis=-1).reshape(H, 2, 1)
    cw = jnp.concatenate([conv_w, conv_b.reshape(1, -1)], axis=0)
    HH = M_STEP_HEADS
    Wb = HH * Dh
    pblk = lambda off: pl.BlockSpec((L, Wb), lambda b, h, c: (b * nc + c, off + h))
    hb = M_WIDTH // Wb
    outs = pl.pallas_call(
        _mlstm_chunk_kernel,
        grid=(B, H // HH, nc),
        in_specs=[pblk(0), pblk(hb), pblk(2 * hb), pblk(3 * hb), pblk(4 * hb),
                  pl.BlockSpec((None, HH, None, 2, L), lambda b, h, c: (b, h, c, 0, 0)),
                  pl.BlockSpec((HH, 2, 1), lambda b, h, c: (h, 0, 0)),
                  pl.BlockSpec((CONV_WIDTH + 1, Wb), lambda b, h, c: (0, h)),
                  pl.BlockSpec((CONV_WIDTH + 1, Wb), lambda b, h, c: (0, hb + h)),
                  pl.BlockSpec((1, Wb), lambda b, h, c: (0, h))],
        out_specs=[pl.BlockSpec((L, Wb), lambda b, h, c: (b * nc + c, h)),
                   pl.BlockSpec((None, HH, Dh, Dh), lambda b, h, c: (b, h, 0, 0)),
                   pl.BlockSpec((None, HH, 1, Dh), lambda b, h, c: (b, h, 0, 0)),
                   pl.BlockSpec((None, HH, 1, 1), lambda b, h, c: (b, h, 0, 0))],
        out_shape=[jax.ShapeDtypeStruct((M, M_WIDTH), F32),
                   jax.ShapeDtypeStruct((B, H, Dh, Dh), F32),
                   jax.ShapeDtypeStruct((B, H, 1, Dh), F32),
                   jax.ShapeDtypeStruct((B, H, 1, 1), F32)],
        scratch_shapes=[pltpu.VMEM((HH, Dh, Dh), F32), pltpu.VMEM((HH, 1, Dh), F32), pltpu.VMEM((HH, 1, 1), F32),
                        pltpu.VMEM((L + 8, Wb), F32), pltpu.VMEM((L + 8, Wb), F32)],
        compiler_params=_cp(("parallel", "parallel", "arbitrary")),
        name="mlstm_prompt",
    )(proj, proj, proj, proj, proj, g2, gb, cw, cw, m_norm.reshape(1, -1))
    return outs


def _mlstm_step_kernel(x_ref, conv0_ref, cw_ref, cb_ref, v_ref, gi_ref, gf_ref, bi_ref, bf_ref,
                       mo_ref, mz_ref, mn_ref, C_ref, n_ref, m_ref,
                       out_ref, conv_out, C_out, n_out, m_out):
    H, Dh = v_ref.shape
    x = x_ref[...]
    qk = cb_ref[...]
    for j in range(CONV_WIDTH - 1):
        qk = qk + conv0_ref[j] * cw_ref[j]
    qk = qk + x * cw_ref[CONV_WIDTH - 1]
    for j in range(CONV_WIDTH - 2):
        conv_out[j] = conv0_ref[j + 1]
    conv_out[CONV_WIDTH - 2] = x
    qk = _silu(qk)
    q = qk[0:H]
    k = qk[H:2 * H] * (Dh ** -0.5)
    v = v_ref[...]
    li = gi_ref[...] + bi_ref[...]
    lf = _log_sigmoid(gf_ref[...] + bf_ref[...])
    m0 = m_ref[...]
    inter = lf + m0
    m_t = jnp.maximum(inter, li)
    w_intra = jnp.exp(li - m_t)
    w_inter = jnp.exp(inter - m_t)
    s = jnp.sum(q * k, axis=-1, keepdims=True) * w_intra
    n0 = n_ref[...]
    hrow = lax.broadcasted_iota(jnp.int32, (H, Dh), 0)
    cq = jnp.zeros((H, Dh), F32)
    vT = _dot_nt(_eye(Dh), v * w_intra, HI)
    Cs = [C_ref[h] for h in range(H)]
    cqs = [_dot_nt(q, Cs[h], HI) for h in range(H)]
    for h in range(H):
        cq = jnp.where(hrow == h, cqs[h], cq)
        C_out[h] = w_inter[h:h + 1, :] * Cs[h] + vT[:, h:h + 1] * k[h:h + 1, :]
    num = s * v + w_inter * cq
    den = s + w_inter * jnp.sum(n0 * q, axis=-1, keepdims=True)
    hval = num / jnp.maximum(jnp.abs(den), jnp.exp(-m_t))
    n_out[...] = w_inter * n0 + w_intra * k
    m_out[...] = m_t
    hn = hval * lax.rsqrt(jnp.mean(hval * hval, axis=-1, keepdims=True) + NORM_EPS) * mn_ref[...]
    out_ref[...] = hn * jax.nn.sigmoid(mo_ref[...]) * _silu(mz_ref[...])


def mlstm_step(x16, conv0, conv_w, conv_b, v, gi, gf, b_i, b_f, mo, mz, m_norm, C0, n0, m0):
    B = x16.shape[0]
    H, Dh = M_HEADS, M_HEAD_DIM
    W = CONV_WIDTH
    per_b = lambda *shape: pl.BlockSpec((None,) + shape, lambda b: (b,) + (0,) * len(shape))
    const = lambda *shape: pl.BlockSpec(shape, lambda b: (0,) * len(shape))
    return pl.pallas_call(
        _mlstm_step_kernel,
        grid=(B,),
        in_specs=[per_b(2 * H, Dh), per_b(W - 1, 2 * H, Dh), const(W, 2 * H, Dh), const(2 * H, Dh),
                  per_b(H, Dh), per_b(H, 1), per_b(H, 1), const(H, 1), const(H, 1),
                  per_b(H, Dh), per_b(H, Dh), const(H, Dh),
                  per_b(H, Dh, Dh), per_b(H, Dh), per_b(H, 1)],
        out_specs=[per_b(H, Dh), per_b(W - 1, 2 * H, Dh), per_b(H, Dh, Dh), per_b(H, Dh), per_b(H, 1)],
        out_shape=[jax.ShapeDtypeStruct((B, H, Dh), F32),
                   jax.ShapeDtypeStruct((B, W - 1, 2 * H, Dh), F32),
                   jax.ShapeDtypeStruct((B, H, Dh, Dh), F32),
                   jax.ShapeDtypeStruct((B, H, Dh), F32),
                   jax.ShapeDtypeStruct((B, H, 1), F32)],
        compiler_params=_cp(("parallel",)),
        name="mlstm_step",
    )(x16, conv0, conv_w, conv_b, v, gi, gf, b_i, b_f, mo, mz, m_norm, C0, n0, m0)


def _rwkv_chunk_kernel(rr_ref, rk_ref, rv_ref, rz_ref, rw_ref, ra_ref, mu4_ref, mu2_ref, par_ref,
                       w2_ref, a2_ref, out_ref, S_out, S_s, carry_s):
    c = pl.program_id(2)
    nc = pl.num_programs(2)
    TB = rr_ref.shape[0]
    W = rr_ref.shape[1]
    N = R_HEAD_DIM
    L = R_CHUNK

    @pl.when(c == 0)
    def _():
        S_s[...] = jnp.zeros(S_s.shape, F32)
        carry_s[...] = jnp.zeros(carry_s.shape, F32)

    row = lax.broadcasted_iota(jnp.int32, (TB, W), 0)
    lo = lax.broadcasted_iota(jnp.int32, (TB, W), 1) < N

    def shift(x, idx, mu):
        prev = jnp.where(row == 0, carry_s[idx:idx + 1, :], pltpu.roll(x, 1, 0))
        carry_s[idx:idx + 1, :] = x[TB - 1:TB, :]
        return x + mu * (prev - x)

    r = shift(rr_ref[...], 0, mu4_ref[0:1, :])
    k = shift(rk_ref[...], 1, mu4_ref[1:2, :])
    v = shift(rv_ref[...], 2, mu4_ref[2:3, :])
    z = shift(rz_ref[...], 3, mu4_ref[3:4, :])
    xw = shift(rw_ref[...], 4, mu2_ref[0:1, :])
    xa = shift(ra_ref[...], 5, mu2_ref[1:2, :])
    w0, a0 = par_ref[0:1, :], par_ref[1:2, :]
    k_k, k_a, r_k = par_ref[2:3, :], par_ref[3:4, :], par_ref[4:5, :]
    gn_w, gn_b = par_ref[5:6, :], par_ref[6:7, :]

    def seg_sum(x):
        s0 = jnp.sum(jnp.where(lo, x, 0.0), axis=1, keepdims=True)
        s1 = jnp.sum(jnp.where(lo, 0.0, x), axis=1, keepdims=True)
        return jnp.where(lo, s0, s1)

    w_log = -_softplus(-(w0 + _dot3(jnp.tanh(xw), w2_ref[...]))) - 0.5
    logw = -jnp.exp(w_log)
    a = jax.nn.sigmoid(a0 + _dot3(xa, a2_ref[...]))
    kk = k * k_k
    kk = kk * lax.rsqrt(jnp.maximum(seg_sum(kk * kk), 1e-24))
    k2 = k * (1.0 + (a - 1.0) * k_a)
    alpha = -kk
    beta = kk * a

    tr = lax.broadcasted_iota(jnp.int32, (TB, TB), 0)
    tc = lax.broadcasted_iota(jnp.int32, (TB, TB), 1)
    in_chunk = jnp.logical_and(tc <= tr, tc >= tr - jnp.bitwise_and(tr, L - 1))
    l_hi = logw.astype(BF16)
    l_r = logw - l_hi.astype(F32)
    l_mid = l_r.astype(BF16)
    l_lo = (l_r - l_mid.astype(F32)).astype(BF16)
    b3 = _dot(jnp.where(in_chunk, 1.0, 0.0).astype(BF16), jnp.concatenate([l_hi, l_mid, l_lo], axis=1))
    b = (b3[:, 0:W] + b3[:, W:2 * W]) + b3[:, 2 * W:3 * W]

    lo_c = lax.broadcasted_iota(jnp.int32, (L, W), 1) < N

    def stack(x):
        return jnp.concatenate([jnp.where(lo_c, x, 0.0), jnp.where(lo_c, 0.0, x)], axis=0)

    ri = lax.broadcasted_iota(jnp.int32, (2 * L, 2 * L), 0)
    ci = lax.broadcasted_iota(jnp.int32, (2 * L, 2 * L), 1)
    same = jnp.where(ri >= L, 1, 0) == jnp.where(ci >= L, 1, 0)
    strict = jnp.logical_and(same, ci < ri)
    incl = jnp.logical_and(same, ci <= ri)
    eye = jnp.where(ri == ci, 1.0, 0.0)

    chunks = range(TB // L)
    sls = [slice(ch * L, (ch + 1) * L) for ch in chunks]
    bLs = [b[sl][L - 1:L, :] for sl in sls]
    a_s = [stack(alpha[sl] * jnp.exp(b[sl] - logw[sl])) for sl in sls]
    r_s = [stack(r[sl] * jnp.exp(b[sl])) for sl in sls]
    Ys_ = [jnp.concatenate([stack(beta[sl] * jnp.exp(-b[sl])), stack(k2[sl] * jnp.exp(-b[sl]))], axis=0) for sl in sls]
    bh_s = [stack(beta[sl] * jnp.exp(bL - b[sl])) for sl, bL in zip(sls, bLs)]
    kh_s = [stack(k2[sl] * jnp.exp(bL - b[sl])) for sl, bL in zip(sls, bLs)]
    v_s = [stack(v[sl]) for sl in sls]
    Gs = [_dot3_nt(a_s[i], Ys_[i]) for i in chunks]
    Go = [_dot_nt(r_s[i].astype(BF16), Ys_[i].astype(BF16)) for i in chunks]
    A = [jnp.where(strict, Gs[i][:, 0:2 * L], 0.0) for i in chunks]
    Bm = [jnp.where(strict, Gs[i][:, 2 * L:4 * L], 0.0) for i in chunks]
    P = [jnp.where(incl, Go[i][:, 0:2 * L], 0.0).astype(BF16) for i in chunks]
    Q = [jnp.where(incl, Go[i][:, 2 * L:4 * L], 0.0).astype(BF16) for i in chunks]
    BV = [_dot3(Bm[i], v_s[i]) for i in chunks]
    QV = [_dot(Q[i], v_s[i].astype(BF16)) for i in chunks]
    T = [eye + A[i] for i in chunks]
    Ap = [_dot3(A[i], A[i]) for i in chunks]
    for _ in range(4):
        R = [_dot(_lhs3(Ap[i], 1), _rhs3(jnp.concatenate([Ap[i], T[i]], axis=1), 0)) for i in chunks]
        T = [T[i] + R[i][:, 2 * L:4 * L] for i in chunks]
        Ap = [R[i][:, 0:2 * L] for i in chunks]
    T = [T[i] + _dot3(Ap[i], T[i]) for i in chunks]
    TXB = [_dot3(T[i], jnp.concatenate([a_s[i], BV[i]], axis=1)) for i in chunks]
    PTX = [_dot(P[i], TXB[i].astype(BF16)) for i in chunks]
    MN = [_dot3_tn(TXB[i], bh_s[i]) for i in chunks]
    VK = [_dot3_tn(v_s[i], kh_s[i]) for i in chunks]
    pre = [((r_s[i] + PTX[i][:, 0:W]).astype(BF16), QV[i] + PTX[i][:, W:2 * W],
            MN[i][0:W], MN[i][W:2 * W] + VK[i], jnp.exp(bLs[i])) for i in chunks]

    S = S_s[...]
    ys = []
    for Rp, Y0, Mc, Nc, gL in pre:
        Ys = _dot_nt(Rp, S.astype(BF16)) + Y0
        ys.append(Ys[0:L] + Ys[L:2 * L])
        S = (S * gL + _dot3(S, Mc)) + Nc
    S_new = S
    S_s[...] = S_new
    y = jnp.concatenate(ys, axis=0)

    mu_y = seg_sum(y) * (1.0 / N)
    d = y - mu_y
    var = seg_sum(d * d) * (1.0 / N)
    yn = d * lax.rsqrt(var + RWKV_GN_EPS) * gn_w + gn_b
    yn = yn + seg_sum(r * k2 * r_k) * v
    out_ref[...] = yn * _silu(z)

    @pl.when(c == nc - 1)
    def _():
        S_out[...] = S_new


def rwkv_prompt(proj, B, S, mu4, mu2, par, w2p, a2p, r_off):
    L = R_STEP_TOKENS
    W = 2 * R_HEAD_DIM
    nc = S // L
    M = B * S
    npair = mu4.shape[1] // W
    blk = lambda off: pl.BlockSpec((L, W), lambda b, p, c: (b * nc + c, off + p))
    cblk = lambda off: pl.BlockSpec((L, W), lambda b, p, c: (b * nc + c, off))
    return pl.pallas_call(
        _rwkv_chunk_kernel,
        grid=(B, npair, nc),
        in_specs=[blk(r_off), blk(r_off + npair), blk(r_off + 2 * npair), blk(r_off + 3 * npair),
                  cblk(r_off + 4 * npair), cblk(r_off + 4 * npair + 1),
                  pl.BlockSpec((8, W), lambda b, p, c: (0, p)),
                  pl.BlockSpec((8, W), lambda b, p, c: (0, 0)),
                  pl.BlockSpec((8, W), lambda b, p, c: (0, p)),
                  pl.BlockSpec((W, W), lambda b, p, c: (0, p)),
                  pl.BlockSpec((W, W), lambda b, p, c: (0, p))],
        out_specs=[pl.BlockSpec((L, W), lambda b, p, c: (b * nc + c, p)),
                   pl.BlockSpec((None, None, W, W), lambda b, p, c: (b, p, 0, 0))],
        out_shape=[jax.ShapeDtypeStruct((M, npair * W), F32),
                   jax.ShapeDtypeStruct((B, npair, W, W), F32)],
        scratch_shapes=[pltpu.VMEM((W, W), F32), pltpu.VMEM((8, W), F32)],
        compiler_params=_cp(("parallel", "parallel", "arbitrary")),
        name="rwkv_prompt",
    )(proj, proj, proj, proj, proj, proj, mu4, mu2, par, w2p, a2p)


def _rwkv_prep_kernel(rr_ref, rk_ref, rv_ref, rz_ref, rw_ref, ra_ref, sh4_ref, sh2_ref, mu4_ref, mu2_ref,
                      w0_ref, a0_ref, w2_ref, a2_ref, r_o, k_o, v_o, z_o, w_o, a_o):
    Wd = rr_ref.shape[1]
    sh = lambda x, prev, mu: x + mu * (prev - x)
    r_o[...] = sh(rr_ref[...], sh4_ref[:, 0:Wd], mu4_ref[:, 0:Wd])
    k_o[...] = sh(rk_ref[...], sh4_ref[:, Wd:2 * Wd], mu4_ref[:, Wd:2 * Wd])
    v_o[...] = sh(rv_ref[...], sh4_ref[:, 2 * Wd:3 * Wd], mu4_ref[:, 2 * Wd:3 * Wd])
    z_o[...] = sh(rz_ref[...], sh4_ref[:, 3 * Wd:4 * Wd], mu4_ref[:, 3 * Wd:4 * Wd])
    xw = sh(rw_ref[...], sh2_ref[:, 0:LANES], mu2_ref[:, 0:LANES])
    xa = sh(ra_ref[...], sh2_ref[:, LANES:2 * LANES], mu2_ref[:, LANES:2 * LANES])
    w_log = -_softplus(-(w0_ref[...] + _dot(jnp.tanh(xw), w2_ref[...], HI))) - 0.5
    w_o[...] = jnp.exp(-jnp.exp(w_log))
    a_o[...] = jax.nn.sigmoid(a0_ref[...] + _dot(xa, a2_ref[...], HI))


def rwkv_prep(proj, sh4, sh2, mu4, mu2, w0, a0, w2p, a2p, r_blk, s_blk):
    B = proj.shape[0]
    Wd = w0.shape[1]
    full = lambda a: pl.BlockSpec(a.shape, lambda i: (0,) * a.ndim)
    return pl.pallas_call(
        _rwkv_prep_kernel,
        grid=(1,),
        in_specs=[pl.BlockSpec((B, Wd), lambda i: (0, r_blk)), pl.BlockSpec((B, Wd), lambda i: (0, r_blk + 1)),
                  pl.BlockSpec((B, Wd), lambda i: (0, r_blk + 2)), pl.BlockSpec((B, Wd), lambda i: (0, r_blk + 3)),
                  pl.BlockSpec((B, LANES), lambda i: (0, s_blk)), pl.BlockSpec((B, LANES), lambda i: (0, s_blk + 1)),
                  full(sh4), full(sh2), full(mu4), full(mu2), full(w0), full(a0), full(w2p), full(a2p)],
        out_specs=[pl.BlockSpec((B, Wd), lambda i: (0, 0))] * 6,
        out_shape=[jax.ShapeDtypeStruct((B, Wd), F32)] * 6,
        compiler_params=_cp(("arbitrary",)),
        name="rwkv_prep",
    )(proj, proj, proj, proj, proj, proj, sh4, sh2, mu4, mu2, w0, a0, w2p, a2p)


def _rwkv_step_kernel(r_ref, k_ref, v_ref, z_ref, w_ref, a_ref, par_ref, S_ref, out_ref, S_out):
    H, N = r_ref.shape
    r, k, v, z, w, a = (t[...] for t in (r_ref, k_ref, v_ref, z_ref, w_ref, a_ref))
    k_k, k_a, r_k, gn_w, gn_b = (par_ref[i] for i in range(5))
    kk = k * k_k
    kk = kk * lax.rsqrt(jnp.maximum(jnp.sum(kk * kk, axis=-1, keepdims=True), 1e-24))
    k2 = k * (1.0 + (a - 1.0) * k_a)
    alpha = -kk
    beta = kk * a
    vT = _dot_nt(_eye(N), v, HI)
    lane_h = lax.broadcasted_iota(jnp.int32, (N, H), 1)
    heads = range(H)
    S = [S_ref[h] for h in heads]
    sa = [jnp.sum(S[h] * alpha[h:h + 1, :], axis=-1, keepdims=True) for h in heads]
    Sn = [S[h] * w[h:h + 1, :] + sa[h] * beta[h:h + 1, :] + vT[:, h:h + 1] * k2[h:h + 1, :] for h in heads]
    ys = [jnp.sum(Sn[h] * r[h:h + 1, :], axis=-1, keepdims=True) for h in heads]
    yT = jnp.zeros((N, H), F32)
    for h in heads:
        S_out[h] = Sn[h]
        yT = jnp.where(lane_h == h, ys[h], yT)
    y = _dot_nt(_eye(H), yT, HI)
    mu_y = jnp.mean(y, axis=-1, keepdims=True)
    d = y - mu_y
    var = jnp.mean(d * d, axis=-1, keepdims=True)
    yn = d * lax.rsqrt(var + RWKV_GN_EPS) * gn_w + gn_b
    yn = yn + jnp.sum(r * k2 * r_k, axis=-1, keepdims=True) * v
    out_ref[...] = yn * _silu(z)


def rwkv_step(r, k, v, z, w, a, par, S0):
    B, H, N = r.shape
    vec = pl.BlockSpec((None, H, N), lambda b: (b, 0, 0))
    mat = pl.BlockSpec((None, H, N, N), lambda b: (b, 0, 0, 0))
    return pl.pallas_call(
        _rwkv_step_kernel,
        grid=(B,),
        in_specs=[vec] * 6 + [pl.BlockSpec(par.shape, lambda b: (0, 0, 0)), mat],
        out_specs=[vec, mat],
        out_shape=[jax.ShapeDtypeStruct((B, H, N), F32), jax.ShapeDtypeStruct((B, H, N, N), F32)],
        compiler_params=_cp(("parallel",)),
        name="rwkv_step",
    )(r, k, v, z, w, a, par, S0)


def _qk_norm_kernel(q_ref, k_ref, v_ref, f_ref, gq_ref, gk_ref, bf_ref, kn_ref, v_out_ref, lf_ref, *q_refs,
                    mxu_copies):
    D = F_HEAD_DIM
    for h in range(q_ref.shape[1] // D):
        sl = slice(h * D, (h + 1) * D)
        x = q_ref[:, sl]
        qn = x * lax.rsqrt(jnp.mean(x * x, axis=-1, keepdims=True) + NORM_EPS) * gq_ref[...]
        y = k_ref[:, sl]
        kn = y * lax.rsqrt(jnp.mean(y * y, axis=-1, keepdims=True) + NORM_EPS) * gk_ref[...]
        v = v_ref[:, sl]
        kn_ref[:, sl] = kn
        v_out_ref[:, sl] = v
        if mxu_copies:
            qb_ref, kb_ref, vb_ref = q_refs
            qb_ref[:, sl] = (qn * (D ** -0.5 * LOG2E)).astype(BF16)
            kb_ref[:, sl] = kn.astype(BF16)
            vb_ref[:, sl] = v.astype(BF16)
        else:
            q_refs[0][:, sl] = qn
    lf_ref[...] = _log_sigmoid(f_ref[...] + bf_ref[...])


def qk_norm(proj, gq, gk, bf_pad, tm, Wd, f_blk, mxu_copies):
    M = proj.shape[0]
    wide = pl.BlockSpec((tm, Wd), lambda i: (i, 0))
    wide_sds = lambda dt: jax.ShapeDtypeStruct((M, Wd), dt)
    extra = [wide_sds(BF16)] * 3 if mxu_copies else [wide_sds(F32)]
    return pl.pallas_call(
        functools.partial(_qk_norm_kernel, mxu_copies=mxu_copies),
        grid=(M // tm,),
        in_specs=[pl.BlockSpec((tm, Wd), lambda i: (i, 0)), pl.BlockSpec((tm, Wd), lambda i: (i, 1)),
                  pl.BlockSpec((tm, Wd), lambda i: (i, 2)),
                  pl.BlockSpec((tm, LANES), lambda i: (i, f_blk)),
                  pl.BlockSpec((1, F_HEAD_DIM), lambda i: (0, 0)), pl.BlockSpec((1, F_HEAD_DIM), lambda i: (0, 0)),
                  pl.BlockSpec((1, LANES), lambda i: (0, 0))],
        out_specs=[wide, wide, pl.BlockSpec((tm, LANES), lambda i: (i, 0))] + [wide] * len(extra),
        out_shape=[wide_sds(F32), wide_sds(F32), jax.ShapeDtypeStruct((M, LANES), F32)] + extra,
        compiler_params=_cp(("parallel",)),
        name="qk_norm",
    )(proj, proj, proj, proj, gq.reshape(1, -1), gk.reshape(1, -1), bf_pad)


def _cumsum_kernel(x_ref, c_ref):
    H, S = x_ref.shape
    U = (lax.broadcasted_iota(jnp.int32, (LANES, LANES), 0) <= lax.broadcasted_iota(jnp.int32, (LANES, LANES), 1)).astype(F32)
    off = jnp.zeros((H, 1), F32)
    for t in range(S // LANES):
        sl = slice(t * LANES, (t + 1) * LANES)
        w = _dot(x_ref[:, sl], U, HI) + off
        c_ref[:, sl] = w
        off = w[:, LANES - 1:LANES]


def cumsum_lanes(x):
    B, H, S = x.shape
    return pl.pallas_call(
        _cumsum_kernel,
        grid=(B,),
        in_specs=[pl.BlockSpec((None, H, S), lambda b: (b, 0, 0))],
        out_specs=pl.BlockSpec((None, H, S), lambda b: (b, 0, 0)),
        out_shape=jax.ShapeDtypeStruct((B, H, S), F32),
        compiler_params=_cp(("parallel",)),
        name="forget_cumsum",
    )(x)


def _fox_flash_kernel(q_ref, k_ref, v_ref, ck_ref, o_ref, m_s, l_s, acc_s):
    i = pl.program_id(2)
    t, D = q_ref.shape
    q = q_ref[...]
    m_s[...] = jnp.full(m_s.shape, -jnp.inf, F32)
    l_s[...] = jnp.zeros(l_s.shape, F32)
    acc_s[...] = jnp.zeros(acc_s.shape, F32)

    def block(j, masked):
        start = pl.multiple_of(j * t, t)
        s = _dot_nt(q, k_ref[pl.ds(start, t), :]) - ck_ref[j] * LOG2E
        if masked:
            row = lax.broadcasted_iota(jnp.int32, (t, t), 0)
            col = lax.broadcasted_iota(jnp.int32, (t, t), 1)
            s = jnp.where(col <= row, s, -jnp.inf)
        m_prev = m_s[...]
        m_new = jnp.maximum(m_prev, jnp.max(s, axis=1, keepdims=True))
        alpha = jnp.exp2(m_prev - m_new)
        p = jnp.exp2(s - jnp.concatenate([m_new] * (t // LANES), axis=1))
        l_s[...] = alpha * l_s[...] + jnp.sum(p, axis=1, keepdims=True)
        acc_s[...] = alpha * acc_s[...] + _dot(p.astype(BF16), v_ref[pl.ds(start, t), :])
        m_s[...] = m_new

    def body(j, carry):
        block(j, False)
        return carry

    lax.fori_loop(0, i, body, 0)
    block(i, True)
    o_ref[...] = acc_s[...] / l_s[...]


def fox_prompt_attn(qb, kb, vb, ck, B, S, t):
    D = F_HEAD_DIM
    H = qb.shape[1] // D
    nb = S // t
    full = pl.BlockSpec((S, D), lambda b, h, i: (b, h))
    return pl.pallas_call(
        _fox_flash_kernel,
        grid=(B, H, nb),
        in_specs=[pl.BlockSpec((t, D), lambda b, h, i: (b * nb + i, h)), full, full,
                  pl.BlockSpec((None, None, nb, 1, t), lambda b, h, i: (b, h, 0, 0, 0))],
        out_specs=pl.BlockSpec((t, D), lambda b, h, i: (b * nb + i, h)),
        out_shape=jax.ShapeDtypeStruct((B * S, H * D), F32),
        scratch_shapes=[pltpu.VMEM((t, LANES), F32), pltpu.VMEM((t, LANES), F32), pltpu.VMEM((t, D), F32)],
        compiler_params=_cp(("parallel", "parallel", "parallel")),
        name="fox_prompt_attn",
    )(qb, kb, vb, ck)


def _fox_decode_kernel(pt_ref, q_ref, kn_ref, vn_ref, lfn_ref, *refs, npg):
    k_refs, v_refs, lf_refs = refs[0:npg], refs[npg:2 * npg], refs[2 * npg:3 * npg]
    o_ref = refs[3 * npg]
    m_s, l_s, acc_s, off_s, cp_s = refs[3 * npg + 1:]
    p = pl.program_id(1)
    H, D = q_ref.shape
    T = k_refs[0].shape[0]
    pages = range(npg)

    @pl.when(p == 0)
    def _():
        m_s[...] = jnp.full(m_s.shape, -jnp.inf, F32)
        l_s[...] = jnp.zeros(l_s.shape, F32)
        acc_s[...] = jnp.zeros(acc_s.shape, F32)
        off_s[...] = jnp.zeros(off_s.shape, F32)
        cp_s[...] = jnp.zeros(cp_s.shape, F32)

    q = q_ref[...] * (D ** -0.5 * LOG2E)
    hmask = (lax.broadcasted_iota(jnp.int32, (H, LANES), 0) == lax.broadcasted_iota(jnp.int32, (H, LANES), 1)).astype(F32)
    hmask2 = hmask * LOG2E
    trow = lax.broadcasted_iota(jnp.int32, (T, LANES), 0)
    for g in pages:
        cp_s[g, :, 0:H] = lf_refs[g][...]
    cps = [cp_s[g] for g in pages]
    sh = 1
    while sh < T:
        cps = [x + jnp.where(trow >= sh, pltpu.roll(x, sh, 0), 0.0) for x in cps]
        sh *= 2
    for g in pages:
        cp_s[g] = cps[g]
    cbm = [jnp.stack([jnp.broadcast_to(cp_s[g, t:t + 1, :], (H, LANES)) for t in range(T)], axis=0) * hmask2[None]
           for g in pages]
    s = [jnp.sum(k_refs[g][...] * q[None] - cbm[g], axis=-1, keepdims=True) for g in pages]
    tot = [jnp.sum(jnp.broadcast_to(cp_s[g, T - 1:T, :], (H, LANES)) * hmask, axis=-1, keepdims=True) for g in pages]
    mg = [jnp.max(s[g], axis=0) for g in pages]
    pr = [jnp.exp2(s[g] - mg[g][None]) for g in pages]
    lg = [jnp.sum(pr[g], axis=0) for g in pages]
    ag = [jnp.sum(pr[g] * v_refs[g][...], axis=0) for g in pages]
    off = off_s[:, 0:1]
    m_prev = m_s[:, 0:1]
    offs = []
    off_new = off
    for g in pages:
        offs.append(off_new * LOG2E)
        off_new = off_new + tot[g]
    m_new = m_prev
    for g in pages:
        m_new = jnp.maximum(m_new, mg[g] - offs[g])
    alpha = jnp.exp2(m_prev - m_new)
    l_new = alpha * l_s[:, 0:1]
    acc_new = alpha * acc_s[...]
    for g in pages:
        wg = jnp.exp2(mg[g] - offs[g] - m_new)
        l_new = l_new + wg * lg[g]
        acc_new = acc_new + wg * ag[g]
    m_s[...] = jnp.broadcast_to(m_new, m_s.shape)
    l_s[...] = jnp.broadcast_to(l_new, l_s.shape)
    acc_s[...] = acc_new
    off_s[...] = jnp.broadcast_to(off_new, off_s.shape)

    @pl.when(p == pl.num_programs(1) - 1)
    def _():
        cq = (off_new + lfn_ref[:, 0:1]) * LOG2E
        s_self = jnp.sum(q * kn_ref[...], axis=-1, keepdims=True) - cq
        m_f = jnp.maximum(m_new, s_self)
        a2 = jnp.exp2(m_new - m_f)
        ps = jnp.exp2(s_self - m_f)
        o_ref[...] = (a2 * acc_new + ps * vn_ref[...]) / (a2 * l_new + ps)


def fox_decode_attn(q, k_new, v_new, lf_new, cache_k, cache_v, cache_lf, page_table):
    B, H, D = q.shape
    n_pages = page_table.shape[1]
    T = cache_k.shape[1]
    npg = DEC_PAGES_PER_STEP if n_pages % DEC_PAGES_PER_STEP == 0 else 1
    vec = pl.BlockSpec((None, H, D), lambda b, p, pt: (b, 0, 0))
    kv = lambda g: pl.BlockSpec((None, T, H, D), lambda b, p, pt: (pt[b, p * npg + g], 0, 0, 0))
    lfs = lambda g: pl.BlockSpec((None, T, H), lambda b, p, pt: (pt[b, p * npg + g], 0, 0))
    grid_spec = pltpu.PrefetchScalarGridSpec(
        num_scalar_prefetch=1,
        grid=(B, n_pages // npg),
        in_specs=[vec, vec, vec, vec] + [kv(g) for g in range(npg)] * 2 + [lfs(g) for g in range(npg)],
        out_specs=pl.BlockSpec((None, H, D), lambda b, p, pt: (b, 0, 0)),
        scratch_shapes=[pltpu.VMEM((H, D), F32)] * 4 + [pltpu.VMEM((npg, T, LANES), F32)],
    )
    return pl.pallas_call(
        functools.partial(_fox_decode_kernel, npg=npg),
        grid_spec=grid_spec,
        out_shape=jax.ShapeDtypeStruct((B, H, D), F32),
        compiler_params=_cp(("parallel", "arbitrary")),
        name="fox_decode_attn",
    )(page_table, q, k_new, v_new, lf_new, *([cache_k] * npg), *([cache_v] * npg), *([cache_lf] * npg))


def _pad_cols(a, width):
    return jnp.pad(a, [(0, 0)] * (a.ndim - 1) + [(0, width - a.shape[-1])])


def _split_mix(a):
    W, H, D = M_WIDTH, M_HEADS, a.shape[-1]
    o = 0
    parts = {}
    for name, n in (("mqk", 2 * W), ("mv", W), ("mi", H), ("mf", H), ("mo", W), ("mz", W)):
        parts[name] = a[..., o:o + n]
        o += n
    Wr = (D - o - 2 * R_LORA) // 4
    for name, n in (("rr", Wr), ("rk", Wr), ("rv", Wr), ("rw", R_LORA), ("ra", R_LORA), ("rz", Wr)):
        parts[name] = a[..., o:o + n]
        o += n
    return parts


def _small_mix(p):
    z = jnp.zeros(p["rw"].shape[:-1] + (LANES - R_LORA,), p["rw"].dtype)
    return _pad_cols(jnp.concatenate([p["rw"], z, p["ra"], z, p["mi"], p["mf"]], axis=-1), SMALL_W)


def _arrange_mix(a):
    p = _split_mix(a)
    return jnp.concatenate([p["mqk"], p["mv"], p["mo"], p["mz"], p["rr"], p["rk"], p["rv"], p["rz"], _small_mix(p)],
                           axis=-1)


def _trunk(x, p_in, state, attn, Wt, tm):
    B, T, D = x.shape
    M = B * T
    x2 = x.reshape(M, D)
    tm_in = 4 * tm if M % (4 * tm) == 0 else tm
    tm_w = min(tm, 256)
    Wr = D
    proj = matmul_bf16(norm_cast(x2, Wt["norm_w"][0], tm_w), Wt["mix_w"], tm_in,
                       _col_tile(Wt["mix_w"].shape[1], D, M))
    W = M_WIDTH
    small0 = 3 * W + 2 * W + 4 * Wr
    gi = proj[:, small0 + 2 * LANES: small0 + 2 * LANES + M_HEADS]
    gf = proj[:, small0 + 2 * LANES + M_HEADS: small0 + 2 * LANES + 2 * M_HEADS]
    r_col0 = 5 * W

    if state is None:
        bm, C, n, m = mlstm_prompt(proj, B, T, gi.reshape(B, T, M_HEADS), gf.reshape(B, T, M_HEADS),
                                   Wt["m_conv_w"], Wt["m_conv_b"], Wt["m_b_i"], Wt["m_b_f"], Wt["m_norm"])
        conv_new = proj.reshape(B, T, -1)[:, T - (CONV_WIDTH - 1):, 0:2 * W]
        C = C
        n = n.reshape(B, M_HEADS, M_HEAD_DIM)
        m = m.reshape(B, M_HEADS)
        br, Sbd = rwkv_prompt(proj, B, T, Wt["mu4"], Wt["mu2"], Wt["r_par"], Wt["w2p"], Wt["a2p"],
                              r_col0 // LANES)
        N = R_HEAD_DIM
        S_new = jnp.stack([Sbd[:, :, 0:N, 0:N], Sbd[:, :, N:2 * N, N:2 * N]], axis=2).reshape(B, Wr // N, N, N)
    else:
        conv0, C0, n0, m0, S0, sh0 = state
        H, Dh = M_HEADS, M_HEAD_DIM
        bm, conv_new, C, n, m = mlstm_step(
            proj[:, 0:2 * W].reshape(B, 2 * H, Dh), conv0.reshape(B, CONV_WIDTH - 1, 2 * H, Dh),
            Wt["m_conv_w"].reshape(CONV_WIDTH, 2 * H, Dh), Wt["m_conv_b"].reshape(2 * H, Dh),
            proj[:, 2 * W:3 * W].reshape(B, H, Dh), gi.reshape(B, H, 1), gf.reshape(B, H, 1),
            Wt["m_b_i"].reshape(H, 1), Wt["m_b_f"].reshape(H, 1),
            proj[:, 3 * W:4 * W].reshape(B, H, Dh), proj[:, 4 * W:5 * W].reshape(B, H, Dh),
            Wt["m_norm"].reshape(H, Dh), C0, n0, m0.reshape(B, H, 1))
        bm = bm.reshape(M, W)
        conv_new = conv_new.reshape(B, CONV_WIDTH - 1, 2 * W)
        m = m.reshape(B, H)
        shp = _split_mix(jnp.concatenate([jnp.zeros((B, 5 * W + 2 * M_HEADS), F32), sh0], axis=-1))
        sh4 = jnp.concatenate([shp["rr"], shp["rk"], shp["rv"], shp["rz"]], axis=-1)
        sh2 = _small_mix(shp)[:, 0:2 * LANES]
        N = R_HEAD_DIM
        Hr = Wr // N
        r_, k_, v_, z_, w_, a_ = rwkv_prep(proj, sh4, sh2, Wt["mu4_flat"],
                                           Wt["mu2_flat"], Wt["r_par"][0:1], Wt["r_par"][1:2], Wt["w2p"], Wt["a2p"],
                                           r_col0 // Wr, small0 // LANES)
        hs = lambda t: t.reshape(B, Hr, N)
        par5 = Wt["r_par"][2:7].reshape(5, Hr, N)
        br, S_new = rwkv_step(hs(r_), hs(k_), hs(v_), hs(z_), hs(w_), hs(a_), par5, S0)
        br = br.reshape(M, Wr)
    last = proj.reshape(B, T, -1)[:, T - 1]
    sh_new = jnp.concatenate([last[:, r_col0:r_col0 + 3 * Wr], last[:, small0:small0 + R_LORA],
                              last[:, small0 + LANES:small0 + LANES + R_LORA],
                              last[:, r_col0 + 3 * Wr:r_col0 + 4 * Wr]], axis=-1)

    h1 = out_mix(bm, br, Wt["out_mix_w"], x2, min(tm, 128))
    h1, xn1 = ple_add(h1, Wt["ple_norm"][0], Wt["ple_gate_w"][0], p_in[0].reshape(M, -1), Wt["ple_proj"][0],
                      Wt["norm_w"][1], tm_w, final=False)

    proj1 = matmul_bf16(xn1, Wt["fox_w"], tm_in, _col_tile(Wt["fox_w"].shape[1], D, M))
    kn, vv, lf, *qs = qk_norm(proj1, Wt["f_q_norm"], Wt["f_k_norm"], Wt["f_b_f_pad"], min(tm, 256), D,
                              4 * D // LANES, state is None)
    Hf = D // F_HEAD_DIM
    lf = lf[:, 0:Hf]
    o = attn(qs, kn, vv, lf, B, T)
    h2 = out_fox(o, proj1, 3, Wt["out_fox_w"], h1, tm_w)
    y = ple_add(h2, Wt["ple_norm"][1], Wt["ple_gate_w"][1], p_in[1].reshape(M, -1), Wt["ple_proj"][1],
                Wt["final_norm"], tm_w, final=True).reshape(B, T, D)

    k_rows = kn.reshape(1, B, T, Hf, F_HEAD_DIM)
    v_rows = vv.reshape(1, B, T, Hf, F_HEAD_DIM)
    lf_rows = lf.reshape(1, B, T, Hf)
    mix_state = tuple(t[None] for t in (conv_new, C, n, m, S_new, sh_new))
    return y, (k_rows, v_rows, lf_rows), mix_state


def kernel(x_prompt, x_sample, cache_k, cache_v, cache_lf, state_mlstm_conv, state_mlstm_C, state_mlstm_n,
           state_mlstm_m, state_rwkv_S, state_rwkv_shift, page_table, p_prompt, p_sample,
           norm_w, final_norm, w_in_mix, w_out_mix, m_conv_w, m_conv_b, m_b_i, m_b_f, m_norm,
           r_mu, r_w0, r_w2, r_a0, r_a2, r_k_k, r_k_a, r_r_k, r_gn_w, r_gn_b,
           w_in_fox, w_out_fox, f_b_f, f_q_norm, f_k_norm, ple_proj, ple_gate_w, ple_norm):
    D = x_prompt.shape[-1]
    Wr = r_w0.shape[-1]
    Hf = D // F_HEAD_DIM
    mix_w = _arrange_mix(w_in_mix[0]).astype(BF16)
    fw = w_in_fox[0]
    fox_w = jnp.concatenate([fw[:, 0:3 * D], fw[:, 3 * D + Hf:4 * D + Hf], _pad_cols(fw[:, 3 * D:3 * D + Hf], SMALL_W)],
                            axis=-1).astype(BF16)
    mu_full = jnp.concatenate([jnp.zeros((5 * M_WIDTH + 2 * M_HEADS,), F32), r_mu[0]])
    mup = _split_mix(mu_full)
    mu4_flat = jnp.concatenate([mup["rr"], mup["rk"], mup["rv"], mup["rz"]])[None]
    mu2_flat = _small_mix(mup)[None, 0:2 * LANES]
    mu4 = _pad_cols(jnp.stack([mup["rr"], mup["rk"], mup["rv"], mup["rz"]]).T, 8).T
    mu2 = _pad_cols(mu2_flat.reshape(2, LANES).T, 8).T
    r_par = jnp.stack([r_w0[0], r_a0[0], r_k_k[0], r_k_a[0], r_r_k[0], r_gn_w[0], r_gn_b[0], jnp.zeros((Wr,), F32)])
    pad_rows = lambda a: jnp.pad(a, ((0, LANES - a.shape[0]), (0, 0)))
    Wt = dict(mix_w=mix_w, fox_w=fox_w, norm_w=norm_w, final_norm=final_norm,
              out_mix_w=w_out_mix[0].astype(BF16), out_fox_w=w_out_fox[0].astype(BF16),
              m_conv_w=m_conv_w[0], m_conv_b=m_conv_b[0], m_b_i=m_b_i[0], m_b_f=m_b_f[0], m_norm=m_norm[0],
              mu4=mu4, mu2=mu2, mu4_flat=mu4_flat, mu2_flat=mu2_flat, r_par=r_par,
              w2p=pad_rows(r_w2[0]), a2p=pad_rows(r_a2[0]),
              f_q_norm=f_q_norm[0], f_k_norm=f_k_norm[0], f_b_f_pad=_pad_cols(f_b_f[0][None], LANES),
              ple_proj=ple_proj.astype(BF16), ple_gate_w=ple_gate_w.astype(BF16), ple_norm=ple_norm)

    def attn_prompt(qs, kn, vv, lf, B, T):
        lfT = lf.reshape(B, T, Hf).transpose(0, 2, 1)
        c = cumsum_lanes(lfT)
        t = FOX_BLOCK if T % FOX_BLOCK == 0 else T
        return fox_prompt_attn(*qs, c.reshape(B, Hf, T // t, 1, t), B, T, t)

    def attn_sample(qs, kn, vv, lf, B, T):
        r3 = lambda t: t.reshape(B, Hf, F_HEAD_DIM)
        lfb = jnp.broadcast_to(lf.reshape(B, Hf, 1), (B, Hf, F_HEAD_DIM))
        o = fox_decode_attn(r3(qs[0]), r3(kn), r3(vv), lfb, cache_k[0], cache_v[0], cache_lf[0], page_table)
        return o.reshape(B, D)

    Bp, Tp, _ = x_prompt.shape
    y_p, (k_p, v_p, lf_p), mix_p = _trunk(x_prompt, p_prompt.reshape(p_prompt.shape[0], Bp * Tp, -1), None,
                                          attn_prompt, Wt, 512)
    Bs, Ts, _ = x_sample.shape
    state = (state_mlstm_conv[0], state_mlstm_C[0], state_mlstm_n[0], state_mlstm_m[0], state_rwkv_S[0],
             state_rwkv_shift[0])
    y_s, (k_s, v_s, lf_s), mix_s = _trunk(x_sample, p_sample.reshape(p_sample.shape[0], Bs * Ts, -1), state,
                                          attn_sample, Wt, Bs * Ts)
    return (y_p, y_s, k_p, v_p, lf_p) + mix_p + (k_s, v_s, lf_s) + mix_s
```
